```python
import jax, jax.numpy as jnp
from jax import lax
import numpy as np

D_MODEL = 1024
BATCH = 8
SEQ = 4096
DEPTH = 4

N_MIXERS = 4
EPS = 1e-6
POOL_WINDOWS = (2, 4, 8, 16)
POOL_GROUPS = len(POOL_WINDOWS)
POOL_GROUP_DIM = D_MODEL // POOL_GROUPS
HGRN_HEAD_DIM = 128
HGRN_HEADS = D_MODEL // HGRN_HEAD_DIM
HGRN_CHUNK = 64
SWA_HEADS = 16
SWA_KV_HEADS = 4
SWA_HEAD_DIM = D_MODEL // SWA_HEADS
SWA_WINDOW = 128
SWA_QKV_DIM = (SWA_HEADS + 2 * SWA_KV_HEADS) * SWA_HEAD_DIM
CONV_WIDTH = 3
N_EXPERTS = 16
N_GROUPS = 4
EXPERTS_PER_GROUP = N_EXPERTS // N_GROUPS
TOP_K = 2
D_FF_EXPERT = 1024
MOE_BLOCK = 128

kernel_name = 'hybrid_interleaved_pool_hgrn2_swa_conv_moe'


def _n_occ(m):
    return len(range(m, DEPTH, N_MIXERS))


def rmsnorm(x, g):
    xf = x.astype(jnp.float32)
    y = xf * lax.rsqrt(jnp.mean(xf * xf, axis=-1, keepdims=True) + EPS)
    return (y * g.astype(jnp.float32)).astype(x.dtype)


def pool_mixer(h, w_in, w_grp, scale, w_out):
    B, T, D = h.shape
    u = (h @ w_in).astype(jnp.float32).reshape(B, T, POOL_GROUPS, POOL_GROUP_DIM)
    cs = jnp.concatenate([jnp.zeros_like(u[:, :1]), jnp.cumsum(u, axis=1)], axis=1)
    pos = jnp.arange(T)
    pooled = []
    for g, w in enumerate(POOL_WINDOWS):
        lo = jnp.maximum(pos + 1 - w, 0)
        cnt = jnp.minimum(pos + 1, w).astype(jnp.float32)
        pooled.append((cs[:, 1:, g] - cs[:, lo, g]) / cnt[None, :, None])
    pooled = jnp.stack(pooled, axis=2) - u
    y = jnp.einsum('btgc,gcd->btgd', pooled, w_grp.astype(jnp.float32)).reshape(B, T, D)
    y = y * scale.astype(jnp.float32)
    return y.astype(h.dtype) @ w_out


def hgrn2_chunked(q, k, v, log_f):
    B, H, T, dk = q.shape
    dv = v.shape[-1]
    N = T // HGRN_CHUNK

    def chunks(a):
        return a.reshape(B, H, N, HGRN_CHUNK, a.shape[-1]).transpose(2, 0, 1, 3, 4)

    causal = jnp.tril(jnp.ones((HGRN_CHUNK, HGRN_CHUNK), bool))[:, :, None]

    def step(S, inp):
        qc, kc, vc, gc = inp
        b = jnp.cumsum(gc, axis=-2)
        rel = jnp.exp(jnp.where(causal, b[..., :, None, :] - b[..., None, :, :], -jnp.inf))
        scores = jnp.einsum('bhtd,bhsd,bhtsd->bhts', qc, kc, rel)
        o = scores @ vc + jnp.einsum('bhtd,bhdv->bhtv', qc * jnp.exp(b), S)
        b_last = b[..., -1:, :]
        S = S * jnp.exp(b_last[:, :, 0, :, None]) + jnp.einsum(
            'bhsd,bhsv->bhdv', kc * jnp.exp(b_last - b), vc)
        return S, o

    S0 = jnp.zeros((B, H, dk, dv), jnp.float32)
    _, o = lax.scan(step, S0, (chunks(q), chunks(k), chunks(v), chunks(log_f)))
    return o.transpose(1, 2, 0, 3, 4).reshape(B, H, T, dv)


def hgrn2_mixer(h, w_in, lower_bound, norm_g, w_out):
    B, T, D = h.shape
    H, dk = HGRN_HEADS, HGRN_HEAD_DIM
    q, f_logit, v, g = jnp.split(h @ w_in, 4, axis=-1)

    def heads(a):
        return a.astype(jnp.float32).reshape(B, T, H, dk).transpose(0, 2, 1, 3)

    q = jax.nn.silu(heads(q))
    lb = lower_bound.astype(jnp.float32).reshape(H, dk)[None, :, None, :]
    f = lb + (1.0 - lb) * jax.nn.sigmoid(heads(f_logit))
    o = hgrn2_chunked(q, 1.0 - f, heads(v), jnp.log(f))
    o = rmsnorm(o, norm_g)
    o = o.transpose(0, 2, 1, 3).reshape(B, T, D) * jax.nn.silu(g.astype(jnp.float32))
    return o.astype(h.dtype) @ w_out


def swa_mixer(h, w_in, sinks, w_out):
    B, T, D = h.shape
    Hq, Hk, hd, W = SWA_HEADS, SWA_KV_HEADS, SWA_HEAD_DIM, SWA_WINDOW
    G = Hq // Hk
    nb = T // W
    q, k, v = jnp.split(h @ w_in, [Hq * hd, (Hq + Hk) * hd], axis=-1)
    q = q.astype(jnp.float32).reshape(B, nb, W, Hk, G, hd) * (hd ** -0.5)

    def kv_blocks(a):
        a = a.astype(jnp.float32).reshape(B, nb, W, Hk, hd)
        prev = jnp.concatenate([jnp.zeros_like(a[:, :1]), a[:, :-1]], axis=1)
        return jnp.concatenate([prev, a], axis=2)

    kb, vb = kv_blocks(k), kv_blocks(v)
    s = jnp.einsum('bnqhgd,bnkhd->bnhgqk', q, kb)
    qi = jnp.arange(W)[:, None]
    kj = jnp.arange(2 * W)[None, :]
    diff = qi + W - kj
    band = (diff >= 0) & (diff < W)
    mask = band[None] & ((jnp.arange(nb)[:, None, None] > 0) | (kj >= W)[None])
    s = jnp.where(mask[None, :, None, None], s, -jnp.inf)
    sink = sinks.astype(jnp.float32).reshape(Hk, G)[None, None, :, :, None, None]
    m = jnp.maximum(jnp.max(s, axis=-1, keepdims=True), sink)
    p = jnp.exp(s - m)
    denom = jnp.sum(p, axis=-1, keepdims=True) + jnp.exp(sink - m)
    o = jnp.einsum('bnhgqk,bnkhd->bnqhgd', p / denom, vb)
    return o.reshape(B, T, D).astype(h.dtype) @ w_out


def conv_mixer(h, w_in, conv_w, w_out):
    D = h.shape[-1]
    b_gate, c_gate, u = jnp.split(h @ w_in, 3, axis=-1)
    z = c_gate * u
    zc = lax.conv_general_dilated(
        z, conv_w[:, None, :].astype(z.dtype), window_strides=(1,),
        padding=[(CONV_WIDTH - 1, 0)], dimension_numbers=('NWC', 'WIO', 'NWC'),
        feature_group_count=D)
    return (b_gate * zc) @ w_out


def route(h2d, router_w, router_bias):
    scores = jax.nn.sigmoid((h2d @ router_w).astype(jnp.float32))
    sel = scores + router_bias.astype(jnp.float32)
    grp = sel.reshape(-1, N_GROUPS, EXPERTS_PER_GROUP)
    group_score = jnp.sum(lax.top_k(grp, TOP_K)[0], axis=-1)
    best = jnp.argmax(group_score, axis=-1)
    in_group = (jnp.arange(N_EXPERTS) // EXPERTS_PER_GROUP)[None, :] == best[:, None]
    _, idx = lax.top_k(jnp.where(in_group, sel, -jnp.inf), TOP_K)
    w = jnp.take_along_axis(scores, idx, axis=-1)
    return idx, w / jnp.sum(w, axis=-1, keepdims=True)


def moe_ffn(h2d, idx, w, w_gate, w_up, w_down):
    T, D = h2d.shape
    A = T * TOP_K
    flat_e = idx.reshape(A)
    order = jnp.argsort(flat_e)
    e_sorted = flat_e[order]
    tok = order // TOP_K
    counts = jnp.bincount(flat_e, length=N_EXPERTS)
    padded = (counts + MOE_BLOCK - 1) // MOE_BLOCK * MOE_BLOCK
    ends_pad = jnp.cumsum(padded)
    starts_pad = ends_pad - padded
    starts = jnp.cumsum(counts) - counts
    dest = starts_pad[e_sorted] + jnp.arange(A) - starts[e_sorted]
    n_blocks = -(-A // MOE_BLOCK) + N_EXPERTS
    slot_tok = jnp.full((n_blocks * MOE_BLOCK,), T, jnp.int32).at[dest].set(tok.astype(jnp.int32))
    x_pad = jnp.concatenate([h2d, jnp.zeros((1, D), h2d.dtype)], axis=0)
    xb = x_pad[slot_tok].reshape(n_blocks, MOE_BLOCK, D)
    block_e = jnp.minimum(
        jnp.searchsorted(ends_pad, jnp.arange(n_blocks) * MOE_BLOCK, side='right'), N_EXPERTS - 1)

    def expert_block(args):
        xblk, e = args
        a = jax.nn.silu(xblk @ w_gate[e]) * (xblk @ w_up[e])
        return a @ w_down[e]

    yb = lax.map(expert_block, (xb, block_e)).reshape(-1, D)
    y = yb[dest] * w.reshape(A)[order][:, None].astype(yb.dtype)
    return jnp.zeros((T, D), yb.dtype).at[tok].add(y)


def setup_inputs(seed: int = 0) -> dict:
    key = jax.random.key(seed)
    ks = jax.random.split(key, 24)
    D, F, E = D_MODEL, D_FF_EXPERT, N_EXPERTS
    nA, nB, nC, nD = _n_occ(0), _n_occ(1), _n_occ(2), _n_occ(3)
    n = lambda k, shape, s: jax.random.normal(k, shape, jnp.float32) * s
    return {
        'x': n(ks[0], (BATCH, SEQ, D), 1.0),
        'c': n(ks[1], (BATCH, D), 1.0),
        'ada_w': n(ks[2], (DEPTH, D, 6 * D), 0.5 * D ** -0.5),
        'ada_b': n(ks[3], (DEPTH, 6 * D), 0.02),
        'norm_g': 1.0 + n(ks[4], (DEPTH, 2, D), 0.05),
        'final_norm_g': 1.0 + n(ks[5], (D,), 0.05),
        'pool_w_in': n(ks[6], (nA, D, D), D ** -0.5),
        'pool_w_grp': n(ks[7], (nA, POOL_GROUPS, POOL_GROUP_DIM, POOL_GROUP_DIM), POOL_GROUP_DIM ** -0.5),
        'pool_scale': 1.0 + n(ks[8], (nA, D), 0.1),
        'pool_w_out': n(ks[9], (nA, D, D), D ** -0.5),
        'hgrn_w_in': n(ks[10], (nB, D, 4 * D), D ** -0.5),
        'hgrn_lb_logits': n(ks[11], (DEPTH, D), 0.1),
        'hgrn_norm_g': 1.0 + n(ks[12], (nB, HGRN_HEAD_DIM), 0.05),
        'hgrn_w_out': n(ks[13], (nB, D, D), D ** -0.5),
        'swa_w_in': n(ks[14], (nC, D, SWA_QKV_DIM), D ** -0.5),
        'swa_sinks': n(ks[15], (nC, SWA_HEADS), 0.5),
        'swa_w_out': n(ks[16], (nC, D, D), D ** -0.5),
        'conv_w_in': n(ks[17], (nD, D, 3 * D), D ** -0.5),
        'conv_w': n(ks[18], (nD, CONV_WIDTH, D), CONV_WIDTH ** -0.5),
        'conv_w_out': n(ks[19], (nD, D, D), D ** -0.5),
        'router_w': n(ks[20], (D, E), D ** -0.5),
        'router_bias': n(ks[21], (E,), 0.01),
        'moe_w_gate': n(ks[22], (DEPTH, E, D, F), D ** -0.5),
        'moe_w_up': n(jax.random.fold_in(ks[22], 1), (DEPTH, E, D, F), D ** -0.5),
        'moe_w_down': n(ks[23], (DEPTH, E, F, D), F ** -0.5),
    }


def reference(x, c, ada_w, ada_b, norm_g, final_norm_g,
              pool_w_in, pool_w_grp, pool_scale, pool_w_out,
              hgrn_w_in, hgrn_lb_logits, hgrn_norm_g, hgrn_w_out,
              swa_w_in, swa_sinks, swa_w_out,
              conv_w_in, conv_w, conv_w_out,
              router_w, router_bias, moe_w_gate, moe_w_up, moe_w_down):
    B, T, D = x.shape
    lb_all = jnp.cumsum(jax.nn.softmax(hgrn_lb_logits.astype(jnp.float32), axis=0), axis=0)
    lb_all = lb_all - lb_all[0]
    cond = jax.nn.silu(c)
    for i in range(DEPTH):
        m, j = i % N_MIXERS, i // N_MIXERS
        mod = cond @ ada_w[i] + ada_b[i]
        sh1, sc1, g1, sh2, sc2, g2 = [a[:, None, :] for a in jnp.split(mod, 6, axis=-1)]
        h = rmsnorm(x, norm_g[i, 0]) * (1.0 + sc1) + sh1
        if m == 0:
            y = pool_mixer(h, pool_w_in[j], pool_w_grp[j], pool_scale[j], pool_w_out[j])
        elif m == 1:
            y = hgrn2_mixer(h, hgrn_w_in[j], lb_all[i], hgrn_norm_g[j], hgrn_w_out[j])
        elif m == 2:
            y = swa_mixer(h, swa_w_in[j], swa_sinks[j], swa_w_out[j])
        else:
            y = conv_mixer(h, conv_w_in[j], conv_w[j], conv_w_out[j])
        x = x + g1 * y
        h = rmsnorm(x, norm_g[i, 1]) * (1.0 + sc2) + sh2
        h2d = h.reshape(B * T, D)
        idx, w = route(h2d, router_w, router_bias)
        x = x + g2 * moe_ffn(h2d, idx, w, moe_w_gate[i], moe_w_up[i], moe_w_down[i]).reshape(B, T, D)
    return rmsnorm(x, final_norm_g)
```

```python
import functools

import jax
import jax.numpy as jnp
from jax import lax
from jax.experimental import pallas as pl
from jax.experimental.pallas import tpu as pltpu

F32 = jnp.float32
BF16 = jnp.bfloat16
EPS = 1e-6

POOL_WINDOWS = (2, 4, 8, 16)
POOL_HALO = 16
HGRN_HEAD_DIM = 128
HGRN_CHUNK = 64
SWA_HEADS = 16
SWA_KV_HEADS = 4
SWA_WINDOW = 128
CONV_WIDTH = 3
CONV_HALO = 8
N_EXPERTS = 16
N_GROUPS = 4
EXPERTS_PER_GROUP = 4
PAIRS = ((0, 1), (0, 2), (0, 3), (1, 2), (1, 3), (2, 3))
N_CLASSES = N_GROUPS * len(PAIRS)
CLASS_ROWS = 32
LANES = 128
FFN_BLOCK = 256
PERM_CHUNK = 2048
VMEM_LIMIT = 52 * 1024 * 1024


def _cparams():
    return pltpu.CompilerParams(dimension_semantics=("arbitrary",), vmem_limit_bytes=VMEM_LIMIT)


def _full(shape):
    nd = len(shape)
    return pl.BlockSpec(shape, lambda i, *_: (0,) * nd)


def _norm_mod(x, g, sc, sh):
    ms = jnp.mean(x * x, axis=-1, keepdims=True)
    return x * lax.rsqrt(ms + EPS) * (g * (1.0 + sc)) + sh


def _sigmoid(x):
    return 1.0 / (1.0 + jnp.exp(-x))


def _silu(x):
    return x * _sigmoid(x)


def _dot(a, b):
    return jnp.dot(a.astype(BF16), b.astype(BF16), preferred_element_type=F32)


def _dot_nt(a, b):
    return lax.dot_general(a.astype(BF16), b.astype(BF16), (((1,), (1,)), ((), ())), preferred_element_type=F32)


def _dot_tn(a, b):
    return lax.dot_general(a.astype(BF16), b.astype(BF16), (((0,), (0,)), ((), ())), preferred_element_type=F32)


def _ada_kernel(c_ref, w_ref, b_ref, o_ref):
    cond = _silu(c_ref[...])
    o_ref[...] = jnp.dot(cond, w_ref[...], preferred_element_type=F32, precision=lax.Precision.HIGHEST) + b_ref[...]


def _ada(c, ada_w, ada_b):
    L, D, D6 = ada_w.shape
    B = c.shape[0]
    bn = D6 // 4
    return pl.pallas_call(
        _ada_kernel,
        grid=(L, D6 // bn),
        in_specs=[pl.BlockSpec((B, D), lambda l, j: (0, 0)),
                  pl.BlockSpec((None, D, bn), lambda l, j: (l, 0, j)),
                  pl.BlockSpec((None, 1, bn), lambda l, j: (l, 0, j))],
        out_specs=pl.BlockSpec((None, B, bn), lambda l, j: (l, 0, j)),
        out_shape=jax.ShapeDtypeStruct((L, B, D6), F32),
        compiler_params=pltpu.CompilerParams(dimension_semantics=("arbitrary", "arbitrary"),
                                             vmem_limit_bytes=VMEM_LIMIT),
        name="ada_mod",
    )(c, ada_w, ada_b.reshape(L, 1, D6))


def _mixer_call(body, name, tm, T, x, prev, mod, g, weights, scratch, smem=()):
    N, D = x.shape
    row = pl.BlockSpec((tm, D), lambda i: (i, 0))
    modspec = pl.BlockSpec((None, 6, D), lambda i: ((i * tm) // T, 0, 0))
    args, specs = [x], [row]
    if prev is not None:
        args += [prev[0], prev[1]]
        specs += [row, modspec]
    args += [mod, g.reshape(1, D)]
    specs += [modspec, _full((1, D))]
    for w in weights:
        args.append(w)
        specs.append(_full(w.shape))
    for s in smem:
        args.append(s)
        specs.append(pl.BlockSpec(memory_space=pltpu.SMEM))
    return pl.pallas_call(
        functools.partial(body, prev is not None, tm, T),
        grid=(N // tm,),
        in_specs=specs,
        out_specs=row,
        out_shape=jax.ShapeDtypeStruct((N, D), F32),
        scratch_shapes=scratch,
        compiler_params=_cparams(),
        name=name,
    )(*args)


def _mixer_input(has_prev, refs):
    if has_prev:
        x_ref, y_ref, pm_ref, m_ref, g_ref = refs[:5]
        x = x_ref[...] + pm_ref[5:6, :] * y_ref[...]
        rest = refs[5:]
    else:
        x_ref, m_ref, g_ref = refs[:3]
        x = x_ref[...]
        rest = refs[3:]
    return x, m_ref, g_ref, rest


def _pool_kernel(has_prev, tm, T, *refs):
    x, m_ref, g_ref, (win_ref, wgrp_ref, scale_ref, wout_ref, o_ref, tail_ref) = _mixer_input(has_prev, refs)
    i = pl.program_id(0)
    start = (i * tm) % T
    h = _norm_mod(x, g_ref[...], m_ref[1:2, :], m_ref[0:1, :])
    u = _dot(h, win_ref[...])

    @pl.when(start == 0)
    def _():
        tail_ref[...] = jnp.zeros_like(tail_ref)

    pos = start + lax.broadcasted_iota(jnp.int32, (tm, 1), 0)
    C = u.shape[1] // len(POOL_WINDOWS)
    ys = []
    for gi, w in enumerate(POOL_WINDOWS):
        ug = u[:, gi * C:(gi + 1) * C]
        s = jnp.concatenate([tail_ref[:, gi * C:(gi + 1) * C], ug], axis=0)
        k = 1
        while k < w:
            s = s + pltpu.roll(s, k, 0)
            k *= 2
        cnt = jnp.minimum(pos + 1, w).astype(F32)
        pooled = s[POOL_HALO:] / cnt - ug
        ys.append(_dot(pooled, wgrp_ref[gi]))
    tail_ref[...] = u[tm - POOL_HALO:, :]
    y = jnp.concatenate(ys, axis=1) * scale_ref[...]
    o_ref[...] = x + m_ref[2:3, :] * _dot(y, wout_ref[...])


def _pool_mixer(x, prev, mod, g, w_in, w_grp, scale, w_out, T):
    D = x.shape[1]
    weights = [w_in.astype(BF16), w_grp.astype(BF16), scale.reshape(1, D), w_out.astype(BF16)]
    return _mixer_call(_pool_kernel, "mixer_pool", 512, T, x, prev, mod, g, weights,
                       [pltpu.VMEM((POOL_HALO, D), F32)])


def _conv_kernel(has_prev, tm, T, *refs):
    x, m_ref, g_ref, (win_ref, cw_ref, wout_ref, o_ref, tail_ref) = _mixer_input(has_prev, refs)
    i = pl.program_id(0)
    D = x.shape[1]
    h = _norm_mod(x, g_ref[...], m_ref[1:2, :], m_ref[0:1, :])
    bcu = _dot(h, win_ref[...])
    z = bcu[:, D:2 * D] * bcu[:, 2 * D:]

    @pl.when((i * tm) % T == 0)
    def _():
        tail_ref[...] = jnp.zeros_like(tail_ref)

    ze = jnp.concatenate([tail_ref[...], z], axis=0)
    zc = cw_ref[CONV_WIDTH - 1:CONV_WIDTH, :] * ze
    for j in range(1, CONV_WIDTH):
        zc = zc + cw_ref[CONV_WIDTH - 1 - j:CONV_WIDTH - j, :] * pltpu.roll(ze, j, 0)
    tail_ref[...] = z[tm - CONV_HALO:, :]
    y = bcu[:, :D] * zc[CONV_HALO:]
    o_ref[...] = x + m_ref[2:3, :] * _dot(y, wout_ref[...])


def _conv_mixer(x, prev, mod, g, w_in, conv_w, w_out, T):
    D = x.shape[1]
    weights = [w_in.astype(BF16), conv_w, w_out.astype(BF16)]
    return _mixer_call(_conv_kernel, "mixer_conv", 512, T, x, prev, mod, g, weights,
                       [pltpu.VMEM((CONV_HALO, D), F32)])


def _swa_kernel(has_prev, tm, T, *refs):
    x, m_ref, g_ref, (win_ref, wout_ref, sink_ref, o_ref, kt_ref, vt_ref) = _mixer_input(has_prev, refs)
    i = pl.program_id(0)
    D = x.shape[1]
    W = SWA_WINDOW
    hd = D // SWA_HEADS
    G = SWA_HEADS // SWA_KV_HEADS
    kvd = SWA_KV_HEADS * hd
    first_key = jnp.where((i * tm) % T == 0, W, 0)

    @pl.when((i * tm) % T == 0)
    def _():
        kt_ref[...] = jnp.zeros_like(kt_ref)
        vt_ref[...] = jnp.zeros_like(vt_ref)

    h = _norm_mod(x, g_ref[...], m_ref[1:2, :], m_ref[0:1, :])
    qkv = _dot(h, win_ref[...])
    q = qkv[:, :D] * (hd ** -0.5)
    k = qkv[:, D:D + kvd]
    v = qkv[:, D + kvd:]
    ke = jnp.concatenate([kt_ref[...], k], axis=0)
    ve = jnp.concatenate([vt_ref[...], v], axis=0)
    kt_ref[...] = k[tm - W:, :]
    vt_ref[...] = v[tm - W:, :]
    qpos = lax.broadcasted_iota(jnp.int32, (G * W, 2 * W), 0) % W
    kj = lax.broadcasted_iota(jnp.int32, (G * W, 2 * W), 1)
    band = (kj > qpos) & (kj <= qpos + W)
    blocks = []
    for j in range(tm // W):
        mask = band & (kj >= first_key) if j == 0 else band
        heads = []
        for hk in range(SWA_KV_HEADS):
            kc = ke[j * W:(j + 2) * W, hk * hd:(hk + 1) * hd]
            vc = ve[j * W:(j + 2) * W, hk * hd:(hk + 1) * hd]
            qs = jnp.concatenate(
                [q[j * W:(j + 1) * W, (hk * G + gq) * hd:(hk * G + gq + 1) * hd] for gq in range(G)], axis=0)
            sink = jnp.concatenate(
                [jnp.full((W, 1), sink_ref[hk * G + gq], F32) for gq in range(G)], axis=0)
            s = jnp.where(mask, _dot_nt(qs, kc), -jnp.inf)
            mx = jnp.maximum(jnp.max(s, axis=-1, keepdims=True), sink)
            p = jnp.exp(s - mx)
            denom = jnp.sum(p, axis=-1, keepdims=True) + jnp.exp(sink - mx)
            o = _dot(p, vc) / denom
            heads += [o[gq * W:(gq + 1) * W] for gq in range(G)]
        blocks.append(jnp.concatenate(heads, axis=1))
    o_all = jnp.concatenate(blocks, axis=0)
    o_ref[...] = x + m_ref[2:3, :] * _dot(o_all, wout_ref[...])


def _swa_mixer(x, prev, mod, g, w_in, sinks, w_out, T):
    D = x.shape[1]
    kvd = SWA_KV_HEADS * (D // SWA_HEADS)
    weights = [w_in.astype(BF16), w_out.astype(BF16)]
    return _mixer_call(_swa_kernel, "mixer_swa", 256, T, x, prev, mod, g, weights,
                       [pltpu.VMEM((SWA_WINDOW, kvd), F32), pltpu.VMEM((SWA_WINDOW, kvd), F32)], smem=[sinks])


def _chunk_cumsum(x, C):
    ridx = lax.broadcasted_iota(jnp.int32, (x.shape[0], 1), 0) % C
    k = 1
    while k < C:
        x = x + jnp.where(ridx >= k, pltpu.roll(x, k, 0), 0.0)
        k *= 2
    return x


def _hgrn_kernel(layer, has_prev, tm, T, *refs):
    x, m_ref, g_ref, (win_ref, lbl_ref, ng_ref, wout_ref, o_ref, st_ref) = _mixer_input(has_prev, refs)
    i = pl.program_id(0)
    D = x.shape[1]
    dk = HGRN_HEAD_DIM
    C = HGRN_CHUNK

    @pl.when((i * tm) % T == 0)
    def _():
        st_ref[...] = jnp.zeros_like(st_ref)

    rows = [lbl_ref[j:j + 1, :] for j in range(lbl_ref.shape[0])]
    mx = functools.reduce(jnp.maximum, rows)
    es = [jnp.exp(r - mx) for r in rows]
    tot = functools.reduce(lambda a, b: a + b, es)
    lb = jnp.zeros_like(mx)
    for j in range(1, layer + 1):
        lb = lb + es[j] / tot

    h = _norm_mod(x, g_ref[...], m_ref[1:2, :], m_ref[0:1, :])
    proj = _dot(h, win_ref[...])
    causal = lax.broadcasted_iota(jnp.int32, (C, C), 0) >= lax.broadcasted_iota(jnp.int32, (C, C), 1)
    outs = []
    for hh in range(D // dk):
        sl = slice(hh * dk, (hh + 1) * dk)
        q = _silu(proj[:, sl])
        lbh = lb[:, sl]
        f = lbh + (1.0 - lbh) * _sigmoid(proj[:, D + hh * dk:D + (hh + 1) * dk])
        kk = 1.0 - f
        v = proj[:, 2 * D + hh * dk:2 * D + (hh + 1) * dk]
        gate = proj[:, 3 * D + hh * dk:3 * D + (hh + 1) * dk]
        b = _chunk_cumsum(jnp.log(f), C)
        st = st_ref[hh]
        oc = []
        for c in range(tm // C):
            rs = slice(c * C, (c + 1) * C)
            bc, qc, kc, vc = b[rs], q[rs], kk[rs], v[rs]
            ref = bc[C // 2 - 1:C // 2, :]
            last = bc[C - 1:C, :]
            sc = jnp.where(causal, _dot_nt(qc * jnp.exp(bc - ref), kc * jnp.exp(ref - bc)), 0.0)
            oc.append(_dot(sc, vc) + _dot_nt(qc * jnp.exp(bc), st))
            st = st * jnp.exp(last) + _dot_tn(vc, kc * jnp.exp(last - bc))
        st_ref[hh] = st
        o = jnp.concatenate(oc, axis=0)
        o = o * lax.rsqrt(jnp.mean(o * o, axis=-1, keepdims=True) + EPS) * ng_ref[...]
        outs.append(o * _silu(gate))
    y = jnp.concatenate(outs, axis=1)
    o_ref[...] = x + m_ref[2:3, :] * _dot(y, wout_ref[...])


def _hgrn_mixer(layer, x, prev, mod, g, w_in, lb_logits, norm_g, w_out, T):
    D = x.shape[1]
    dk = HGRN_HEAD_DIM
    weights = [w_in.astype(BF16), lb_logits, norm_g.reshape(1, dk), w_out.astype(BF16)]
    return _mixer_call(functools.partial(_hgrn_kernel, layer), "mixer_hgrn", 256, T, x, prev, mod, g, weights,
                       [pltpu.VMEM((D // dk, dk, dk), F32)])


def _split2(a):
    hi = a.astype(BF16)
    lo = (a - hi.astype(F32)).astype(BF16)
    return hi, lo


def _route_kernel(tm, x_ref, m_ref, g_ref, rw_ref, rb_ref, tri_ref, h_ref, cls_ref, rank_ref, cnt_ref, carry_ref):
    i = pl.program_id(0)

    @pl.when(i == 0)
    def _():
        carry_ref[...] = jnp.zeros_like(carry_ref)

    h = _norm_mod(x_ref[...], g_ref[...], m_ref[4:5, :], m_ref[3:4, :])
    h_ref[...] = h
    h_hi, h_lo = _split2(h)
    w_hi, w_lo = _split2(rw_ref[...])
    logits = (jnp.dot(h_hi, w_hi, preferred_element_type=F32) + jnp.dot(h_lo, w_hi, preferred_element_type=F32)
              + jnp.dot(h_hi, w_lo, preferred_element_type=F32))
    lt = jnp.transpose(logits)[:N_EXPERTS, :]
    score = _sigmoid(lt)
    sel = score + rb_ref[...]
    gscore, gsel = [], []
    for gi in range(N_GROUPS):
        r = [sel[gi * 4 + e:gi * 4 + e + 1, :] for e in range(EXPERTS_PER_GROUP)]
        m1 = functools.reduce(jnp.maximum, r)
        m2 = None
        for a in range(EXPERTS_PER_GROUP):
            for b2 in range(a + 1, EXPERTS_PER_GROUP):
                pm = jnp.minimum(r[a], r[b2])
                m2 = pm if m2 is None else jnp.maximum(m2, pm)
        gscore.append(m1 + m2)
        gsel.append(r)
    best = jnp.zeros((1, tm), jnp.int32)
    bs = gscore[0]
    for gi in range(1, N_GROUPS):
        better = gscore[gi] > bs
        best = jnp.where(better, gi, best)
        bs = jnp.where(better, gscore[gi], bs)
    r = [functools.reduce(lambda a, b2: a + b2,
                          [jnp.where(best == gi, gsel[gi][e], 0.0) for gi in range(N_GROUPS)])
         for e in range(EXPERTS_PER_GROUP)]
    keep = []
    for e in range(EXPERTS_PER_GROUP):
        beaten = jnp.zeros((1, tm), jnp.int32)
        for o in range(EXPERTS_PER_GROUP):
            if o != e:
                wins = (r[o] > r[e]) | ((r[o] == r[e]) & (o < e))
                beaten = beaten + wins.astype(jnp.int32)
        keep.append(beaten < 2)
    pair = jnp.zeros((1, tm), jnp.int32)
    for pi, (a, b2) in enumerate(PAIRS):
        pair = jnp.where(keep[a] & keep[b2], pi, pair)
    cls = best * len(PAIRS) + pair
    cls_ref[...] = cls.reshape(1, 1, tm)
    onehot = (lax.broadcasted_iota(jnp.int32, (CLASS_ROWS, tm), 0) == cls).astype(F32)
    before = jnp.dot(onehot.astype(BF16), tri_ref[...], preferred_element_type=F32) + carry_ref[:, 0:1]
    rank = jnp.sum(onehot * before, axis=0, keepdims=True)
    rank_ref[...] = rank.astype(jnp.int32).reshape(1, 1, tm)
    carry_ref[...] = carry_ref[...] + jnp.sum(onehot, axis=1, keepdims=True)
    cnt_ref[...] = carry_ref[...]


def _route(x1, mod, g, router_w, router_bias, T):
    N, D = x1.shape
    tm = 512
    nt = N // tm
    E = router_w.shape[1]
    rw = jnp.zeros((D, LANES), F32).at[:, :E].set(router_w)
    rb = router_bias.reshape(E, 1)
    tri = (jnp.arange(tm)[:, None] < jnp.arange(tm)[None, :]).astype(BF16)
    row = pl.BlockSpec((tm, D), lambda i: (i, 0))
    modspec = pl.BlockSpec((None, 6, D), lambda i: ((i * tm) // T, 0, 0))
    tok = pl.BlockSpec((1, 1, tm), lambda i: (i, 0, 0))
    h2, cls, rank, cnt = pl.pallas_call(
        functools.partial(_route_kernel, tm),
        grid=(nt,),
        in_specs=[row, modspec, _full((1, D)), _full((D, LANES)), _full((E, 1)), _full((tm, tm))],
        out_specs=[row, tok, tok, _full((CLASS_ROWS, LANES))],
        out_shape=[jax.ShapeDtypeStruct((N, D), F32),
                   jax.ShapeDtypeStruct((nt, 1, tm), jnp.int32),
                   jax.ShapeDtypeStruct((nt, 1, tm), jnp.int32),
                   jax.ShapeDtypeStruct((CLASS_ROWS, LANES), F32)],
        scratch_shapes=[pltpu.VMEM((CLASS_ROWS, LANES), F32)],
        compiler_params=_cparams(),
        name="router",
    )(x1, mod, g.reshape(1, D), rw, rb, tri)
    return h2, cls.reshape(N), rank.reshape(N), cnt[:N_CLASSES, 0]


def _scatter_rows_kernel(chunk, idx_ref, src_ref, init_ref, dst_ref, sem):
    del init_ref
    base = pl.program_id(0) * chunk

    def issue(r, carry):
        t = base + r
        pltpu.make_async_copy(src_ref.at[pl.ds(t, 1), :], dst_ref.at[pl.ds(idx_ref[t], 1), :], sem).start()
        return carry

    lax.fori_loop(0, chunk, issue, 0, unroll=8)
    pltpu.make_async_copy(src_ref.at[pl.ds(0, chunk), :], dst_ref.at[pl.ds(0, chunk), :], sem).wait()


def _scatter_rows(src, idx, n_out):
    N, D = src.shape
    chunk = min(PERM_CHUNK, N)
    init = jnp.zeros((n_out, D), src.dtype)
    return pl.pallas_call(
        functools.partial(_scatter_rows_kernel, chunk),
        grid_spec=pltpu.PrefetchScalarGridSpec(
            num_scalar_prefetch=1, grid=(N // chunk,),
            in_specs=[pl.BlockSpec(memory_space=pl.ANY), pl.BlockSpec(memory_space=pl.ANY)],
            out_specs=pl.BlockSpec(memory_space=pl.ANY),
            scratch_shapes=[pltpu.SemaphoreType.DMA(())]),
        out_shape=jax.ShapeDtypeStruct((n_out, D), src.dtype),
        input_output_aliases={2: 0},
        compiler_params=_cparams(),
        name="scatter_rows",
    )(idx, src, init)


def _gather_rows_kernel(chunk, idx_ref, src_ref, dst_ref, sem):
    base = pl.program_id(0) * chunk

    def issue(r, carry):
        t = base + r
        pltpu.make_async_copy(src_ref.at[pl.ds(idx_ref[t], 1), :], dst_ref.at[pl.ds(t, 1), :], sem).start()
        return carry

    lax.fori_loop(0, chunk, issue, 0, unroll=8)
    pltpu.make_async_copy(src_ref.at[pl.ds(0, chunk), :], dst_ref.at[pl.ds(0, chunk), :], sem).wait()


def _gather_rows(src, idx):
    N = idx.shape[0]
    D = src.shape[1]
    chunk = min(PERM_CHUNK, N)
    return pl.pallas_call(
        functools.partial(_gather_rows_kernel, chunk),
        grid_spec=pltpu.PrefetchScalarGridSpec(
            num_scalar_prefetch=1, grid=(N // chunk,),
            in_specs=[pl.BlockSpec(memory_space=pl.ANY)],
            out_specs=pl.BlockSpec(memory_space=pl.ANY),
            scratch_shapes=[pltpu.SemaphoreType.DMA(())]),
        out_shape=jax.ShapeDtypeStruct((N, D), src.dtype),
        compiler_params=_cparams(),
        name="gather_rows",
    )(idx, src)


def _ffn_kernel(ea_ref, eb_ref, nact_ref, x_ref, rwt_ref, wga_ref, wua_ref, wda_ref, wgb_ref, wub_ref, wdb_ref,
                o_ref):
    j = pl.program_id(0)

    @pl.when(j < nact_ref[0])
    def _():
        x = x_ref[...]
        sa = _sigmoid(jnp.sum(x * rwt_ref[pl.ds(ea_ref[j], 1), :], axis=-1, keepdims=True))
        sb = _sigmoid(jnp.sum(x * rwt_ref[pl.ds(eb_ref[j], 1), :], axis=-1, keepdims=True))
        inv = 1.0 / (sa + sb)
        xb = x.astype(BF16)
        aa = _silu(_dot(xb, wga_ref[...])) * _dot(xb, wua_ref[...]) * (sa * inv)
        ab = _silu(_dot(xb, wgb_ref[...])) * _dot(xb, wub_ref[...]) * (sb * inv)
        o_ref[...] = _dot(aa, wda_ref[...]) + _dot(ab, wdb_ref[...])

    @pl.when(j >= nact_ref[0])
    def _():
        o_ref[...] = jnp.zeros_like(o_ref)


def _ffn(xs, blk_ea, blk_eb, n_active, router_wt, w_gate, w_up, w_down):
    Ns, D = xs.shape
    F = w_gate.shape[2]
    nb = Ns // FFN_BLOCK

    def wa(j, ea, eb, na):
        return (ea[j], 0, 0)

    def wb(j, ea, eb, na):
        return (eb[j], 0, 0)

    row = pl.BlockSpec((FFN_BLOCK, D), lambda j, ea, eb, na: (j, 0))
    return pl.pallas_call(
        _ffn_kernel,
        grid_spec=pltpu.PrefetchScalarGridSpec(
            num_scalar_prefetch=3, grid=(nb,),
            in_specs=[row, pl.BlockSpec(router_wt.shape, lambda j, ea, eb, na: (0, 0)),
                      pl.BlockSpec((None, D, F), wa), pl.BlockSpec((None, D, F), wa), pl.BlockSpec((None, F, D), wa),
                      pl.BlockSpec((None, D, F), wb), pl.BlockSpec((None, D, F), wb), pl.BlockSpec((None, F, D), wb)],
            out_specs=row),
        out_shape=jax.ShapeDtypeStruct((Ns, D), F32),
        compiler_params=_cparams(),
        name="moe_ffn",
    )(blk_ea, blk_eb, n_active, xs, router_wt, w_gate, w_up, w_down, w_gate, w_up, w_down)


def _moe(x1, mod, g, router_w, router_bias, w_gate, w_up, w_down, T):
    N, D = x1.shape
    h2, cls, rank, cnt = _route(x1, mod, g, router_w, router_bias, T)
    counts = cnt.astype(jnp.int32)
    padded = (counts + FFN_BLOCK - 1) // FFN_BLOCK * FFN_BLOCK
    ends = jnp.cumsum(padded)
    starts = ends - padded
    dest = starts[cls] + rank
    nb = N // FFN_BLOCK + N_CLASSES
    n_active = (ends[-1] // FFN_BLOCK).astype(jnp.int32)
    blk = jnp.arange(nb, dtype=jnp.int32)
    blk_cls = jnp.searchsorted(ends, jnp.minimum(blk, n_active - 1) * FFN_BLOCK, side="right").astype(jnp.int32)
    blk_cls = jnp.minimum(blk_cls, N_CLASSES - 1)
    pair_a = jnp.array([p[0] for p in PAIRS], jnp.int32)
    pair_b = jnp.array([p[1] for p in PAIRS], jnp.int32)
    grp = blk_cls // len(PAIRS)
    blk_ea = grp * EXPERTS_PER_GROUP + pair_a[blk_cls % len(PAIRS)]
    blk_eb = grp * EXPERTS_PER_GROUP + pair_b[blk_cls % len(PAIRS)]
    xs = _scatter_rows(h2, dest, nb * FFN_BLOCK)
    ys = _ffn(xs, blk_ea, blk_eb, n_active.reshape(1), jnp.transpose(router_w),
              w_gate.astype(BF16), w_up.astype(BF16), w_down.astype(BF16))
    return _gather_rows(ys, dest)


def _final_kernel(x_ref, y_ref, pm_ref, g_ref, o_ref):
    x = x_ref[...] + pm_ref[5:6, :] * y_ref[...]
    ms = jnp.mean(x * x, axis=-1, keepdims=True)
    o_ref[...] = x * lax.rsqrt(ms + EPS) * g_ref[...]


def _final(x, y, mod, g, T):
    N, D = x.shape
    tm = min(1024, T)
    row = pl.BlockSpec((tm, D), lambda i: (i, 0))
    modspec = pl.BlockSpec((None, 6, D), lambda i: ((i * tm) // T, 0, 0))
    return pl.pallas_call(
        _final_kernel,
        grid=(N // tm,),
        in_specs=[row, row, modspec, _full((1, D))],
        out_specs=row,
        out_shape=jax.ShapeDtypeStruct((N, D), F32),
        compiler_params=_cparams(),
        name="final_norm",
    )(x, y, mod, g.reshape(1, D))


def kernel(x, c, ada_w, ada_b, norm_g, final_norm_g, pool_w_in, pool_w_grp, pool_scale, pool_w_out, hgrn_w_in, hgrn_lb_logits, hgrn_norm_g, hgrn_w_out, swa_w_in, swa_sinks, swa_w_out, conv_w_in, conv_w, conv_w_out, router_w, router_bias, moe_w_gate, moe_w_up, moe_w_down):
    B, T, D = x.shape
    depth = ada_w.shape[0]
    n_mixers = 4
    mod = _ada(c, ada_w, ada_b).reshape(depth, B, 6, D)
    xt = x.reshape(B * T, D)
    prev = None
    for i in range(depth):
        m, j = i % n_mixers, i // n_mixers
        g1n = norm_g[i, 0]
        if m == 0:
            x1 = _pool_mixer(xt, prev, mod[i], g1n, pool_w_in[j], pool_w_grp[j], pool_scale[j], pool_w_out[j], T)
        elif m == 1:
            x1 = _hgrn_mixer(i, xt, prev, mod[i], g1n, hgrn_w_in[j], hgrn_lb_logits, hgrn_norm_g[j],
                             hgrn_w_out[j], T)
        elif m == 2:
            x1 = _swa_mixer(xt, prev, mod[i], g1n, swa_w_in[j], swa_sinks[j], swa_w_out[j], T)
        else:
            x1 = _conv_mixer(xt, prev, mod[i], g1n, conv_w_in[j], conv_w[j], conv_w_out[j], T)
        y = _moe(x1, mod[i], norm_g[i, 1], router_w, router_bias, moe_w_gate[i], moe_w_up[i], moe_w_down[i], T)
        xt = x1
        prev = (y, mod[i])
    out = _final(xt, prev[0], prev[1], final_norm_g, T)
    return out.reshape(B, T, D)
```

```python
import functools

import jax
import jax.numpy as jnp
from jax import lax
from jax.experimental import pallas as pl
from jax.experimental.pallas import tpu as pltpu

F32 = jnp.float32
BF16 = jnp.bfloat16
EPS = 1e-6

POOL_WINDOWS = (2, 4, 8, 16)
POOL_HALO = 16
HGRN_HEAD_DIM = 128
HGRN_CHUNK = 64
SWA_HEADS = 16
SWA_KV_HEADS = 4
SWA_WINDOW = 128
CONV_WIDTH = 3
CONV_HALO = 8
N_EXPERTS = 16
N_GROUPS = 4
EXPERTS_PER_GROUP = 4
PAIRS = ((0, 1), (0, 2), (0, 3), (1, 2), (1, 3), (2, 3))
N_CLASSES = N_GROUPS * len(PAIRS)
CLASS_ROWS = 32
LANES = 128
FFN_BLOCK = 256
PERM_CHUNK = 2048
VMEM_LIMIT = 52 * 1024 * 1024


def _cparams():
    return pltpu.CompilerParams(dimension_semantics=("arbitrary",), vmem_limit_bytes=VMEM_LIMIT)


def _full(shape):
    nd = len(shape)
    return pl.BlockSpec(shape, lambda i, *_: (0,) * nd)


def _norm_mod(x, g, sc, sh):
    ms = jnp.mean(x * x, axis=-1, keepdims=True)
    return x * lax.rsqrt(ms + EPS) * (g * (1.0 + sc)) + sh


def _sigmoid(x):
    return 1.0 / (1.0 + jnp.exp(-x))


def _silu(x):
    return x * _sigmoid(x)


def _dot(a, b):
    return jnp.dot(a.astype(BF16), b.astype(BF16), preferred_element_type=F32)


def _dot_nt(a, b):
    return lax.dot_general(a.astype(BF16), b.astype(BF16), (((1,), (1,)), ((), ())), preferred_element_type=F32)


def _dot_tn(a, b):
    return lax.dot_general(a.astype(BF16), b.astype(BF16), (((0,), (0,)), ((), ())), preferred_element_type=F32)


def _tt_load(ref, rows, d):
    n = d // LANES
    return jnp.concatenate([ref[pl.ds(j, rows, stride=n), :] for j in range(n)], axis=1)


def _tt_store(ref, val):
    rows, d = val.shape
    n = d // LANES
    for j in range(n):
        ref[pl.ds(j, rows, stride=n), :] = val[:, j * LANES:(j + 1) * LANES]


def _cast_kernel(w_ref, o_ref):
    o_ref[...] = w_ref[...].astype(BF16)


def _cast_bf16(w):
    E, K, M = w.shape
    spec = pl.BlockSpec((None, K, M), lambda e: (e, 0, 0))
    return pl.pallas_call(
        _cast_kernel,
        grid=(E,),
        in_specs=[spec],
        out_specs=spec,
        out_shape=jax.ShapeDtypeStruct(w.shape, BF16),
        compiler_params=_cparams(),
        name="cast_bf16",
    )(w)


def _ada_kernel(c_ref, w_ref, b_ref, o_ref):
    cond = _silu(c_ref[...])
    o_ref[...] = jnp.dot(cond, w_ref[...], preferred_element_type=F32, precision=lax.Precision.HIGHEST) + b_ref[...]


def _ada(c, ada_w, ada_b):
    L, D, D6 = ada_w.shape
    B = c.shape[0]
    bn = D6 // 4
    return pl.pallas_call(
        _ada_kernel,
        grid=(L, D6 // bn),
        in_specs=[pl.BlockSpec((B, D), lambda l, j: (0, 0)),
                  pl.BlockSpec((None, D, bn), lambda l, j: (l, 0, j)),
                  pl.BlockSpec((None, 1, bn), lambda l, j: (l, 0, j))],
        out_specs=pl.BlockSpec((None, B, bn), lambda l, j: (l, 0, j)),
        out_shape=jax.ShapeDtypeStruct((L, B, D6), F32),
        compiler_params=pltpu.CompilerParams(dimension_semantics=("arbitrary", "arbitrary"),
                                             vmem_limit_bytes=VMEM_LIMIT),
        name="ada_mod",
    )(c, ada_w, ada_b.reshape(L, 1, D6))


def _mixer_call(body, name, tm, T, x, prev, mod, g, weights, scratch, smem=()):
    N, D = x.shape
    row = pl.BlockSpec((tm, D), lambda i: (i, 0))
    modspec = pl.BlockSpec((None, 6, D), lambda i: ((i * tm) // T, 0, 0))
    args, specs = [x], [row]
    if prev is not None:
        args += [prev[0], prev[1]]
        specs += [pl.BlockSpec((tm * D // LANES, LANES), lambda i: (i, 0)), modspec]
    args += [mod, g.reshape(1, D)]
    specs += [modspec, _full((1, D))]
    for w in weights:
        args.append(w)
        specs.append(_full(w.shape))
    for s in smem:
        args.append(s)
        specs.append(pl.BlockSpec(memory_space=pltpu.SMEM))
    return pl.pallas_call(
        functools.partial(body, prev is not None, tm, T),
        grid=(N // tm,),
        in_specs=specs,
        out_specs=row,
        out_shape=jax.ShapeDtypeStruct((N, D), F32),
        scratch_shapes=scratch,
        compiler_params=_cparams(),
        name=name,
    )(*args)


def _mixer_input(has_prev, refs):
    if has_prev:
        x_ref, y_ref, pm_ref, m_ref, g_ref = refs[:5]
        x = x_ref[...]
        x = x + pm_ref[5:6, :] * _tt_load(y_ref, *x.shape)
        rest = refs[5:]
    else:
        x_ref, m_ref, g_ref = refs[:3]
        x = x_ref[...]
        rest = refs[3:]
    return x, m_ref, g_ref, rest


def _pool_kernel(has_prev, tm, T, *refs):
    x, m_ref, g_ref, (win_ref, wgrp_ref, scale_ref, wout_ref, o_ref, tail_ref) = _mixer_input(has_prev, refs)
    i = pl.program_id(0)
    start = (i * tm) % T
    h = _norm_mod(x, g_ref[...], m_ref[1:2, :], m_ref[0:1, :])
    u = _dot(h, win_ref[...])

    @pl.when(start == 0)
    def _():
        tail_ref[...] = jnp.zeros_like(tail_ref)

    pos = start + lax.broadcasted_iota(jnp.int32, (tm, 1), 0)
    C = u.shape[1] // len(POOL_WINDOWS)
    ys = []
    for gi, w in enumerate(POOL_WINDOWS):
        ug = u[:, gi * C:(gi + 1) * C]
        s = jnp.concatenate([tail_ref[:, gi * C:(gi + 1) * C], ug], axis=0)
        k = 1
        while k < w:
            s = s + pltpu.roll(s, k, 0)
            k *= 2
        cnt = jnp.minimum(pos + 1, w).astype(F32)
        pooled = s[POOL_HALO:] / cnt - ug
        ys.append(_dot(pooled, wgrp_ref[gi]))
    tail_ref[...] = u[tm - POOL_HALO:, :]
    y = jnp.concatenate(ys, axis=1) * scale_ref[...]
    o_ref[...] = x + m_ref[2:3, :] * _dot(y, wout_ref[...])


def _pool_mixer(x, prev, mod, g, w_in, w_grp, scale, w_out, T):
    D = x.shape[1]
    weights = [w_in.astype(BF16), w_grp.astype(BF16), scale.reshape(1, D), w_out.astype(BF16)]
    return _mixer_call(_pool_kernel, "mixer_pool", 512, T, x, prev, mod, g, weights,
                       [pltpu.VMEM((POOL_HALO, D), F32)])


def _conv_kernel(has_prev, tm, T, *refs):
    x, m_ref, g_ref, (win_ref, cw_ref, wout_ref, o_ref, tail_ref) = _mixer_input(has_prev, refs)
    i = pl.program_id(0)
    D = x.shape[1]
    h = _norm_mod(x, g_ref[...], m_ref[1:2, :], m_ref[0:1, :])
    bcu = _dot(h, win_ref[...])
    z = bcu[:, D:2 * D] * bcu[:, 2 * D:]

    @pl.when((i * tm) % T == 0)
    def _():
        tail_ref[...] = jnp.zeros_like(tail_ref)

    ze = jnp.concatenate([tail_ref[...], z], axis=0)
    zc = cw_ref[CONV_WIDTH - 1:CONV_WIDTH, :] * ze
    for j in range(1, CONV_WIDTH):
        zc = zc + cw_ref[CONV_WIDTH - 1 - j:CONV_WIDTH - j, :] * pltpu.roll(ze, j, 0)
    tail_ref[...] = z[tm - CONV_HALO:, :]
    y = bcu[:, :D] * zc[CONV_HALO:]
    o_ref[...] = x + m_ref[2:3, :] * _dot(y, wout_ref[...])


def _conv_mixer(x, prev, mod, g, w_in, conv_w, w_out, T):
    D = x.shape[1]
    weights = [w_in.astype(BF16), conv_w, w_out.astype(BF16)]
    return _mixer_call(_conv_kernel, "mixer_conv", 512, T, x, prev, mod, g, weights,
                       [pltpu.VMEM((CONV_HALO, D), F32)])


def _swa_kernel(has_prev, tm, T, *refs):
    x, m_ref, g_ref, (win_ref, wout_ref, sink_ref, o_ref, kt_ref, vt_ref) = _mixer_input(has_prev, refs)
    i = pl.program_id(0)
    D = x.shape[1]
    W = SWA_WINDOW
    hd = D // SWA_HEADS
    G = SWA_HEADS // SWA_KV_HEADS
    kvd = SWA_KV_HEADS * hd
    first_key = jnp.where((i * tm) % T == 0, W, 0)

    @pl.when((i * tm) % T == 0)
    def _():
        kt_ref[...] = jnp.zeros_like(kt_ref)
        vt_ref[...] = jnp.zeros_like(vt_ref)

    h = _norm_mod(x, g_ref[...], m_ref[1:2, :], m_ref[0:1, :])
    qkv = _dot(h, win_ref[...])
    q = qkv[:, :D] * (hd ** -0.5)
    k = qkv[:, D:D + kvd]
    v = qkv[:, D + kvd:]
    ke = jnp.concatenate([kt_ref[...], k], axis=0)
    ve = jnp.concatenate([vt_ref[...], v], axis=0)
    kt_ref[...] = k[tm - W:, :]
    vt_ref[...] = v[tm - W:, :]
    qpos = lax.broadcasted_iota(jnp.int32, (G * W, 2 * W), 0) % W
    kj = lax.broadcasted_iota(jnp.int32, (G * W, 2 * W), 1)
    band = (kj > qpos) & (kj <= qpos + W)
    blocks = []
    for j in range(tm // W):
        mask = band & (kj >= first_key) if j == 0 else band
        heads = []
        for hk in range(SWA_KV_HEADS):
            kc = ke[j * W:(j + 2) * W, hk * hd:(hk + 1) * hd]
            vc = ve[j * W:(j + 2) * W, hk * hd:(hk + 1) * hd]
            qs = jnp.concatenate(
                [q[j * W:(j + 1) * W, (hk * G + gq) * hd:(hk * G + gq + 1) * hd] for gq in range(G)], axis=0)
            sink = jnp.concatenate(
                [jnp.full((W, 1), sink_ref[hk * G + gq], F32) for gq in range(G)], axis=0)
            s = jnp.where(mask, _dot_nt(qs, kc), -jnp.inf)
            mx = jnp.maximum(jnp.max(s, axis=-1, keepdims=True), sink)
            p = jnp.exp(s - mx)
            denom = jnp.sum(p, axis=-1, keepdims=True) + jnp.exp(sink - mx)
            o = _dot(p, vc) / denom
            heads += [o[gq * W:(gq + 1) * W] for gq in range(G)]
        blocks.append(jnp.concatenate(heads, axis=1))
    o_all = jnp.concatenate(blocks, axis=0)
    o_ref[...] = x + m_ref[2:3, :] * _dot(o_all, wout_ref[...])


def _swa_mixer(x, prev, mod, g, w_in, sinks, w_out, T):
    D = x.shape[1]
    kvd = SWA_KV_HEADS * (D // SWA_HEADS)
    weights = [w_in.astype(BF16), w_out.astype(BF16)]
    return _mixer_call(_swa_kernel, "mixer_swa", 256, T, x, prev, mod, g, weights,
                       [pltpu.VMEM((SWA_WINDOW, kvd), F32), pltpu.VMEM((SWA_WINDOW, kvd), F32)], smem=[sinks])


def _chunk_cumsum(x, C):
    ridx = lax.broadcasted_iota(jnp.int32, (x.shape[0], 1), 0) % C
    k = 1
    while k < C:
        x = x + jnp.where(ridx >= k, pltpu.roll(x, k, 0), 0.0)
        k *= 2
    return x


def _hgrn_kernel(layer, has_prev, tm, T, *refs):
    x, m_ref, g_ref, (win_ref, lbl_ref, ng_ref, wout_ref, o_ref, st_ref) = _mixer_input(has_prev, refs)
    i = pl.program_id(0)
    D = x.shape[1]
    dk = HGRN_HEAD_DIM
    C = HGRN_CHUNK

    @pl.when((i * tm) % T == 0)
    def _():
        st_ref[...] = jnp.zeros_like(st_ref)

    rows = [lbl_ref[j:j + 1, :] for j in range(lbl_ref.shape[0])]
    mx = functools.reduce(jnp.maximum, rows)
    es = [jnp.exp(r - mx) for r in rows]
    tot = functools.reduce(lambda a, b: a + b, es)
    lb = jnp.zeros_like(mx)
    for j in range(1, layer + 1):
        lb = lb + es[j] / tot

    h = _norm_mod(x, g_ref[...], m_ref[1:2, :], m_ref[0:1, :])
    proj = _dot(h, win_ref[...])
    causal = lax.broadcasted_iota(jnp.int32, (C, C), 0) >= lax.broadcasted_iota(jnp.int32, (C, C), 1)
    outs = []
    for hh in range(D // dk):
        sl = slice(hh * dk, (hh + 1) * dk)
        q = _silu(proj[:, sl])
        lbh = lb[:, sl]
        f = lbh + (1.0 - lbh) * _sigmoid(proj[:, D + hh * dk:D + (hh + 1) * dk])
        kk = 1.0 - f
        v = proj[:, 2 * D + hh * dk:2 * D + (hh + 1) * dk]
        gate = proj[:, 3 * D + hh * dk:3 * D + (hh + 1) * dk]
        b = _chunk_cumsum(jnp.log(f), C)
        st = st_ref[hh]
        oc = []
        for c in range(tm // C):
            rs = slice(c * C, (c + 1) * C)
            bc, qc, kc, vc = b[rs], q[rs], kk[rs], v[rs]
            ref = bc[C // 2 - 1:C // 2, :]
            last = bc[C - 1:C, :]
            sc = jnp.where(causal, _dot_nt(qc * jnp.exp(bc - ref), kc * jnp.exp(ref - bc)), 0.0)
            oc.append(_dot(sc, vc) + _dot_nt(qc * jnp.exp(bc), st))
            st = st * jnp.exp(last) + _dot_tn(vc, kc * jnp.exp(last - bc))
        st_ref[hh] = st
        o = jnp.concatenate(oc, axis=0)
        o = o * lax.rsqrt(jnp.mean(o * o, axis=-1, keepdims=True) + EPS) * ng_ref[...]
        outs.append(o * _silu(gate))
    y = jnp.concatenate(outs, axis=1)
    o_ref[...] = x + m_ref[2:3, :] * _dot(y, wout_ref[...])


def _hgrn_mixer(layer, x, prev, mod, g, w_in, lb_logits, norm_g, w_out, T):
    D = x.shape[1]
    dk = HGRN_HEAD_DIM
    weights = [w_in.astype(BF16), lb_logits, norm_g.reshape(1, dk), w_out.astype(BF16)]
    return _mixer_call(functools.partial(_hgrn_kernel, layer), "mixer_hgrn", 256, T, x, prev, mod, g, weights,
                       [pltpu.VMEM((D // dk, dk, dk), F32)])


def _split2(a):
    hi = a.astype(BF16)
    lo = (a - hi.astype(F32)).astype(BF16)
    return hi, lo


def _route_kernel(tm, x_ref, m_ref, g_ref, rw_ref, rb_ref, tri_ref, h_ref, cls_ref, rank_ref, cnt_ref, carry_ref):
    i = pl.program_id(0)

    @pl.when(i == 0)
    def _():
        carry_ref[...] = jnp.zeros_like(carry_ref)

    h = _norm_mod(x_ref[...], g_ref[...], m_ref[4:5, :], m_ref[3:4, :])
    _tt_store(h_ref, h)
    h_hi, h_lo = _split2(h)
    w_hi, w_lo = _split2(rw_ref[...])
    logits = (jnp.dot(h_hi, w_hi, preferred_element_type=F32) + jnp.dot(h_lo, w_hi, preferred_element_type=F32)
              + jnp.dot(h_hi, w_lo, preferred_element_type=F32))
    lt = jnp.transpose(logits)[:N_EXPERTS, :]
    score = _sigmoid(lt)
    sel = score + rb_ref[...]
    gscore, gsel = [], []
    for gi in range(N_GROUPS):
        r = [sel[gi * 4 + e:gi * 4 + e + 1, :] for e in range(EXPERTS_PER_GROUP)]
        m1 = functools.reduce(jnp.maximum, r)
        m2 = None
        for a in range(EXPERTS_PER_GROUP):
            for b2 in range(a + 1, EXPERTS_PER_GROUP):
                pm = jnp.minimum(r[a], r[b2])
                m2 = pm if m2 is None else jnp.maximum(m2, pm)
        gscore.append(m1 + m2)
        gsel.append(r)
    best = jnp.zeros((1, tm), jnp.int32)
    bs = gscore[0]
    for gi in range(1, N_GROUPS):
        better = gscore[gi] > bs
        best = jnp.where(better, gi, best)
        bs = jnp.where(better, gscore[gi], bs)
    r = [functools.reduce(lambda a, b2: a + b2,
                          [jnp.where(best == gi, gsel[gi][e], 0.0) for gi in range(N_GROUPS)])
         for e in range(EXPERTS_PER_GROUP)]
    keep = []
    for e in range(EXPERTS_PER_GROUP):
        beaten = jnp.zeros((1, tm), jnp.int32)
        for o in range(EXPERTS_PER_GROUP):
            if o != e:
                wins = (r[o] > r[e]) | ((r[o] == r[e]) & (o < e))
                beaten = beaten + wins.astype(jnp.int32)
        keep.append(beaten < 2)
    pair = jnp.zeros((1, tm), jnp.int32)
    for pi, (a, b2) in enumerate(PAIRS):
        pair = jnp.where(keep[a] & keep[b2], pi, pair)
    cls = best * len(PAIRS) + pair
    cls_ref[...] = cls.reshape(1, 1, tm)
    onehot = (lax.broadcasted_iota(jnp.int32, (CLASS_ROWS, tm), 0) == cls).astype(F32)
    before = jnp.dot(onehot.astype(BF16), tri_ref[...], preferred_element_type=F32) + carry_ref[:, 0:1]
    rank = jnp.sum(onehot * before, axis=0, keepdims=True)
    rank_ref[...] = rank.astype(jnp.int32).reshape(1, 1, tm)
    carry_ref[...] = carry_ref[...] + jnp.sum(onehot, axis=1, keepdims=True)
    cnt_ref[...] = carry_ref[...]


def _route(x1, mod, g, router_w, router_bias, T):
    N, D = x1.shape
    tm = 512
    nt = N // tm
    E = router_w.shape[1]
    rw = jnp.zeros((D, LANES), F32).at[:, :E].set(router_w)
    rb = router_bias.reshape(E, 1)
    tri = (jnp.arange(tm)[:, None] < jnp.arange(tm)[None, :]).astype(BF16)
    row = pl.BlockSpec((tm, D), lambda i: (i, 0))
    modspec = pl.BlockSpec((None, 6, D), lambda i: ((i * tm) // T, 0, 0))
    tok = pl.BlockSpec((1, 1, tm), lambda i: (i, 0, 0))
    h2, cls, rank, cnt = pl.pallas_call(
        functools.partial(_route_kernel, tm),
        grid=(nt,),
        in_specs=[row, modspec, _full((1, D)), _full((D, LANES)), _full((E, 1)), _full((tm, tm))],
        out_specs=[pl.BlockSpec((tm * D // LANES, LANES), lambda i: (i, 0)), tok, tok, _full((CLASS_ROWS, LANES))],
        out_shape=[jax.ShapeDtypeStruct((N * D // LANES, LANES), F32),
                   jax.ShapeDtypeStruct((nt, 1, tm), jnp.int32),
                   jax.ShapeDtypeStruct((nt, 1, tm), jnp.int32),
                   jax.ShapeDtypeStruct((CLASS_ROWS, LANES), F32)],
        scratch_shapes=[pltpu.VMEM((CLASS_ROWS, LANES), F32)],
        compiler_params=_cparams(),
        name="router",
    )(x1, mod, g.reshape(1, D), rw, rb, tri)
    return h2, cls.reshape(N), rank.reshape(N), cnt[:N_CLASSES, 0]


def _tok(ref, t, rpt):
    return ref.at[pl.ds(pl.multiple_of(t * rpt, rpt), rpt), :]


def _scatter_rows_kernel(chunk, rpt, idx_ref, src_ref, init_ref, dst_ref, sem):
    del init_ref
    base = pl.program_id(0) * chunk

    def issue(r, carry):
        t = base + r
        pltpu.make_async_copy(_tok(src_ref, t, rpt), _tok(dst_ref, idx_ref[t], rpt), sem).start()
        return carry

    lax.fori_loop(0, chunk, issue, 0, unroll=8)
    pltpu.make_async_copy(src_ref.at[pl.ds(0, chunk * rpt), :], dst_ref.at[pl.ds(0, chunk * rpt), :], sem).wait()


def _scatter_rows(src, idx, n_out, rpt):
    N = idx.shape[0]
    chunk = min(PERM_CHUNK, N)
    init = jnp.zeros((n_out * rpt, LANES), src.dtype)
    return pl.pallas_call(
        functools.partial(_scatter_rows_kernel, chunk, rpt),
        grid_spec=pltpu.PrefetchScalarGridSpec(
            num_scalar_prefetch=1, grid=(N // chunk,),
            in_specs=[pl.BlockSpec(memory_space=pl.ANY), pl.BlockSpec(memory_space=pl.ANY)],
            out_specs=pl.BlockSpec(memory_space=pl.ANY),
            scratch_shapes=[pltpu.SemaphoreType.DMA(())]),
        out_shape=jax.ShapeDtypeStruct(init.shape, src.dtype),
        input_output_aliases={2: 0},
        compiler_params=_cparams(),
        name="scatter_rows",
    )(idx, src, init)


def _gather_rows_kernel(chunk, rpt, idx_ref, src_ref, dst_ref, sem):
    base = pl.program_id(0) * chunk

    def issue(r, carry):
        t = base + r
        pltpu.make_async_copy(_tok(src_ref, idx_ref[t], rpt), _tok(dst_ref, t, rpt), sem).start()
        return carry

    lax.fori_loop(0, chunk, issue, 0, unroll=8)
    pltpu.make_async_copy(src_ref.at[pl.ds(0, chunk * rpt), :], dst_ref.at[pl.ds(0, chunk * rpt), :], sem).wait()


def _gather_rows(src, idx, rpt):
    N = idx.shape[0]
    chunk = min(PERM_CHUNK, N)
    return pl.pallas_call(
        functools.partial(_gather_rows_kernel, chunk, rpt),
        grid_spec=pltpu.PrefetchScalarGridSpec(
            num_scalar_prefetch=1, grid=(N // chunk,),
            in_specs=[pl.BlockSpec(memory_space=pl.ANY)],
            out_specs=pl.BlockSpec(memory_space=pl.ANY),
            scratch_shapes=[pltpu.SemaphoreType.DMA(())]),
        out_shape=jax.ShapeDtypeStruct((N * rpt, LANES), src.dtype),
        compiler_params=_cparams(),
        name="gather_rows",
    )(idx, src)


def _ffn_kernel(ea_ref, eb_ref, nact_ref, x_ref, rwt_ref, wga_ref, wua_ref, wda_ref, wgb_ref, wub_ref, wdb_ref,
                o_ref):
    j = pl.program_id(0)

    @pl.when(j < nact_ref[0])
    def _():
        x = _tt_load(x_ref, FFN_BLOCK, wga_ref.shape[0])
        sa = _sigmoid(jnp.sum(x * rwt_ref[pl.ds(ea_ref[j], 1), :], axis=-1, keepdims=True))
        sb = _sigmoid(jnp.sum(x * rwt_ref[pl.ds(eb_ref[j], 1), :], axis=-1, keepdims=True))
        inv = 1.0 / (sa + sb)
        xb = x.astype(BF16)
        aa = _silu(_dot(xb, wga_ref[...])) * _dot(xb, wua_ref[...]) * (sa * inv)
        ab = _silu(_dot(xb, wgb_ref[...])) * _dot(xb, wub_ref[...]) * (sb * inv)
        _tt_store(o_ref, _dot(aa, wda_ref[...]) + _dot(ab, wdb_ref[...]))

    @pl.when(j >= nact_ref[0])
    def _():
        o_ref[...] = jnp.zeros_like(o_ref)


def _ffn(xs, blk_ea, blk_eb, n_active, router_wt, w_gate, w_up, w_down):
    D, F = w_gate.shape[1:]
    rpt = D // LANES
    nb = xs.shape[0] // (FFN_BLOCK * rpt)

    def wa(j, ea, eb, na):
        return (ea[j], 0, 0)

    def wb(j, ea, eb, na):
        return (eb[j], 0, 0)

    row = pl.BlockSpec((FFN_BLOCK * rpt, LANES), lambda j, ea, eb, na: (j, 0))
    return pl.pallas_call(
        _ffn_kernel,
        grid_spec=pltpu.PrefetchScalarGridSpec(
            num_scalar_prefetch=3, grid=(nb,),
            in_specs=[row, pl.BlockSpec(router_wt.shape, lambda j, ea, eb, na: (0, 0)),
                      pl.BlockSpec((None, D, F), wa), pl.BlockSpec((None, D, F), wa), pl.BlockSpec((None, F, D), wa),
                      pl.BlockSpec((None, D, F), wb), pl.BlockSpec((None, D, F), wb), pl.BlockSpec((None, F, D), wb)],
            out_specs=row),
        out_shape=jax.ShapeDtypeStruct(xs.shape, F32),
        compiler_params=_cparams(),
        name="moe_ffn",
    )(blk_ea, blk_eb, n_active, xs, router_wt, w_gate, w_up, w_down, w_gate, w_up, w_down)


def _moe(x1, mod, g, router_w, router_bias, w_gate, w_up, w_down, T):
    N, D = x1.shape
    h2, cls, rank, cnt = _route(x1, mod, g, router_w, router_bias, T)
    counts = cnt.astype(jnp.int32)
    padded = (counts + FFN_BLOCK - 1) // FFN_BLOCK * FFN_BLOCK
    ends = jnp.cumsum(padded)
    starts = ends - padded
    dest = starts[cls] + rank
    nb = N // FFN_BLOCK + N_CLASSES
    n_active = (ends[-1] // FFN_BLOCK).astype(jnp.int32)
    blk = jnp.arange(nb, dtype=jnp.int32)
    blk_cls = jnp.searchsorted(ends, jnp.minimum(blk, n_active - 1) * FFN_BLOCK, side="right").astype(jnp.int32)
    blk_cls = jnp.minimum(blk_cls, N_CLASSES - 1)
    pair_a = jnp.array([p[0] for p in PAIRS], jnp.int32)
    pair_b = jnp.array([p[1] for p in PAIRS], jnp.int32)
    grp = blk_cls // len(PAIRS)
    blk_ea = grp * EXPERTS_PER_GROUP + pair_a[blk_cls % len(PAIRS)]
    blk_eb = grp * EXPERTS_PER_GROUP + pair_b[blk_cls % len(PAIRS)]
    rpt = D // LANES
    xs = _scatter_rows(h2, dest, nb * FFN_BLOCK, rpt)
    ys = _ffn(xs, blk_ea, blk_eb, n_active.reshape(1), jnp.transpose(router_w),
              _cast_bf16(w_gate), _cast_bf16(w_up), _cast_bf16(w_down))
    return _gather_rows(ys, dest, rpt)


def _final_kernel(x_ref, y_ref, pm_ref, g_ref, o_ref):
    x = x_ref[...]
    x = x + pm_ref[5:6, :] * _tt_load(y_ref, *x.shape)
    ms = jnp.mean(x * x, axis=-1, keepdims=True)
    o_ref[...] = x * lax.rsqrt(ms + EPS) * g_ref[...]


def _final(x, y, mod, g, T):
    N, D = x.shape
    tm = min(1024, T)
    row = pl.BlockSpec((tm, D), lambda i: (i, 0))
    modspec = pl.BlockSpec((None, 6, D), lambda i: ((i * tm) // T, 0, 0))
    return pl.pallas_call(
        _final_kernel,
        grid=(N // tm,),
        in_specs=[row, pl.BlockSpec((tm * D // LANES, LANES), lambda i: (i, 0)), modspec, _full((1, D))],
        out_specs=row,
        out_shape=jax.ShapeDtypeStruct((N, D), F32),
        compiler_params=_cparams(),
        name="final_norm",
    )(x, y, mod, g.reshape(1, D))


def kernel(x, c, ada_w, ada_b, norm_g, final_norm_g, pool_w_in, pool_w_grp, pool_scale, pool_w_out, hgrn_w_in, hgrn_lb_logits, hgrn_norm_g, hgrn_w_out, swa_w_in, swa_sinks, swa_w_out, conv_w_in, conv_w, conv_w_out, router_w, router_bias, moe_w_gate, moe_w_up, moe_w_down):
    B, T, D = x.shape
    depth = ada_w.shape[0]
    n_mixers = 4
    mod = _ada(c, ada_w, ada_b).reshape(depth, B, 6, D)
    xt = x.reshape(B * T, D)
    prev = None
    for i in range(depth):
        m, j = i % n_mixers, i // n_mixers
        g1n = norm_g[i, 0]
        if m == 0:
            x1 = _pool_mixer(xt, prev, mod[i], g1n, pool_w_in[j], pool_w_grp[j], pool_scale[j], pool_w_out[j], T)
        elif m == 1:
            x1 = _hgrn_mixer(i, xt, prev, mod[i], g1n, hgrn_w_in[j], hgrn_lb_logits, hgrn_norm_g[j],
                             hgrn_w_out[j], T)
        elif m == 2:
            x1 = _swa_mixer(xt, prev, mod[i], g1n, swa_w_in[j], swa_sinks[j], swa_w_out[j], T)
        else:
            x1 = _conv_mixer(xt, prev, mod[i], g1n, conv_w_in[j], conv_w[j], conv_w_out[j], T)
        y = _moe(x1, mod[i], norm_g[i, 1], router_w, router_bias, moe_w_gate[i], moe_w_up[i], moe_w_down[i], T)
        xt = x1
        prev = (y, mod[i])
    out = _final(xt, prev[0], prev[1], final_norm_g, T)
    return out.reshape(B, T, D)
```

```python
import functools

import jax
import jax.numpy as jnp
from jax import lax
from jax.experimental import pallas as pl
from jax.experimental.pallas import tpu as pltpu

F32 = jnp.float32
BF16 = jnp.bfloat16
EPS = 1e-6

POOL_WINDOWS = (2, 4, 8, 16)
POOL_HALO = 16
HGRN_HEAD_DIM = 128
HGRN_CHUNK = 64
SWA_HEADS = 16
SWA_KV_HEADS = 4
SWA_WINDOW = 128
CONV_WIDTH = 3
CONV_HALO = 8
N_EXPERTS = 16
N_GROUPS = 4
EXPERTS_PER_GROUP = 4
PAIRS = ((0, 1), (0, 2), (0, 3), (1, 2), (1, 3), (2, 3))
N_CLASSES = N_GROUPS * len(PAIRS)
CLASS_ROWS = 32
LANES = 128
FFN_BLOCK = 256
PERM_CHUNK = 2048
VMEM_LIMIT = 52 * 1024 * 1024


def _cparams():
    return pltpu.CompilerParams(dimension_semantics=("arbitrary",), vmem_limit_bytes=VMEM_LIMIT)


def _full(shape):
    nd = len(shape)
    return pl.BlockSpec(shape, lambda i, *_: (0,) * nd)


def _norm_mod(x, g, sc, sh):
    ms = jnp.mean(x * x, axis=-1, keepdims=True)
    return x * lax.rsqrt(ms + EPS) * (g * (1.0 + sc)) + sh


def _sigmoid(x):
    return 1.0 / (1.0 + jnp.exp(-x))


def _silu(x):
    return x * _sigmoid(x)


def _dot(a, b):
    return jnp.dot(a.astype(BF16), b.astype(BF16), preferred_element_type=F32)


def _dot_nt(a, b):
    return lax.dot_general(a.astype(BF16), b.astype(BF16), (((1,), (1,)), ((), ())), preferred_element_type=F32)


def _dot_tn(a, b):
    return lax.dot_general(a.astype(BF16), b.astype(BF16), (((0,), (0,)), ((), ())), preferred_element_type=F32)


def _tt_load(ref, rows, d):
    n = d // LANES
    return jnp.concatenate([ref[pl.ds(j, rows, stride=n), :] for j in range(n)], axis=1)


def _tt_store(ref, val):
    rows, d = val.shape
    n = d // LANES
    for j in range(n):
        ref[pl.ds(j, rows, stride=n), :] = val[:, j * LANES:(j + 1) * LANES]


def _cast_kernel(w_ref, o_ref):
    o_ref[...] = w_ref[...].astype(BF16)


def _cast_bf16(w):
    E, K, M = w.shape
    spec = pl.BlockSpec((None, K, M), lambda e: (e, 0, 0))
    return pl.pallas_call(
        _cast_kernel,
        grid=(E,),
        in_specs=[spec],
        out_specs=spec,
        out_shape=jax.ShapeDtypeStruct(w.shape, BF16),
        compiler_params=_cparams(),
        name="cast_bf16",
    )(w)


def _ada_kernel(c_ref, w_ref, b_ref, o_ref):
    cond = _silu(c_ref[...])
    o_ref[...] = jnp.dot(cond, w_ref[...], preferred_element_type=F32, precision=lax.Precision.HIGHEST) + b_ref[...]


def _ada(c, ada_w, ada_b):
    L, D, D6 = ada_w.shape
    B = c.shape[0]
    bn = D6 // 4
    return pl.pallas_call(
        _ada_kernel,
        grid=(L, D6 // bn),
        in_specs=[pl.BlockSpec((B, D), lambda l, j: (0, 0)),
                  pl.BlockSpec((None, D, bn), lambda l, j: (l, 0, j)),
                  pl.BlockSpec((None, 1, bn), lambda l, j: (l, 0, j))],
        out_specs=pl.BlockSpec((None, B, bn), lambda l, j: (l, 0, j)),
        out_shape=jax.ShapeDtypeStruct((L, B, D6), F32),
        compiler_params=pltpu.CompilerParams(dimension_semantics=("arbitrary", "arbitrary"),
                                             vmem_limit_bytes=VMEM_LIMIT),
        name="ada_mod",
    )(c, ada_w, ada_b.reshape(L, 1, D6))


def _mixer_call(body, name, tm, T, x, prev, mod, g, weights, scratch, smem=()):
    N, D = x.shape
    row = pl.BlockSpec((tm, D), lambda i: (i, 0))
    modspec = pl.BlockSpec((None, 6, D), lambda i: ((i * tm) // T, 0, 0))
    args, specs = [x], [row]
    if prev is not None:
        args += [prev[0], prev[1]]
        specs += [pl.BlockSpec((tm * D // LANES, LANES), lambda i: (i, 0)), modspec]
    args += [mod, g.reshape(1, D)]
    specs += [modspec, _full((1, D))]
    for w in weights:
        args.append(w)
        specs.append(_full(w.shape))
    for s in smem:
        args.append(s)
        specs.append(pl.BlockSpec(memory_space=pltpu.SMEM))
    return pl.pallas_call(
        functools.partial(body, prev is not None, tm, T),
        grid=(N // tm,),
        in_specs=specs,
        out_specs=row,
        out_shape=jax.ShapeDtypeStruct((N, D), F32),
        scratch_shapes=scratch,
        compiler_params=_cparams(),
        name=name,
    )(*args)


def _mixer_input(has_prev, refs):
    if has_prev:
        x_ref, y_ref, pm_ref, m_ref, g_ref = refs[:5]
        x = x_ref[...]
        x = x + pm_ref[5:6, :] * _tt_load(y_ref, *x.shape)
        rest = refs[5:]
    else:
        x_ref, m_ref, g_ref = refs[:3]
        x = x_ref[...]
        rest = refs[3:]
    return x, m_ref, g_ref, rest


def _pool_kernel(has_prev, tm, T, *refs):
    x, m_ref, g_ref, (win_ref, wgrp_ref, scale_ref, wout_ref, o_ref, tail_ref) = _mixer_input(has_prev, refs)
    i = pl.program_id(0)
    start = (i * tm) % T
    h = _norm_mod(x, g_ref[...], m_ref[1:2, :], m_ref[0:1, :])
    u = _dot(h, win_ref[...])

    @pl.when(start == 0)
    def _():
        tail_ref[...] = jnp.zeros_like(tail_ref)

    pos = start + lax.broadcasted_iota(jnp.int32, (tm, 1), 0)
    C = u.shape[1] // len(POOL_WINDOWS)
    ys = []
    for gi, w in enumerate(POOL_WINDOWS):
        ug = u[:, gi * C:(gi + 1) * C]
        s = jnp.concatenate([tail_ref[:, gi * C:(gi + 1) * C], ug], axis=0)
        k = 1
        while k < w:
            s = s + pltpu.roll(s, k, 0)
            k *= 2
        cnt = jnp.minimum(pos + 1, w).astype(F32)
        pooled = s[POOL_HALO:] / cnt - ug
        ys.append(_dot(pooled, wgrp_ref[gi]))
    tail_ref[...] = u[tm - POOL_HALO:, :]
    y = jnp.concatenate(ys, axis=1) * scale_ref[...]
    o_ref[...] = x + m_ref[2:3, :] * _dot(y, wout_ref[...])


def _pool_mixer(x, prev, mod, g, w_in, w_grp, scale, w_out, T):
    D = x.shape[1]
    weights = [w_in.astype(BF16), w_grp.astype(BF16), scale.reshape(1, D), w_out.astype(BF16)]
    return _mixer_call(_pool_kernel, "mixer_pool", 512, T, x, prev, mod, g, weights,
                       [pltpu.VMEM((POOL_HALO, D), F32)])


def _conv_kernel(has_prev, tm, T, *refs):
    x, m_ref, g_ref, (win_ref, cw_ref, wout_ref, o_ref, tail_ref) = _mixer_input(has_prev, refs)
    i = pl.program_id(0)
    D = x.shape[1]
    h = _norm_mod(x, g_ref[...], m_ref[1:2, :], m_ref[0:1, :])
    bcu = _dot(h, win_ref[...])
    z = bcu[:, D:2 * D] * bcu[:, 2 * D:]

    @pl.when((i * tm) % T == 0)
    def _():
        tail_ref[...] = jnp.zeros_like(tail_ref)

    ze = jnp.concatenate([tail_ref[...], z], axis=0)
    zc = cw_ref[CONV_WIDTH - 1:CONV_WIDTH, :] * ze
    for j in range(1, CONV_WIDTH):
        zc = zc + cw_ref[CONV_WIDTH - 1 - j:CONV_WIDTH - j, :] * pltpu.roll(ze, j, 0)
    tail_ref[...] = z[tm - CONV_HALO:, :]
    y = bcu[:, :D] * zc[CONV_HALO:]
    o_ref[...] = x + m_ref[2:3, :] * _dot(y, wout_ref[...])


def _conv_mixer(x, prev, mod, g, w_in, conv_w, w_out, T):
    D = x.shape[1]
    weights = [w_in.astype(BF16), conv_w, w_out.astype(BF16)]
    return _mixer_call(_conv_kernel, "mixer_conv", 512, T, x, prev, mod, g, weights,
                       [pltpu.VMEM((CONV_HALO, D), F32)])


def _swa_kernel(has_prev, tm, T, *refs):
    x, m_ref, g_ref, (win_ref, wout_ref, sink_ref, o_ref, kt_ref, vt_ref) = _mixer_input(has_prev, refs)
    i = pl.program_id(0)
    D = x.shape[1]
    W = SWA_WINDOW
    hd = D // SWA_HEADS
    G = SWA_HEADS // SWA_KV_HEADS
    kvd = SWA_KV_HEADS * hd
    first_key = jnp.where((i * tm) % T == 0, W, 0)

    @pl.when((i * tm) % T == 0)
    def _():
        kt_ref[...] = jnp.zeros_like(kt_ref)
        vt_ref[...] = jnp.zeros_like(vt_ref)

    h = _norm_mod(x, g_ref[...], m_ref[1:2, :], m_ref[0:1, :])
    qkv = _dot(h, win_ref[...])
    q = qkv[:, :D] * (hd ** -0.5)
    k = qkv[:, D:D + kvd]
    v = qkv[:, D + kvd:]
    ke = jnp.concatenate([kt_ref[...], k], axis=0)
    ve = jnp.concatenate([vt_ref[...], v], axis=0)
    kt_ref[...] = k[tm - W:, :]
    vt_ref[...] = v[tm - W:, :]
    qpos = lax.broadcasted_iota(jnp.int32, (G * W, 2 * W), 0) % W
    kj = lax.broadcasted_iota(jnp.int32, (G * W, 2 * W), 1)
    band = (kj > qpos) & (kj <= qpos + W)
    blocks = []
    for j in range(tm // W):
        mask = band & (kj >= first_key) if j == 0 else band
        heads = []
        for hk in range(SWA_KV_HEADS):
            kc = ke[j * W:(j + 2) * W, hk * hd:(hk + 1) * hd]
            vc = ve[j * W:(j + 2) * W, hk * hd:(hk + 1) * hd]
            qs = jnp.concatenate(
                [q[j * W:(j + 1) * W, (hk * G + gq) * hd:(hk * G + gq + 1) * hd] for gq in range(G)], axis=0)
            sink = jnp.concatenate(
                [jnp.full((W, 1), sink_ref[hk * G + gq], F32) for gq in range(G)], axis=0)
            s = jnp.where(mask, _dot_nt(qs, kc), -jnp.inf)
            mx = jnp.maximum(jnp.max(s, axis=-1, keepdims=True), sink)
            p = jnp.exp(s - mx)
            denom = jnp.sum(p, axis=-1, keepdims=True) + jnp.exp(sink - mx)
            o = _dot(p, vc) / denom
            heads += [o[gq * W:(gq + 1) * W] for gq in range(G)]
        blocks.append(jnp.concatenate(heads, axis=1))
    o_all = jnp.concatenate(blocks, axis=0)
    o_ref[...] = x + m_ref[2:3, :] * _dot(o_all, wout_ref[...])


def _swa_mixer(x, prev, mod, g, w_in, sinks, w_out, T):
    D = x.shape[1]
    kvd = SWA_KV_HEADS * (D // SWA_HEADS)
    weights = [w_in.astype(BF16), w_out.astype(BF16)]
    return _mixer_call(_swa_kernel, "mixer_swa", 256, T, x, prev, mod, g, weights,
                       [pltpu.VMEM((SWA_WINDOW, kvd), F32), pltpu.VMEM((SWA_WINDOW, kvd), F32)], smem=[sinks])


def _chunk_cumsum(x, C):
    ridx = lax.broadcasted_iota(jnp.int32, (x.shape[0], 1), 0) % C
    k = 1
    while k < C:
        x = x + jnp.where(ridx >= k, pltpu.roll(x, k, 0), 0.0)
        k *= 2
    return x


def _hgrn_kernel(layer, has_prev, tm, T, *refs):
    x, m_ref, g_ref, (win_ref, lbl_ref, ng_ref, wout_ref, o_ref, st_ref) = _mixer_input(has_prev, refs)
    i = pl.program_id(0)
    D = x.shape[1]
    dk = HGRN_HEAD_DIM
    C = HGRN_CHUNK

    @pl.when((i * tm) % T == 0)
    def _():
        st_ref[...] = jnp.zeros_like(st_ref)

    rows = [lbl_ref[j:j + 1, :] for j in range(lbl_ref.shape[0])]
    mx = functools.reduce(jnp.maximum, rows)
    es = [jnp.exp(r - mx) for r in rows]
    tot = functools.reduce(lambda a, b: a + b, es)
    lb = jnp.zeros_like(mx)
    for j in range(1, layer + 1):
        lb = lb + es[j] / tot

    h = _norm_mod(x, g_ref[...], m_ref[1:2, :], m_ref[0:1, :])
    proj = _dot(h, win_ref[...])
    causal = lax.broadcasted_iota(jnp.int32, (C, C), 0) >= lax.broadcasted_iota(jnp.int32, (C, C), 1)
    outs = []
    for hh in range(D // dk):
        sl = slice(hh * dk, (hh + 1) * dk)
        q = _silu(proj[:, sl])
        lbh = lb[:, sl]
        f = lbh + (1.0 - lbh) * _sigmoid(proj[:, D + hh * dk:D + (hh + 1) * dk])
        kk = 1.0 - f
        v = proj[:, 2 * D + hh * dk:2 * D + (hh + 1) * dk]
        gate = proj[:, 3 * D + hh * dk:3 * D + (hh + 1) * dk]
        b = _chunk_cumsum(jnp.log(f), C)
        st = st_ref[hh]
        oc = []
        for c in range(tm // C):
            rs = slice(c * C, (c + 1) * C)
            bc, qc, kc, vc = b[rs], q[rs], kk[rs], v[rs]
            ref = bc[C // 2 - 1:C // 2, :]
            last = bc[C - 1:C, :]
            sc = jnp.where(causal, _dot_nt(qc * jnp.exp(bc - ref), kc * jnp.exp(ref - bc)), 0.0)
            oc.append(_dot(sc, vc) + _dot_nt(qc * jnp.exp(bc), st))
            st = st * jnp.exp(last) + _dot_tn(vc, kc * jnp.exp(last - bc))
        st_ref[hh] = st
        o = jnp.concatenate(oc, axis=0)
        o = o * lax.rsqrt(jnp.mean(o * o, axis=-1, keepdims=True) + EPS) * ng_ref[...]
        outs.append(o * _silu(gate))
    y = jnp.concatenate(outs, axis=1)
    o_ref[...] = x + m_ref[2:3, :] * _dot(y, wout_ref[...])


def _hgrn_mixer(layer, x, prev, mod, g, w_in, lb_logits, norm_g, w_out, T):
    D = x.shape[1]
    dk = HGRN_HEAD_DIM
    weights = [w_in.astype(BF16), lb_logits, norm_g.reshape(1, dk), w_out.astype(BF16)]
    return _mixer_call(functools.partial(_hgrn_kernel, layer), "mixer_hgrn", 256, T, x, prev, mod, g, weights,
                       [pltpu.VMEM((D // dk, dk, dk), F32)])


def _split2(a):
    hi = a.astype(BF16)
    lo = (a - hi.astype(F32)).astype(BF16)
    return hi, lo


def _route_kernel(tm, x_ref, m_ref, g_ref, rw_ref, rb_ref, tri_ref, h_ref, cls_ref, rank_ref, cnt_ref, carry_ref):
    i = pl.program_id(0)

    @pl.when(i == 0)
    def _():
        carry_ref[...] = jnp.zeros_like(carry_ref)

    h = _norm_mod(x_ref[...], g_ref[...], m_ref[4:5, :], m_ref[3:4, :])
    _tt_store(h_ref, h)
    h_hi, h_lo = _split2(h)
    w_hi, w_lo = _split2(rw_ref[...])
    logits = (jnp.dot(h_hi, w_hi, preferred_element_type=F32) + jnp.dot(h_lo, w_hi, preferred_element_type=F32)
              + jnp.dot(h_hi, w_lo, preferred_element_type=F32))
    lt = jnp.transpose(logits)[:N_EXPERTS, :]
    score = _sigmoid(lt)
    sel = score + rb_ref[...]
    gscore, gsel = [], []
    for gi in range(N_GROUPS):
        r = [sel[gi * 4 + e:gi * 4 + e + 1, :] for e in range(EXPERTS_PER_GROUP)]
        m1 = functools.reduce(jnp.maximum, r)
        m2 = None
        for a in range(EXPERTS_PER_GROUP):
            for b2 in range(a + 1, EXPERTS_PER_GROUP):
                pm = jnp.minimum(r[a], r[b2])
                m2 = pm if m2 is None else jnp.maximum(m2, pm)
        gscore.append(m1 + m2)
        gsel.append(r)
    best = jnp.zeros((1, tm), jnp.int32)
    bs = gscore[0]
    for gi in range(1, N_GROUPS):
        better = gscore[gi] > bs
        best = jnp.where(better, gi, best)
        bs = jnp.where(better, gscore[gi], bs)
    r = [functools.reduce(lambda a, b2: a + b2,
                          [jnp.where(best == gi, gsel[gi][e], 0.0) for gi in range(N_GROUPS)])
         for e in range(EXPERTS_PER_GROUP)]
    keep = []
    for e in range(EXPERTS_PER_GROUP):
        beaten = jnp.zeros((1, tm), jnp.int32)
        for o in range(EXPERTS_PER_GROUP):
            if o != e:
                wins = (r[o] > r[e]) | ((r[o] == r[e]) & (o < e))
                beaten = beaten + wins.astype(jnp.int32)
        keep.append(beaten < 2)
    pair = jnp.zeros((1, tm), jnp.int32)
    for pi, (a, b2) in enumerate(PAIRS):
        pair = jnp.where(keep[a] & keep[b2], pi, pair)
    cls = best * len(PAIRS) + pair
    cls_ref[...] = cls.reshape(1, 1, tm)
    onehot = (lax.broadcasted_iota(jnp.int32, (CLASS_ROWS, tm), 0) == cls).astype(F32)
    before = jnp.dot(onehot.astype(BF16), tri_ref[...], preferred_element_type=F32) + carry_ref[:, 0:1]
    rank = jnp.sum(onehot * before, axis=0, keepdims=True)
    rank_ref[...] = rank.astype(jnp.int32).reshape(1, 1, tm)
    carry_ref[...] = carry_ref[...] + jnp.sum(onehot, axis=1, keepdims=True)
    cnt_ref[...] = carry_ref[...]


def _route(x1, mod, g, router_w, router_bias, T):
    N, D = x1.shape
    tm = 512
    nt = N // tm
    E = router_w.shape[1]
    rw = jnp.zeros((D, LANES), F32).at[:, :E].set(router_w)
    rb = router_bias.reshape(E, 1)
    tri = (jnp.arange(tm)[:, None] < jnp.arange(tm)[None, :]).astype(BF16)
    row = pl.BlockSpec((tm, D), lambda i: (i, 0))
    modspec = pl.BlockSpec((None, 6, D), lambda i: ((i * tm) // T, 0, 0))
    tok = pl.BlockSpec((1, 1, tm), lambda i: (i, 0, 0))
    h2, cls, rank, cnt = pl.pallas_call(
        functools.partial(_route_kernel, tm),
        grid=(nt,),
        in_specs=[row, modspec, _full((1, D)), _full((D, LANES)), _full((E, 1)), _full((tm, tm))],
        out_specs=[pl.BlockSpec((tm * D // LANES, LANES), lambda i: (i, 0)), tok, tok, _full((CLASS_ROWS, LANES))],
        out_shape=[jax.ShapeDtypeStruct((N * D // LANES, LANES), F32),
                   jax.ShapeDtypeStruct((nt, 1, tm), jnp.int32),
                   jax.ShapeDtypeStruct((nt, 1, tm), jnp.int32),
                   jax.ShapeDtypeStruct((CLASS_ROWS, LANES), F32)],
        scratch_shapes=[pltpu.VMEM((CLASS_ROWS, LANES), F32)],
        compiler_params=_cparams(),
        name="router",
    )(x1, mod, g.reshape(1, D), rw, rb, tri)
    return h2, cls.reshape(N), rank.reshape(N), cnt[:N_CLASSES, 0]


def _tok(ref, t, rpt):
    return ref.at[pl.ds(pl.multiple_of(t * rpt, rpt), rpt), :]


def _scatter_rows_kernel(chunk, rpt, idx_ref, src_ref, init_ref, dst_ref, sem):
    del init_ref
    base = pl.program_id(0) * chunk

    def issue(r, carry):
        pltpu.make_async_copy(_tok(src_ref, r, rpt), _tok(dst_ref, idx_ref[base + r], rpt), sem).start()
        return carry

    lax.fori_loop(0, chunk, issue, 0, unroll=8)
    pltpu.make_async_copy(src_ref, dst_ref.at[pl.ds(0, chunk * rpt), :], sem).wait()


def _scatter_rows(src, idx, n_out, rpt):
    N = idx.shape[0]
    chunk = min(PERM_CHUNK, N)
    init = jnp.zeros((n_out * rpt, LANES), src.dtype)
    return pl.pallas_call(
        functools.partial(_scatter_rows_kernel, chunk, rpt),
        grid_spec=pltpu.PrefetchScalarGridSpec(
            num_scalar_prefetch=1, grid=(N // chunk,),
            in_specs=[pl.BlockSpec((chunk * rpt, LANES), lambda i, idx: (i, 0)), pl.BlockSpec(memory_space=pl.ANY)],
            out_specs=pl.BlockSpec(memory_space=pl.ANY),
            scratch_shapes=[pltpu.SemaphoreType.DMA(())]),
        out_shape=jax.ShapeDtypeStruct(init.shape, src.dtype),
        input_output_aliases={2: 0},
        compiler_params=_cparams(),
        name="scatter_rows",
    )(idx, src, init)


def _gather_rows_kernel(chunk, rpt, idx_ref, src_ref, dst_ref, sem):
    base = pl.program_id(0) * chunk

    def issue(r, carry):
        pltpu.make_async_copy(_tok(src_ref, idx_ref[base + r], rpt), _tok(dst_ref, r, rpt), sem).start()
        return carry

    lax.fori_loop(0, chunk, issue, 0, unroll=8)
    pltpu.make_async_copy(src_ref.at[pl.ds(0, chunk * rpt), :], dst_ref, sem).wait()


def _gather_rows(src, idx, rpt):
    N = idx.shape[0]
    chunk = min(PERM_CHUNK, N)
    return pl.pallas_call(
        functools.partial(_gather_rows_kernel, chunk, rpt),
        grid_spec=pltpu.PrefetchScalarGridSpec(
            num_scalar_prefetch=1, grid=(N // chunk,),
            in_specs=[pl.BlockSpec(memory_space=pl.ANY)],
            out_specs=pl.BlockSpec((chunk * rpt, LANES), lambda i, idx: (i, 0)),
            scratch_shapes=[pltpu.SemaphoreType.DMA(())]),
        out_shape=jax.ShapeDtypeStruct((N * rpt, LANES), src.dtype),
        compiler_params=_cparams(),
        name="gather_rows",
    )(idx, src)


def _ffn_kernel(ea_ref, eb_ref, nact_ref, x_ref, rwt_ref, wga_ref, wua_ref, wda_ref, wgb_ref, wub_ref, wdb_ref,
                o_ref):
    j = pl.program_id(0)

    @pl.when(j < nact_ref[0])
    def _():
        x = _tt_load(x_ref, FFN_BLOCK, wga_ref.shape[0])
        sa = _sigmoid(jnp.sum(x * rwt_ref[pl.ds(ea_ref[j], 1), :], axis=-1, keepdims=True))
        sb = _sigmoid(jnp.sum(x * rwt_ref[pl.ds(eb_ref[j], 1), :], axis=-1, keepdims=True))
        inv = 1.0 / (sa + sb)
        xb = x.astype(BF16)
        aa = _silu(_dot(xb, wga_ref[...])) * _dot(xb, wua_ref[...]) * (sa * inv)
        ab = _silu(_dot(xb, wgb_ref[...])) * _dot(xb, wub_ref[...]) * (sb * inv)
        _tt_store(o_ref, _dot(aa, wda_ref[...]) + _dot(ab, wdb_ref[...]))

    @pl.when(j >= nact_ref[0])
    def _():
        o_ref[...] = jnp.zeros_like(o_ref)


def _ffn(xs, blk_ea, blk_eb, n_active, router_wt, w_gate, w_up, w_down):
    D, F = w_gate.shape[1:]
    rpt = D // LANES
    nb = xs.shape[0] // (FFN_BLOCK * rpt)

    def wa(j, ea, eb, na):
        return (ea[j], 0, 0)

    def wb(j, ea, eb, na):
        return (eb[j], 0, 0)

    row = pl.BlockSpec((FFN_BLOCK * rpt, LANES), lambda j, ea, eb, na: (j, 0))
    return pl.pallas_call(
        _ffn_kernel,
        grid_spec=pltpu.PrefetchScalarGridSpec(
            num_scalar_prefetch=3, grid=(nb,),
            in_specs=[row, pl.BlockSpec(router_wt.shape, lambda j, ea, eb, na: (0, 0)),
                      pl.BlockSpec((None, D, F), wa), pl.BlockSpec((None, D, F), wa), pl.BlockSpec((None, F, D), wa),
                      pl.BlockSpec((None, D, F), wb), pl.BlockSpec((None, D, F), wb), pl.BlockSpec((None, F, D), wb)],
            out_specs=row),
        out_shape=jax.ShapeDtypeStruct(xs.shape, F32),
        compiler_params=_cparams(),
        name="moe_ffn",
    )(blk_ea, blk_eb, n_active, xs, router_wt, w_gate, w_up, w_down, w_gate, w_up, w_down)


def _moe(x1, mod, g, router_w, router_bias, w_gate, w_up, w_down, T):
    N, D = x1.shape
    h2, cls, rank, cnt = _route(x1, mod, g, router_w, router_bias, T)
    counts = cnt.astype(jnp.int32)
    padded = (counts + FFN_BLOCK - 1) // FFN_BLOCK * FFN_BLOCK
    ends = jnp.cumsum(padded)
    starts = ends - padded
    dest = starts[cls] + rank
    nb = N // FFN_BLOCK + N_CLASSES
    n_active = (ends[-1] // FFN_BLOCK).astype(jnp.int32)
    blk = jnp.arange(nb, dtype=jnp.int32)
    blk_cls = jnp.searchsorted(ends, jnp.minimum(blk, n_active - 1) * FFN_BLOCK, side="right").astype(jnp.int32)
    blk_cls = jnp.minimum(blk_cls, N_CLASSES - 1)
    pair_a = jnp.array([p[0] for p in PAIRS], jnp.int32)
    pair_b = jnp.array([p[1] for p in PAIRS], jnp.int32)
    grp = blk_cls // len(PAIRS)
    blk_ea = grp * EXPERTS_PER_GROUP + pair_a[blk_cls % len(PAIRS)]
    blk_eb = grp * EXPERTS_PER_GROUP + pair_b[blk_cls % len(PAIRS)]
    rpt = D // LANES
    xs = _scatter_rows(h2, dest, nb * FFN_BLOCK, rpt)
    ys = _ffn(xs, blk_ea, blk_eb, n_active.reshape(1), jnp.transpose(router_w),
              _cast_bf16(w_gate), _cast_bf16(w_up), _cast_bf16(w_down))
    return _gather_rows(ys, dest, rpt)


def _final_kernel(x_ref, y_ref, pm_ref, g_ref, o_ref):
    x = x_ref[...]
    x = x + pm_ref[5:6, :] * _tt_load(y_ref, *x.shape)
    ms = jnp.mean(x * x, axis=-1, keepdims=True)
    o_ref[...] = x * lax.rsqrt(ms + EPS) * g_ref[...]


def _final(x, y, mod, g, T):
    N, D = x.shape
    tm = min(1024, T)
    row = pl.BlockSpec((tm, D), lambda i: (i, 0))
    modspec = pl.BlockSpec((None, 6, D), lambda i: ((i * tm) // T, 0, 0))
    return pl.pallas_call(
        _final_kernel,
        grid=(N // tm,),
        in_specs=[row, pl.BlockSpec((tm * D // LANES, LANES), lambda i: (i, 0)), modspec, _full((1, D))],
        out_specs=row,
        out_shape=jax.ShapeDtypeStruct((N, D), F32),
        compiler_params=_cparams(),
        name="final_norm",
    )(x, y, mod, g.reshape(1, D))


def kernel(x, c, ada_w, ada_b, norm_g, final_norm_g, pool_w_in, pool_w_grp, pool_scale, pool_w_out, hgrn_w_in, hgrn_lb_logits, hgrn_norm_g, hgrn_w_out, swa_w_in, swa_sinks, swa_w_out, conv_w_in, conv_w, conv_w_out, router_w, router_bias, moe_w_gate, moe_w_up, moe_w_down):
    B, T, D = x.shape
    depth = ada_w.shape[0]
    n_mixers = 4
    mod = _ada(c, ada_w, ada_b).reshape(depth, B, 6, D)
    xt = x.reshape(B * T, D)
    prev = None
    for i in range(depth):
        m, j = i % n_mixers, i // n_mixers
        g1n = norm_g[i, 0]
        if m == 0:
            x1 = _pool_mixer(xt, prev, mod[i], g1n, pool_w_in[j], pool_w_grp[j], pool_scale[j], pool_w_out[j], T)
        elif m == 1:
            x1 = _hgrn_mixer(i, xt, prev, mod[i], g1n, hgrn_w_in[j], hgrn_lb_logits, hgrn_norm_g[j],
                             hgrn_w_out[j], T)
        elif m == 2:
            x1 = _swa_mixer(xt, prev, mod[i], g1n, swa_w_in[j], swa_sinks[j], swa_w_out[j], T)
        else:
            x1 = _conv_mixer(xt, prev, mod[i], g1n, conv_w_in[j], conv_w[j], conv_w_out[j], T)
        y = _moe(x1, mod[i], norm_g[i, 1], router_w, router_bias, moe_w_gate[i], moe_w_up[i], moe_w_down[i], T)
        xt = x1
        prev = (y, mod[i])
    out = _final(xt, prev[0], prev[1], final_norm_g, T)
    return out.reshape(B, T, D)
```

```python
import functools

import jax
import jax.numpy as jnp
from jax import lax
from jax.experimental import pallas as pl
from jax.experimental.pallas import tpu as pltpu

F32 = jnp.float32
BF16 = jnp.bfloat16
EPS = 1e-6
LOG2E = 1.4426950408889634

POOL_WINDOWS = (2, 4, 8, 16)
POOL_HALO = 16
HGRN_HEAD_DIM = 128
HGRN_CHUNK = 64
SWA_HEADS = 16
SWA_KV_HEADS = 4
SWA_WINDOW = 128
CONV_WIDTH = 3
CONV_HALO = 8
N_EXPERTS = 16
N_GROUPS = 4
EXPERTS_PER_GROUP = 4
PAIRS = ((0, 1), (0, 2), (0, 3), (1, 2), (1, 3), (2, 3))
N_CLASSES = N_GROUPS * len(PAIRS)
CLASS_ROWS = 32
LANES = 128
FFN_BLOCK = 256
PERM_CHUNK = 2048
VMEM_LIMIT = 52 * 1024 * 1024


def _cparams():
    return pltpu.CompilerParams(dimension_semantics=("arbitrary",), vmem_limit_bytes=VMEM_LIMIT)


def _full(shape):
    nd = len(shape)
    return pl.BlockSpec(shape, lambda i, *_: (0,) * nd)


def _norm_mod(x, g, sc, sh):
    ms = jnp.mean(x * x, axis=-1, keepdims=True)
    return x * lax.rsqrt(ms + EPS) * (g * (1.0 + sc)) + sh


def _sigmoid(x):
    return 1.0 / (1.0 + jnp.exp(-x))


def _silu(x):
    return x * _sigmoid(x)


def _dot(a, b):
    return jnp.dot(a.astype(BF16), b.astype(BF16), preferred_element_type=F32)


def _dot_nt(a, b):
    return lax.dot_general(a.astype(BF16), b.astype(BF16), (((1,), (1,)), ((), ())), preferred_element_type=F32)


def _dot_tn(a, b):
    return lax.dot_general(a.astype(BF16), b.astype(BF16), (((0,), (0,)), ((), ())), preferred_element_type=F32)


def _tt_load(ref, rows, d):
    n = d // LANES
    return jnp.concatenate([ref[pl.ds(j, rows, stride=n), :] for j in range(n)], axis=1)


def _tt_store(ref, val):
    rows, d = val.shape
    n = d // LANES
    for j in range(n):
        ref[pl.ds(j, rows, stride=n), :] = val[:, j * LANES:(j + 1) * LANES]


def _cast_kernel(w_ref, o_ref):
    o_ref[...] = w_ref[...].astype(BF16)


def _cast_bf16(w, layer):
    _, E, K, M = w.shape
    return pl.pallas_call(
        _cast_kernel,
        grid=(E,),
        in_specs=[pl.BlockSpec((None, None, K, M), lambda e: (layer, e, 0, 0))],
        out_specs=pl.BlockSpec((None, K, M), lambda e: (e, 0, 0)),
        out_shape=jax.ShapeDtypeStruct((E, K, M), BF16),
        compiler_params=_cparams(),
        name="cast_bf16",
    )(w)


def _ada_kernel(c_ref, w_ref, b_ref, o_ref):
    cond = _silu(c_ref[...])
    o_ref[...] = jnp.dot(cond, w_ref[...], preferred_element_type=F32, precision=lax.Precision.HIGHEST) + b_ref[...]


def _ada(c, ada_w, ada_b):
    L, D, D6 = ada_w.shape
    B = c.shape[0]
    bn = D6 // 4
    return pl.pallas_call(
        _ada_kernel,
        grid=(L, D6 // bn),
        in_specs=[pl.BlockSpec((B, D), lambda l, j: (0, 0)),
                  pl.BlockSpec((None, D, bn), lambda l, j: (l, 0, j)),
                  pl.BlockSpec((None, 1, bn), lambda l, j: (l, 0, j))],
        out_specs=pl.BlockSpec((None, B, bn), lambda l, j: (l, 0, j)),
        out_shape=jax.ShapeDtypeStruct((L, B, D6), F32),
        compiler_params=pltpu.CompilerParams(dimension_semantics=("arbitrary", "arbitrary"),
                                             vmem_limit_bytes=VMEM_LIMIT),
        name="ada_mod",
    )(c, ada_w, ada_b.reshape(L, 1, D6))


def _mixer_call(body, name, tm, T, x, prev, mod, g, weights, scratch, smem=()):
    N, D = x.shape
    row = pl.BlockSpec((tm, D), lambda i: (i, 0))
    modspec = pl.BlockSpec((None, 6, D), lambda i: ((i * tm) // T, 0, 0))
    args, specs = [x], [row]
    if prev is not None:
        args += [prev[0], prev[1]]
        specs += [pl.BlockSpec((tm * D // LANES, LANES), lambda i: (i, 0)), modspec]
    args += [mod, g.reshape(1, D)]
    specs += [modspec, _full((1, D))]
    for w in weights:
        args.append(w)
        specs.append(_full(w.shape))
    for s in smem:
        args.append(s)
        specs.append(pl.BlockSpec(memory_space=pltpu.SMEM))
    return pl.pallas_call(
        functools.partial(body, prev is not None, tm, T),
        grid=(N // tm,),
        in_specs=specs,
        out_specs=row,
        out_shape=jax.ShapeDtypeStruct((N, D), F32),
        scratch_shapes=scratch,
        compiler_params=_cparams(),
        name=name,
    )(*args)


def _mixer_input(has_prev, refs):
    if has_prev:
        x_ref, y_ref, pm_ref, m_ref, g_ref = refs[:5]
        x = x_ref[...]
        x = x + pm_ref[5:6, :] * _tt_load(y_ref, *x.shape)
        rest = refs[5:]
    else:
        x_ref, m_ref, g_ref = refs[:3]
        x = x_ref[...]
        rest = refs[3:]
    return x, m_ref, g_ref, rest


def _pool_kernel(has_prev, tm, T, *refs):
    x, m_ref, g_ref, (win_ref, wgrp_ref, scale_ref, wout_ref, o_ref, tail_ref) = _mixer_input(has_prev, refs)
    i = pl.program_id(0)
    start = (i * tm) % T
    h = _norm_mod(x, g_ref[...], m_ref[1:2, :], m_ref[0:1, :])
    u = _dot(h, win_ref[...])

    @pl.when(start == 0)
    def _():
        tail_ref[...] = jnp.zeros_like(tail_ref)

    pos = start + lax.broadcasted_iota(jnp.int32, (tm, 1), 0)
    C = u.shape[1] // len(POOL_WINDOWS)
    ys = []
    for gi, w in enumerate(POOL_WINDOWS):
        ug = u[:, gi * C:(gi + 1) * C]
        s = jnp.concatenate([tail_ref[:, gi * C:(gi + 1) * C], ug], axis=0)
        k = 1
        while k < w:
            s = s + pltpu.roll(s, k, 0)
            k *= 2
        cnt = jnp.minimum(pos + 1, w).astype(F32)
        pooled = s[POOL_HALO:] / cnt - ug
        ys.append(_dot(pooled, wgrp_ref[gi]))
    tail_ref[...] = u[tm - POOL_HALO:, :]
    y = jnp.concatenate(ys, axis=1) * scale_ref[...]
    o_ref[...] = x + m_ref[2:3, :] * _dot(y, wout_ref[...])


def _pool_mixer(x, prev, mod, g, w_in, w_grp, scale, w_out, T):
    D = x.shape[1]
    weights = [w_in.astype(BF16), w_grp.astype(BF16), scale.reshape(1, D), w_out.astype(BF16)]
    return _mixer_call(_pool_kernel, "mixer_pool", 512, T, x, prev, mod, g, weights,
                       [pltpu.VMEM((POOL_HALO, D), F32)])


def _conv_kernel(has_prev, tm, T, *refs):
    x, m_ref, g_ref, (win_ref, cw_ref, wout_ref, o_ref, tail_ref) = _mixer_input(has_prev, refs)
    i = pl.program_id(0)
    D = x.shape[1]
    h = _norm_mod(x, g_ref[...], m_ref[1:2, :], m_ref[0:1, :])
    bcu = _dot(h, win_ref[...])
    z = bcu[:, D:2 * D] * bcu[:, 2 * D:]

    @pl.when((i * tm) % T == 0)
    def _():
        tail_ref[...] = jnp.zeros_like(tail_ref)

    ze = jnp.concatenate([tail_ref[...], z], axis=0)
    zc = cw_ref[CONV_WIDTH - 1:CONV_WIDTH, :] * ze
    for j in range(1, CONV_WIDTH):
        zc = zc + cw_ref[CONV_WIDTH - 1 - j:CONV_WIDTH - j, :] * pltpu.roll(ze, j, 0)
    tail_ref[...] = z[tm - CONV_HALO:, :]
    y = bcu[:, :D] * zc[CONV_HALO:]
    o_ref[...] = x + m_ref[2:3, :] * _dot(y, wout_ref[...])


def _conv_mixer(x, prev, mod, g, w_in, conv_w, w_out, T):
    D = x.shape[1]
    weights = [w_in.astype(BF16), conv_w, w_out.astype(BF16)]
    return _mixer_call(_conv_kernel, "mixer_conv", 512, T, x, prev, mod, g, weights,
                       [pltpu.VMEM((CONV_HALO, D), F32)])


def _swa_kernel(has_prev, tm, T, *refs):
    x, m_ref, g_ref, (win_ref, wout_ref, sink_ref, o_ref, q_scr, klo, khi, vlo, vhi, o_scr) = \
        _mixer_input(has_prev, refs)
    i = pl.program_id(0)
    D = x.shape[1]
    W = SWA_WINDOW
    hd = D // SWA_HEADS
    G = SWA_HEADS // SWA_KV_HEADS
    kvd = SWA_KV_HEADS * hd
    assert 2 * hd == LANES
    seq_start = (i * tm) % T == 0
    kv_scr = (klo, khi, vlo, vhi)

    @pl.when(seq_start)
    def _():
        for r in kv_scr:
            r[0:W, :] = jnp.zeros((W, r.shape[1]), BF16)

    @pl.when(jnp.logical_not(seq_start))
    def _():
        for r in kv_scr:
            r[0:W, :] = r[tm:tm + W, :]

    h = _norm_mod(x, g_ref[...], m_ref[1:2, :], m_ref[0:1, :])
    qkv = _dot(h, win_ref[...])
    q_scr[...] = (qkv[:, :D] * (hd ** -0.5 * LOG2E)).astype(BF16)
    low = lax.broadcasted_iota(jnp.int32, (tm, LANES), 1) < hd
    for src, lo_ref, hi_ref in ((qkv[:, D:D + kvd], klo, khi), (qkv[:, D + kvd:], vlo, vhi)):
        for a in range(kvd // LANES):
            kg = src[:, a * LANES:(a + 1) * LANES]
            sw = pltpu.roll(kg, hd, 1)
            for par, (lo_v, hi_v) in enumerate(((kg, sw), (sw, kg))):
                c = (2 * a + par) * LANES
                lo_ref[W:, c:c + LANES] = jnp.where(low, lo_v, 0.0).astype(BF16)
                hi_ref[W:, c:c + LANES] = jnp.where(low, 0.0, hi_v).astype(BF16)
    qpos = lax.broadcasted_iota(jnp.int32, (W, 2 * W), 0)
    kj = lax.broadcasted_iota(jnp.int32, (W, 2 * W), 1)
    band = (kj > qpos) & (kj <= qpos + W)
    bias = jnp.where(band, 0.0, -jnp.inf)
    bias0 = jnp.where(band & (kj >= jnp.where(seq_start, W, 0)), 0.0, -jnp.inf)
    for j in range(tm // W):
        b = bias0 if j == 0 else bias
        for grp in range(SWA_HEADS // 2):
            hk = (2 * grp) // G
            qg = q_scr[j * W:(j + 1) * W, grp * LANES:(grp + 1) * LANES]
            pair = None
            for par, (k_ref, v_ref) in enumerate(((klo, vlo), (khi, vhi))):
                kk = k_ref[j * W:(j + 2) * W, hk * LANES:(hk + 1) * LANES]
                vv = v_ref[j * W:(j + 2) * W, hk * LANES:(hk + 1) * LANES]
                sink = sink_ref[2 * grp + par] * LOG2E
                s = lax.dot_general(qg, kk, (((1,), (1,)), ((), ())), preferred_element_type=F32) + b
                mx = jnp.maximum(jnp.max(s, axis=-1, keepdims=True), sink)
                p = jnp.exp2(s - mx)
                denom = jnp.sum(p, axis=-1, keepdims=True) + jnp.exp2(sink - mx)
                o = jnp.dot(p.astype(BF16), vv, preferred_element_type=F32) * (1.0 / denom)
                pair = o if pair is None else pair + o
            o_scr[j * W:(j + 1) * W, grp * LANES:(grp + 1) * LANES] = pair.astype(BF16)
    o_ref[...] = x + m_ref[2:3, :] * jnp.dot(o_scr[...], wout_ref[...], preferred_element_type=F32)


def _swa_mixer(x, prev, mod, g, w_in, sinks, w_out, T):
    D = x.shape[1]
    tm = 256
    rows = SWA_WINDOW + tm
    kv = pltpu.VMEM((rows, SWA_KV_HEADS * LANES), BF16)
    weights = [w_in.astype(BF16), w_out.astype(BF16)]
    return _mixer_call(_swa_kernel, "mixer_swa", tm, T, x, prev, mod, g, weights,
                       [pltpu.VMEM((tm, D), BF16), kv, kv, kv, kv, pltpu.VMEM((tm, D), BF16)], smem=[sinks])


def _chunk_cumsum(x, C):
    ridx = lax.broadcasted_iota(jnp.int32, (x.shape[0], 1), 0) % C
    k = 1
    while k < C:
        x = x + jnp.where(ridx >= k, pltpu.roll(x, k, 0), 0.0)
        k *= 2
    return x


def _hgrn_kernel(layer, has_prev, tm, T, *refs):
    x, m_ref, g_ref, (win_ref, lbl_ref, ng_ref, wout_ref, o_ref, st_ref) = _mixer_input(has_prev, refs)
    i = pl.program_id(0)
    D = x.shape[1]
    dk = HGRN_HEAD_DIM
    C = HGRN_CHUNK

    @pl.when((i * tm) % T == 0)
    def _():
        st_ref[...] = jnp.zeros_like(st_ref)

    rows = [lbl_ref[j:j + 1, :] for j in range(lbl_ref.shape[0])]
    mx = functools.reduce(jnp.maximum, rows)
    es = [jnp.exp(r - mx) for r in rows]
    tot = functools.reduce(lambda a, b: a + b, es)
    lb = jnp.zeros_like(mx)
    for j in range(1, layer + 1):
        lb = lb + es[j] / tot

    h = _norm_mod(x, g_ref[...], m_ref[1:2, :], m_ref[0:1, :])
    proj = _dot(h, win_ref[...])
    causal = lax.broadcasted_iota(jnp.int32, (C, C), 0) >= lax.broadcasted_iota(jnp.int32, (C, C), 1)
    outs = []
    for hh in range(D // dk):
        sl = slice(hh * dk, (hh + 1) * dk)
        q = _silu(proj[:, sl])
        lbh = lb[:, sl]
        f = lbh + (1.0 - lbh) * _sigmoid(proj[:, D + hh * dk:D + (hh + 1) * dk])
        kk = 1.0 - f
        v = proj[:, 2 * D + hh * dk:2 * D + (hh + 1) * dk]
        gate = proj[:, 3 * D + hh * dk:3 * D + (hh + 1) * dk]
        b = _chunk_cumsum(jnp.log(f), C)
        st = st_ref[hh]
        oc = []
        for c in range(tm // C):
            rs = slice(c * C, (c + 1) * C)
            bc, qc, kc, vc = b[rs], q[rs], kk[rs], v[rs]
            ref = bc[C // 2 - 1:C // 2, :]
            last = bc[C - 1:C, :]
            sc = jnp.where(causal, _dot_nt(qc * jnp.exp(bc - ref), kc * jnp.exp(ref - bc)), 0.0)
            oc.append(_dot(sc, vc) + _dot_nt(qc * jnp.exp(bc), st))
            st = st * jnp.exp(last) + _dot_tn(vc, kc * jnp.exp(last - bc))
        st_ref[hh] = st
        o = jnp.concatenate(oc, axis=0)
        o = o * lax.rsqrt(jnp.mean(o * o, axis=-1, keepdims=True) + EPS) * ng_ref[...]
        outs.append(o * _silu(gate))
    y = jnp.concatenate(outs, axis=1)
    o_ref[...] = x + m_ref[2:3, :] * _dot(y, wout_ref[...])


def _hgrn_mixer(layer, x, prev, mod, g, w_in, lb_logits, norm_g, w_out, T):
    D = x.shape[1]
    dk = HGRN_HEAD_DIM
    weights = [w_in.astype(BF16), lb_logits, norm_g.reshape(1, dk), w_out.astype(BF16)]
    return _mixer_call(functools.partial(_hgrn_kernel, layer), "mixer_hgrn", 256, T, x, prev, mod, g, weights,
                       [pltpu.VMEM((D // dk, dk, dk), F32)])


def _split2(a):
    hi = a.astype(BF16)
    lo = (a - hi.astype(F32)).astype(BF16)
    return hi, lo


def _route_kernel(tm, x_ref, m_ref, g_ref, rw_ref, rb_ref, tri_ref, h_ref, cls_ref, rank_ref, cnt_ref, carry_ref):
    i = pl.program_id(0)

    @pl.when(i == 0)
    def _():
        carry_ref[...] = jnp.zeros_like(carry_ref)

    h = _norm_mod(x_ref[...], g_ref[...], m_ref[4:5, :], m_ref[3:4, :])
    _tt_store(h_ref, h)
    h_hi, h_lo = _split2(h)
    w_hi, w_lo = _split2(rw_ref[...])
    logits = (jnp.dot(h_hi, w_hi, preferred_element_type=F32) + jnp.dot(h_lo, w_hi, preferred_element_type=F32)
              + jnp.dot(h_hi, w_lo, preferred_element_type=F32))
    lt = jnp.transpose(logits)[:N_EXPERTS, :]
    score = _sigmoid(lt)
    sel = score + rb_ref[...]
    gscore, gsel = [], []
    for gi in range(N_GROUPS):
        r = [sel[gi * 4 + e:gi * 4 + e + 1, :] for e in range(EXPERTS_PER_GROUP)]
        m1 = functools.reduce(jnp.maximum, r)
        m2 = None
        for a in range(EXPERTS_PER_GROUP):
            for b2 in range(a + 1, EXPERTS_PER_GROUP):
                pm = jnp.minimum(r[a], r[b2])
                m2 = pm if m2 is None else jnp.maximum(m2, pm)
        gscore.append(m1 + m2)
        gsel.append(r)
    best = jnp.zeros((1, tm), jnp.int32)
    bs = gscore[0]
    for gi in range(1, N_GROUPS):
        better = gscore[gi] > bs
        best = jnp.where(better, gi, best)
        bs = jnp.where(better, gscore[gi], bs)
    r = [functools.reduce(lambda a, b2: a + b2,
                          [jnp.where(best == gi, gsel[gi][e], 0.0) for gi in range(N_GROUPS)])
         for e in range(EXPERTS_PER_GROUP)]
    keep = []
    for e in range(EXPERTS_PER_GROUP):
        beaten = jnp.zeros((1, tm), jnp.int32)
        for o in range(EXPERTS_PER_GROUP):
            if o != e:
                wins = (r[o] > r[e]) | ((r[o] == r[e]) & (o < e))
                beaten = beaten + wins.astype(jnp.int32)
        keep.append(beaten < 2)
    pair = jnp.zeros((1, tm), jnp.int32)
    for pi, (a, b2) in enumerate(PAIRS):
        pair = jnp.where(keep[a] & keep[b2], pi, pair)
    cls = best * len(PAIRS) + pair
    cls_ref[...] = cls.reshape(1, 1, tm)
    onehot = (lax.broadcasted_iota(jnp.int32, (CLASS_ROWS, tm), 0) == cls).astype(F32)
    before = jnp.dot(onehot.astype(BF16), tri_ref[...], preferred_element_type=F32) + carry_ref[:, 0:1]
    rank = jnp.sum(onehot * before, axis=0, keepdims=True)
    rank_ref[...] = rank.astype(jnp.int32).reshape(1, 1, tm)
    carry_ref[...] = carry_ref[...] + jnp.sum(onehot, axis=1, keepdims=True)
    cnt_ref[...] = carry_ref[...]


def _route(x1, mod, g, router_w, router_bias, T):
    N, D = x1.shape
    tm = 512
    nt = N // tm
    E = router_w.shape[1]
    rw = jnp.zeros((D, LANES), F32).at[:, :E].set(router_w)
    rb = router_bias.reshape(E, 1)
    tri = (jnp.arange(tm)[:, None] < jnp.arange(tm)[None, :]).astype(BF16)
    row = pl.BlockSpec((tm, D), lambda i: (i, 0))
    modspec = pl.BlockSpec((None, 6, D), lambda i: ((i * tm) // T, 0, 0))
    tok = pl.BlockSpec((1, 1, tm), lambda i: (i, 0, 0))
    h2, cls, rank, cnt = pl.pallas_call(
        functools.partial(_route_kernel, tm),
        grid=(nt,),
        in_specs=[row, modspec, _full((1, D)), _full((D, LANES)), _full((E, 1)), _full((tm, tm))],
        out_specs=[pl.BlockSpec((tm * D // LANES, LANES), lambda i: (i, 0)), tok, tok, _full((CLASS_ROWS, LANES))],
        out_shape=[jax.ShapeDtypeStruct((N * D // LANES, LANES), F32),
                   jax.ShapeDtypeStruct((nt, 1, tm), jnp.int32),
                   jax.ShapeDtypeStruct((nt, 1, tm), jnp.int32),
                   jax.ShapeDtypeStruct((CLASS_ROWS, LANES), F32)],
        scratch_shapes=[pltpu.VMEM((CLASS_ROWS, LANES), F32)],
        compiler_params=_cparams(),
        name="router",
    )(x1, mod, g.reshape(1, D), rw, rb, tri)
    return h2, cls.reshape(N), rank.reshape(N), cnt[:N_CLASSES, 0]


def _tok(ref, t, rpt):
    return ref.at[pl.ds(pl.multiple_of(t * rpt, rpt), rpt), :]


def _scatter_rows_kernel(chunk, rpt, idx_ref, src_ref, init_ref, dst_ref, sem):
    del init_ref
    base = pl.program_id(0) * chunk

    def issue(r, carry):
        pltpu.make_async_copy(_tok(src_ref, r, rpt), _tok(dst_ref, idx_ref[base + r], rpt), sem).start()
        return carry

    lax.fori_loop(0, chunk, issue, 0, unroll=8)
    pltpu.make_async_copy(src_ref, dst_ref.at[pl.ds(0, chunk * rpt), :], sem).wait()


def _scatter_rows(src, idx, n_out, rpt):
    N = idx.shape[0]
    chunk = min(PERM_CHUNK, N)
    init = jnp.zeros((n_out * rpt, LANES), src.dtype)
    return pl.pallas_call(
        functools.partial(_scatter_rows_kernel, chunk, rpt),
        grid_spec=pltpu.PrefetchScalarGridSpec(
            num_scalar_prefetch=1, grid=(N // chunk,),
            in_specs=[pl.BlockSpec((chunk * rpt, LANES), lambda i, idx: (i, 0)), pl.BlockSpec(memory_space=pl.ANY)],
            out_specs=pl.BlockSpec(memory_space=pl.ANY),
            scratch_shapes=[pltpu.SemaphoreType.DMA(())]),
        out_shape=jax.ShapeDtypeStruct(init.shape, src.dtype),
        input_output_aliases={2: 0},
        compiler_params=_cparams(),
        name="scatter_rows",
    )(idx, src, init)


def _gather_rows_kernel(chunk, rpt, idx_ref, src_ref, dst_ref, sem):
    base = pl.program_id(0) * chunk

    def issue(r, carry):
        pltpu.make_async_copy(_tok(src_ref, idx_ref[base + r], rpt), _tok(dst_ref, r, rpt), sem).start()
        return carry

    lax.fori_loop(0, chunk, issue, 0, unroll=8)
    pltpu.make_async_copy(src_ref.at[pl.ds(0, chunk * rpt), :], dst_ref, sem).wait()


def _gather_rows(src, idx, rpt):
    N = idx.shape[0]
    chunk = min(PERM_CHUNK, N)
    return pl.pallas_call(
        functools.partial(_gather_rows_kernel, chunk, rpt),
        grid_spec=pltpu.PrefetchScalarGridSpec(
            num_scalar_prefetch=1, grid=(N // chunk,),
            in_specs=[pl.BlockSpec(memory_space=pl.ANY)],
            out_specs=pl.BlockSpec((chunk * rpt, LANES), lambda i, idx: (i, 0)),
            scratch_shapes=[pltpu.SemaphoreType.DMA(())]),
        out_shape=jax.ShapeDtypeStruct((N * rpt, LANES), src.dtype),
        compiler_params=_cparams(),
        name="gather_rows",
    )(idx, src)


def _ffn_kernel(ea_ref, eb_ref, nact_ref, x_ref, rwt_ref, wga_ref, wua_ref, wda_ref, wgb_ref, wub_ref, wdb_ref,
                o_ref):
    j = pl.program_id(0)

    @pl.when(j < nact_ref[0])
    def _():
        x = _tt_load(x_ref, FFN_BLOCK, wga_ref.shape[0])
        sa = _sigmoid(jnp.sum(x * rwt_ref[pl.ds(ea_ref[j], 1), :], axis=-1, keepdims=True))
        sb = _sigmoid(jnp.sum(x * rwt_ref[pl.ds(eb_ref[j], 1), :], axis=-1, keepdims=True))
        inv = 1.0 / (sa + sb)
        xb = x.astype(BF16)
        aa = _silu(_dot(xb, wga_ref[...])) * _dot(xb, wua_ref[...]) * (sa * inv)
        ab = _silu(_dot(xb, wgb_ref[...])) * _dot(xb, wub_ref[...]) * (sb * inv)
        _tt_store(o_ref, _dot(aa, wda_ref[...]) + _dot(ab, wdb_ref[...]))

    @pl.when(j >= nact_ref[0])
    def _():
        o_ref[...] = jnp.zeros_like(o_ref)


def _ffn(xs, blk_ea, blk_eb, n_active, router_wt, w_gate, w_up, w_down):
    D, F = w_gate.shape[1:]
    rpt = D // LANES
    nb = xs.shape[0] // (FFN_BLOCK * rpt)

    def wa(j, ea, eb, na):
        return (ea[j], 0, 0)

    def wb(j, ea, eb, na):
        return (eb[j], 0, 0)

    row = pl.BlockSpec((FFN_BLOCK * rpt, LANES), lambda j, ea, eb, na: (j, 0))
    return pl.pallas_call(
        _ffn_kernel,
        grid_spec=pltpu.PrefetchScalarGridSpec(
            num_scalar_prefetch=3, grid=(nb,),
            in_specs=[row, pl.BlockSpec(router_wt.shape, lambda j, ea, eb, na: (0, 0)),
                      pl.BlockSpec((None, D, F), wa), pl.BlockSpec((None, D, F), wa), pl.BlockSpec((None, F, D), wa),
                      pl.BlockSpec((None, D, F), wb), pl.BlockSpec((None, D, F), wb), pl.BlockSpec((None, F, D), wb)],
            out_specs=row),
        out_shape=jax.ShapeDtypeStruct(xs.shape, F32),
        compiler_params=_cparams(),
        name="moe_ffn",
    )(blk_ea, blk_eb, n_active, xs, router_wt, w_gate, w_up, w_down, w_gate, w_up, w_down)


def _moe(layer, x1, mod, g, router_w, router_bias, w_gate, w_up, w_down, T):
    N, D = x1.shape
    h2, cls, rank, cnt = _route(x1, mod, g, router_w, router_bias, T)
    counts = cnt.astype(jnp.int32)
    padded = (counts + FFN_BLOCK - 1) // FFN_BLOCK * FFN_BLOCK
    ends = jnp.cumsum(padded)
    starts = ends - padded
    dest = starts[cls] + rank
    nb = N // FFN_BLOCK + N_CLASSES
    n_active = (ends[-1] // FFN_BLOCK).astype(jnp.int32)
    blk = jnp.arange(nb, dtype=jnp.int32)
    blk_start = jnp.minimum(blk, n_active - 1) * FFN_BLOCK
    blk_cls = jnp.sum((ends[None, :] <= blk_start[:, None]).astype(jnp.int32), axis=1)
    blk_cls = jnp.minimum(blk_cls, N_CLASSES - 1)
    pair_a = jnp.array([p[0] for p in PAIRS], jnp.int32)
    pair_b = jnp.array([p[1] for p in PAIRS], jnp.int32)
    grp = blk_cls // len(PAIRS)
    blk_ea = grp * EXPERTS_PER_GROUP + pair_a[blk_cls % len(PAIRS)]
    blk_eb = grp * EXPERTS_PER_GROUP + pair_b[blk_cls % len(PAIRS)]
    rpt = D // LANES
    xs = _scatter_rows(h2, dest, nb * FFN_BLOCK, rpt)
    ys = _ffn(xs, blk_ea, blk_eb, n_active.reshape(1), jnp.transpose(router_w),
              _cast_bf16(w_gate, layer), _cast_bf16(w_up, layer), _cast_bf16(w_down, layer))
    return _gather_rows(ys, dest, rpt)


def _final_kernel(x_ref, y_ref, pm_ref, g_ref, o_ref):
    x = x_ref[...]
    x = x + pm_ref[5:6, :] * _tt_load(y_ref, *x.shape)
    ms = jnp.mean(x * x, axis=-1, keepdims=True)
    o_ref[...] = x * lax.rsqrt(ms + EPS) * g_ref[...]


def _final(x, y, mod, g, T):
    N, D = x.shape
    tm = min(1024, T)
    row = pl.BlockSpec((tm, D), lambda i: (i, 0))
    modspec = pl.BlockSpec((None, 6, D), lambda i: ((i * tm) // T, 0, 0))
    return pl.pallas_call(
        _final_kernel,
        grid=(N // tm,),
        in_specs=[row, pl.BlockSpec((tm * D // LANES, LANES), lambda i: (i, 0)), modspec, _full((1, D))],
        out_specs=row,
        out_shape=jax.ShapeDtypeStruct((N, D), F32),
        compiler_params=_cparams(),
        name="final_norm",
    )(x, y, mod, g.reshape(1, D))


def kernel(x, c, ada_w, ada_b, norm_g, final_norm_g, pool_w_in, pool_w_grp, pool_scale, pool_w_out, hgrn_w_in, hgrn_lb_logits, hgrn_norm_g, hgrn_w_out, swa_w_in, swa_sinks, swa_w_out, conv_w_in, conv_w, conv_w_out, router_w, router_bias, moe_w_gate, moe_w_up, moe_w_down):
    B, T, D = x.shape
    depth = ada_w.shape[0]
    n_mixers = 4
    mod = _ada(c, ada_w, ada_b).reshape(depth, B, 6, D)
    xt = x.reshape(B * T, D)
    prev = None
    for i in range(depth):
        m, j = i % n_mixers, i // n_mixers
        g1n = norm_g[i, 0]
        if m == 0:
            x1 = _pool_mixer(xt, prev, mod[i], g1n, pool_w_in[j], pool_w_grp[j], pool_scale[j], pool_w_out[j], T)
        elif m == 1:
            x1 = _hgrn_mixer(i, xt, prev, mod[i], g1n, hgrn_w_in[j], hgrn_lb_logits, hgrn_norm_g[j],
                             hgrn_w_out[j], T)
        elif m == 2:
            x1 = _swa_mixer(xt, prev, mod[i], g1n, swa_w_in[j], swa_sinks[j], swa_w_out[j], T)
        else:
            x1 = _conv_mixer(xt, prev, mod[i], g1n, conv_w_in[j], conv_w[j], conv_w_out[j], T)
        y = _moe(i, x1, mod[i], norm_g[i, 1], router_w, router_bias, moe_w_gate, moe_w_up, moe_w_down, T)
        xt = x1
        prev = (y, mod[i])
    out = _final(xt, prev[0], prev[1], final_norm_g, T)
    return out.reshape(B, T, D)
```

```python
import functools

import jax
import jax.numpy as jnp
from jax import lax
from jax.experimental import pallas as pl
from jax.experimental.pallas import tpu as pltpu

F32 = jnp.float32
BF16 = jnp.bfloat16
EPS = 1e-6
LOG2E = 1.4426950408889634

POOL_WINDOWS = (2, 4, 8, 16)
POOL_HALO = 16
HGRN_HEAD_DIM = 128
HGRN_CHUNK = 64
SWA_HEADS = 16
SWA_KV_HEADS = 4
SWA_WINDOW = 128
CONV_WIDTH = 3
CONV_HALO = 8
N_EXPERTS = 16
N_GROUPS = 4
EXPERTS_PER_GROUP = 4
PAIRS = ((0, 1), (0, 2), (0, 3), (1, 2), (1, 3), (2, 3))
N_CLASSES = N_GROUPS * len(PAIRS)
CLASS_ROWS = 32
LANES = 128
FFN_BLOCK = 256
PERM_CHUNK = 2048
VMEM_LIMIT = 52 * 1024 * 1024


def _cparams():
    return pltpu.CompilerParams(dimension_semantics=("arbitrary",), vmem_limit_bytes=VMEM_LIMIT)


def _full(shape):
    nd = len(shape)
    return pl.BlockSpec(shape, lambda i, *_: (0,) * nd)


def _norm_mod(x, g, sc, sh):
    ms = jnp.mean(x * x, axis=-1, keepdims=True)
    return x * lax.rsqrt(ms + EPS) * (g * (1.0 + sc)) + sh


def _sigmoid(x):
    return 1.0 / (1.0 + jnp.exp(-x))


def _silu(x):
    return x * _sigmoid(x)


def _dot(a, b):
    return jnp.dot(a.astype(BF16), b.astype(BF16), preferred_element_type=F32)


def _dot_nt(a, b):
    return lax.dot_general(a.astype(BF16), b.astype(BF16), (((1,), (1,)), ((), ())), preferred_element_type=F32)


def _dot_tn(a, b):
    return lax.dot_general(a.astype(BF16), b.astype(BF16), (((0,), (0,)), ((), ())), preferred_element_type=F32)


def _tt_load(ref, rows, d):
    n = d // LANES
    return jnp.concatenate([ref[pl.ds(j, rows, stride=n), :] for j in range(n)], axis=1)


def _tt_store(ref, val):
    rows, d = val.shape
    n = d // LANES
    for j in range(n):
        ref[pl.ds(j, rows, stride=n), :] = val[:, j * LANES:(j + 1) * LANES]


def _cast_kernel(w_ref, o_ref):
    o_ref[...] = w_ref[...].astype(BF16)


def _cast_bf16(w, layer):
    _, E, K, M = w.shape
    return pl.pallas_call(
        _cast_kernel,
        grid=(E,),
        in_specs=[pl.BlockSpec((None, None, K, M), lambda e: (layer, e, 0, 0))],
        out_specs=pl.BlockSpec((None, K, M), lambda e: (e, 0, 0)),
        out_shape=jax.ShapeDtypeStruct((E, K, M), BF16),
        compiler_params=_cparams(),
        name="cast_bf16",
    )(w)


def _ada_kernel(c_ref, w_ref, b_ref, o_ref):
    cond = _silu(c_ref[...])
    o_ref[...] = jnp.dot(cond, w_ref[...], preferred_element_type=F32, precision=lax.Precision.HIGHEST) + b_ref[...]


def _ada(c, ada_w, ada_b):
    L, D, D6 = ada_w.shape
    B = c.shape[0]
    bn = D6 // 4
    return pl.pallas_call(
        _ada_kernel,
        grid=(L, D6 // bn),
        in_specs=[pl.BlockSpec((B, D), lambda l, j: (0, 0)),
                  pl.BlockSpec((None, D, bn), lambda l, j: (l, 0, j)),
                  pl.BlockSpec((None, 1, bn), lambda l, j: (l, 0, j))],
        out_specs=pl.BlockSpec((None, B, bn), lambda l, j: (l, 0, j)),
        out_shape=jax.ShapeDtypeStruct((L, B, D6), F32),
        compiler_params=pltpu.CompilerParams(dimension_semantics=("arbitrary", "arbitrary"),
                                             vmem_limit_bytes=VMEM_LIMIT),
        name="ada_mod",
    )(c, ada_w, ada_b.reshape(L, 1, D6))


def _mixer_call(body, name, tm, T, x, prev, mod, g, weights, scratch, smem=()):
    N, D = x.shape
    rpt = D // LANES
    row = pl.BlockSpec((tm, D), lambda i, *_: (i, 0))
    modspec = pl.BlockSpec((None, 6, D), lambda i, *_: ((i * tm) // T, 0, 0))
    args, specs, prefetch = [x], [row], []
    scratch = list(scratch)
    if prev is not None:
        prefetch = [prev[1]]
        args += [prev[0], prev[2]]
        specs += [pl.BlockSpec(memory_space=pl.ANY), modspec]
        scratch += [pltpu.VMEM((2, tm * rpt, LANES), F32), pltpu.SemaphoreType.DMA((2,))]
    args += [mod, g.reshape(1, D)]
    specs += [modspec, _full((1, D))]
    for w in weights:
        args.append(w)
        specs.append(_full(w.shape))
    for s in smem:
        args.append(s)
        specs.append(pl.BlockSpec(memory_space=pltpu.SMEM))
    return pl.pallas_call(
        functools.partial(body, prev is not None, tm, T),
        grid_spec=pltpu.PrefetchScalarGridSpec(
            num_scalar_prefetch=len(prefetch), grid=(N // tm,), in_specs=specs, out_specs=row,
            scratch_shapes=scratch),
        out_shape=jax.ShapeDtypeStruct((N, D), F32),
        compiler_params=_cparams(),
        name=name,
    )(*prefetch, *args)


def _gathered_rows(dest_ref, ys_ref, buf, sem, tm, d):
    i = pl.program_id(0)
    rpt = d // LANES

    def request(tile, slot):
        def issue(r, carry):
            src = _tok(ys_ref, dest_ref[tile * tm + r], rpt)
            pltpu.make_async_copy(src, _tok(buf.at[slot], r, rpt), sem.at[slot]).start()
            return carry
        lax.fori_loop(0, tm, issue, 0, unroll=8)

    @pl.when(i == 0)
    def _():
        request(0, 0)

    @pl.when(i + 1 < pl.num_programs(0))
    def _():
        request(i + 1, (i + 1) % 2)

    slot = i % 2
    pltpu.make_async_copy(ys_ref.at[pl.ds(0, tm * rpt), :], buf.at[slot], sem.at[slot]).wait()
    return _tt_load(buf.at[slot], tm, d)


def _mixer_input(has_prev, tm, refs):
    if has_prev:
        dest_ref, x_ref, ys_ref, pm_ref, m_ref, g_ref = refs[:6]
        buf, sem = refs[-2:]
        x = x_ref[...]
        x = x + pm_ref[5:6, :] * _gathered_rows(dest_ref, ys_ref, buf, sem, tm, x.shape[1])
        rest = refs[6:-2]
    else:
        x_ref, m_ref, g_ref = refs[:3]
        x = x_ref[...]
        rest = refs[3:]
    return x, m_ref, g_ref, rest


def _pool_kernel(has_prev, tm, T, *refs):
    x, m_ref, g_ref, (win_ref, wgrp_ref, scale_ref, wout_ref, o_ref, tail_ref) = _mixer_input(has_prev, tm, refs)
    i = pl.program_id(0)
    start = (i * tm) % T
    h = _norm_mod(x, g_ref[...], m_ref[1:2, :], m_ref[0:1, :])
    u = _dot(h, win_ref[...])

    @pl.when(start == 0)
    def _():
        tail_ref[...] = jnp.zeros_like(tail_ref)

    pos = start + lax.broadcasted_iota(jnp.int32, (tm, 1), 0)
    C = u.shape[1] // len(POOL_WINDOWS)
    ys = []
    for gi, w in enumerate(POOL_WINDOWS):
        ug = u[:, gi * C:(gi + 1) * C]
        s = jnp.concatenate([tail_ref[:, gi * C:(gi + 1) * C], ug], axis=0)
        k = 1
        while k < w:
            s = s + pltpu.roll(s, k, 0)
            k *= 2
        cnt = jnp.minimum(pos + 1, w).astype(F32)
        pooled = s[POOL_HALO:] / cnt - ug
        ys.append(_dot(pooled, wgrp_ref[gi]))
    tail_ref[...] = u[tm - POOL_HALO:, :]
    y = jnp.concatenate(ys, axis=1) * scale_ref[...]
    o_ref[...] = x + m_ref[2:3, :] * _dot(y, wout_ref[...])


def _pool_mixer(x, prev, mod, g, w_in, w_grp, scale, w_out, T):
    D = x.shape[1]
    weights = [w_in.astype(BF16), w_grp.astype(BF16), scale.reshape(1, D), w_out.astype(BF16)]
    return _mixer_call(_pool_kernel, "mixer_pool", 512, T, x, prev, mod, g, weights,
                       [pltpu.VMEM((POOL_HALO, D), F32)])


def _conv_kernel(has_prev, tm, T, *refs):
    x, m_ref, g_ref, (win_ref, cw_ref, wout_ref, o_ref, tail_ref) = _mixer_input(has_prev, tm, refs)
    i = pl.program_id(0)
    D = x.shape[1]
    h = _norm_mod(x, g_ref[...], m_ref[1:2, :], m_ref[0:1, :])
    bcu = _dot(h, win_ref[...])
    z = bcu[:, D:2 * D] * bcu[:, 2 * D:]

    @pl.when((i * tm) % T == 0)
    def _():
        tail_ref[...] = jnp.zeros_like(tail_ref)

    ze = jnp.concatenate([tail_ref[...], z], axis=0)
    zc = cw_ref[CONV_WIDTH - 1:CONV_WIDTH, :] * ze
    for j in range(1, CONV_WIDTH):
        zc = zc + cw_ref[CONV_WIDTH - 1 - j:CONV_WIDTH - j, :] * pltpu.roll(ze, j, 0)
    tail_ref[...] = z[tm - CONV_HALO:, :]
    y = bcu[:, :D] * zc[CONV_HALO:]
    o_ref[...] = x + m_ref[2:3, :] * _dot(y, wout_ref[...])


def _conv_mixer(x, prev, mod, g, w_in, conv_w, w_out, T):
    D = x.shape[1]
    weights = [w_in.astype(BF16), conv_w, w_out.astype(BF16)]
    return _mixer_call(_conv_kernel, "mixer_conv", 512, T, x, prev, mod, g, weights,
                       [pltpu.VMEM((CONV_HALO, D), F32)])


def _swa_kernel(has_prev, tm, T, *refs):
    x, m_ref, g_ref, (win_ref, wout_ref, sink_ref, o_ref, q_scr, klo, khi, vlo, vhi, o_scr) = \
        _mixer_input(has_prev, tm, refs)
    i = pl.program_id(0)
    D = x.shape[1]
    W = SWA_WINDOW
    hd = D // SWA_HEADS
    G = SWA_HEADS // SWA_KV_HEADS
    kvd = SWA_KV_HEADS * hd
    assert 2 * hd == LANES
    seq_start = (i * tm) % T == 0
    kv_scr = (klo, khi, vlo, vhi)

    @pl.when(seq_start)
    def _():
        for r in kv_scr:
            r[0:W, :] = jnp.zeros((W, r.shape[1]), BF16)

    @pl.when(jnp.logical_not(seq_start))
    def _():
        for r in kv_scr:
            r[0:W, :] = r[tm:tm + W, :]

    h = _norm_mod(x, g_ref[...], m_ref[1:2, :], m_ref[0:1, :])
    qkv = _dot(h, win_ref[...])
    q_scr[...] = (qkv[:, :D] * (hd ** -0.5 * LOG2E)).astype(BF16)
    low = lax.broadcasted_iota(jnp.int32, (tm, LANES), 1) < hd
    for src, lo_ref, hi_ref in ((qkv[:, D:D + kvd], klo, khi), (qkv[:, D + kvd:], vlo, vhi)):
        for a in range(kvd // LANES):
            kg = src[:, a * LANES:(a + 1) * LANES]
            sw = pltpu.roll(kg, hd, 1)
            for par, (lo_v, hi_v) in enumerate(((kg, sw), (sw, kg))):
                c = (2 * a + par) * LANES
                lo_ref[W:, c:c + LANES] = jnp.where(low, lo_v, 0.0).astype(BF16)
                hi_ref[W:, c:c + LANES] = jnp.where(low, 0.0, hi_v).astype(BF16)
    qpos = lax.broadcasted_iota(jnp.int32, (W, 2 * W), 0)
    kj = lax.broadcasted_iota(jnp.int32, (W, 2 * W), 1)
    band = (kj > qpos) & (kj <= qpos + W)
    bias = jnp.where(band, 0.0, -jnp.inf)
    bias0 = jnp.where(band & (kj >= jnp.where(seq_start, W, 0)), 0.0, -jnp.inf)
    for j in range(tm // W):
        b = bias0 if j == 0 else bias
        for grp in range(SWA_HEADS // 2):
            hk = (2 * grp) // G
            qg = q_scr[j * W:(j + 1) * W, grp * LANES:(grp + 1) * LANES]
            pair = None
            for par, (k_ref, v_ref) in enumerate(((klo, vlo), (khi, vhi))):
                kk = k_ref[j * W:(j + 2) * W, hk * LANES:(hk + 1) * LANES]
                vv = v_ref[j * W:(j + 2) * W, hk * LANES:(hk + 1) * LANES]
                sink = sink_ref[2 * grp + par] * LOG2E
                s = lax.dot_general(qg, kk, (((1,), (1,)), ((), ())), preferred_element_type=F32) + b
                mx = jnp.maximum(jnp.max(s, axis=-1, keepdims=True), sink)
                p = jnp.exp2(s - mx)
                denom = jnp.sum(p, axis=-1, keepdims=True) + jnp.exp2(sink - mx)
                o = jnp.dot(p.astype(BF16), vv, preferred_element_type=F32) * (1.0 / denom)
                pair = o if pair is None else pair + o
            o_scr[j * W:(j + 1) * W, grp * LANES:(grp + 1) * LANES] = pair.astype(BF16)
    o_ref[...] = x + m_ref[2:3, :] * jnp.dot(o_scr[...], wout_ref[...], preferred_element_type=F32)


def _swa_mixer(x, prev, mod, g, w_in, sinks, w_out, T):
    D = x.shape[1]
    tm = 256
    rows = SWA_WINDOW + tm
    kv = pltpu.VMEM((rows, SWA_KV_HEADS * LANES), BF16)
    weights = [w_in.astype(BF16), w_out.astype(BF16)]
    return _mixer_call(_swa_kernel, "mixer_swa", tm, T, x, prev, mod, g, weights,
                       [pltpu.VMEM((tm, D), BF16), kv, kv, kv, kv, pltpu.VMEM((tm, D), BF16)], smem=[sinks])


def _chunk_cumsum(x, C):
    ridx = lax.broadcasted_iota(jnp.int32, (x.shape[0], 1), 0) % C
    k = 1
    while k < C:
        x = x + jnp.where(ridx >= k, pltpu.roll(x, k, 0), 0.0)
        k *= 2
    return x


def _hgrn_kernel(layer, has_prev, tm, T, *refs):
    x, m_ref, g_ref, (win_ref, lbl_ref, ng_ref, wout_ref, o_ref, st_ref) = _mixer_input(has_prev, tm, refs)
    i = pl.program_id(0)
    D = x.shape[1]
    dk = HGRN_HEAD_DIM
    C = HGRN_CHUNK

    @pl.when((i * tm) % T == 0)
    def _():
        st_ref[...] = jnp.zeros_like(st_ref)

    rows = [lbl_ref[j:j + 1, :] for j in range(lbl_ref.shape[0])]
    mx = functools.reduce(jnp.maximum, rows)
    es = [jnp.exp(r - mx) for r in rows]
    tot = functools.reduce(lambda a, b: a + b, es)
    lb = jnp.zeros_like(mx)
    for j in range(1, layer + 1):
        lb = lb + es[j] / tot

    h = _norm_mod(x, g_ref[...], m_ref[1:2, :], m_ref[0:1, :])
    proj = _dot(h, win_ref[...])
    causal = lax.broadcasted_iota(jnp.int32, (C, C), 0) >= lax.broadcasted_iota(jnp.int32, (C, C), 1)
    outs = []
    for hh in range(D // dk):
        sl = slice(hh * dk, (hh + 1) * dk)
        q = _silu(proj[:, sl])
        lbh = lb[:, sl]
        f = lbh + (1.0 - lbh) * _sigmoid(proj[:, D + hh * dk:D + (hh + 1) * dk])
        kk = 1.0 - f
        v = proj[:, 2 * D + hh * dk:2 * D + (hh + 1) * dk]
        gate = proj[:, 3 * D + hh * dk:3 * D + (hh + 1) * dk]
        b = _chunk_cumsum(jnp.log(f), C)
        st = st_ref[hh]
        oc = []
        for c in range(tm // C):
            rs = slice(c * C, (c + 1) * C)
            bc, qc, kc, vc = b[rs], q[rs], kk[rs], v[rs]
            ref = bc[C // 2 - 1:C // 2, :]
            last = bc[C - 1:C, :]
            sc = jnp.where(causal, _dot_nt(qc * jnp.exp(bc - ref), kc * jnp.exp(ref - bc)), 0.0)
            oc.append(_dot(sc, vc) + _dot_nt(qc * jnp.exp(bc), st))
            st = st * jnp.exp(last) + _dot_tn(vc, kc * jnp.exp(last - bc))
        st_ref[hh] = st
        o = jnp.concatenate(oc, axis=0)
        o = o * lax.rsqrt(jnp.mean(o * o, axis=-1, keepdims=True) + EPS) * ng_ref[...]
        outs.append(o * _silu(gate))
    y = jnp.concatenate(outs, axis=1)
    o_ref[...] = x + m_ref[2:3, :] * _dot(y, wout_ref[...])


def _hgrn_mixer(layer, x, prev, mod, g, w_in, lb_logits, norm_g, w_out, T):
    D = x.shape[1]
    dk = HGRN_HEAD_DIM
    weights = [w_in.astype(BF16), lb_logits, norm_g.reshape(1, dk), w_out.astype(BF16)]
    return _mixer_call(functools.partial(_hgrn_kernel, layer), "mixer_hgrn", 256, T, x, prev, mod, g, weights,
                       [pltpu.VMEM((D // dk, dk, dk), F32)])


def _split2(a):
    hi = a.astype(BF16)
    lo = (a - hi.astype(F32)).astype(BF16)
    return hi, lo


def _route_kernel(tm, x_ref, m_ref, g_ref, rw_ref, rb_ref, tri_ref, h_ref, cls_ref, rank_ref, cnt_ref, carry_ref):
    i = pl.program_id(0)

    @pl.when(i == 0)
    def _():
        carry_ref[...] = jnp.zeros_like(carry_ref)

    h = _norm_mod(x_ref[...], g_ref[...], m_ref[4:5, :], m_ref[3:4, :])
    _tt_store(h_ref, h)
    h_hi, h_lo = _split2(h)
    w_hi, w_lo = _split2(rw_ref[...])
    logits = (jnp.dot(h_hi, w_hi, preferred_element_type=F32) + jnp.dot(h_lo, w_hi, preferred_element_type=F32)
              + jnp.dot(h_hi, w_lo, preferred_element_type=F32))
    lt = jnp.transpose(logits)[:N_EXPERTS, :]
    score = _sigmoid(lt)
    sel = score + rb_ref[...]
    gscore, gsel = [], []
    for gi in range(N_GROUPS):
        r = [sel[gi * 4 + e:gi * 4 + e + 1, :] for e in range(EXPERTS_PER_GROUP)]
        m1 = functools.reduce(jnp.maximum, r)
        m2 = None
        for a in range(EXPERTS_PER_GROUP):
            for b2 in range(a + 1, EXPERTS_PER_GROUP):
                pm = jnp.minimum(r[a], r[b2])
                m2 = pm if m2 is None else jnp.maximum(m2, pm)
        gscore.append(m1 + m2)
        gsel.append(r)
    best = jnp.zeros((1, tm), jnp.int32)
    bs = gscore[0]
    for gi in range(1, N_GROUPS):
        better = gscore[gi] > bs
        best = jnp.where(better, gi, best)
        bs = jnp.where(better, gscore[gi], bs)
    r = [functools.reduce(lambda a, b2: a + b2,
                          [jnp.where(best == gi, gsel[gi][e], 0.0) for gi in range(N_GROUPS)])
         for e in range(EXPERTS_PER_GROUP)]
    keep = []
    for e in range(EXPERTS_PER_GROUP):
        beaten = jnp.zeros((1, tm), jnp.int32)
        for o in range(EXPERTS_PER_GROUP):
            if o != e:
                wins = (r[o] > r[e]) | ((r[o] == r[e]) & (o < e))
                beaten = beaten + wins.astype(jnp.int32)
        keep.append(beaten < 2)
    pair = jnp.zeros((1, tm), jnp.int32)
    for pi, (a, b2) in enumerate(PAIRS):
        pair = jnp.where(keep[a] & keep[b2], pi, pair)
    cls = best * len(PAIRS) + pair
    cls_ref[...] = cls.reshape(1, 1, tm)
    onehot = (lax.broadcasted_iota(jnp.int32, (CLASS_ROWS, tm), 0) == cls).astype(F32)
    before = jnp.dot(onehot.astype(BF16), tri_ref[...], preferred_element_type=F32) + carry_ref[:, 0:1]
    rank = jnp.sum(onehot * before, axis=0, keepdims=True)
    rank_ref[...] = rank.astype(jnp.int32).reshape(1, 1, tm)
    carry_ref[...] = carry_ref[...] + jnp.sum(onehot, axis=1, keepdims=True)
    cnt_ref[...] = carry_ref[...]


def _route(x1, mod, g, router_w, router_bias, T):
    N, D = x1.shape
    tm = 512
    nt = N // tm
    E = router_w.shape[1]
    rw = jnp.zeros((D, LANES), F32).at[:, :E].set(router_w)
    rb = router_bias.reshape(E, 1)
    tri = (jnp.arange(tm)[:, None] < jnp.arange(tm)[None, :]).astype(BF16)
    row = pl.BlockSpec((tm, D), lambda i: (i, 0))
    modspec = pl.BlockSpec((None, 6, D), lambda i: ((i * tm) // T, 0, 0))
    tok = pl.BlockSpec((1, 1, tm), lambda i: (i, 0, 0))
    h2, cls, rank, cnt = pl.pallas_call(
        functools.partial(_route_kernel, tm),
        grid=(nt,),
        in_specs=[row, modspec, _full((1, D)), _full((D, LANES)), _full((E, 1)), _full((tm, tm))],
        out_specs=[pl.BlockSpec((tm * D // LANES, LANES), lambda i: (i, 0)), tok, tok, _full((CLASS_ROWS, LANES))],
        out_shape=[jax.ShapeDtypeStruct((N * D // LANES, LANES), F32),
                   jax.ShapeDtypeStruct((nt, 1, tm), jnp.int32),
                   jax.ShapeDtypeStruct((nt, 1, tm), jnp.int32),
                   jax.ShapeDtypeStruct((CLASS_ROWS, LANES), F32)],
        scratch_shapes=[pltpu.VMEM((CLASS_ROWS, LANES), F32)],
        compiler_params=_cparams(),
        name="router",
    )(x1, mod, g.reshape(1, D), rw, rb, tri)
    return h2, cls.reshape(N), rank.reshape(N), cnt[:N_CLASSES, 0]


def _tok(ref, t, rpt):
    return ref.at[pl.ds(pl.multiple_of(t * rpt, rpt), rpt), :]


def _scatter_rows_kernel(chunk, rpt, idx_ref, src_ref, init_ref, dst_ref, sem):
    del init_ref
    base = pl.program_id(0) * chunk

    def issue(r, carry):
        pltpu.make_async_copy(_tok(src_ref, r, rpt), _tok(dst_ref, idx_ref[base + r], rpt), sem).start()
        return carry

    lax.fori_loop(0, chunk, issue, 0, unroll=8)
    pltpu.make_async_copy(src_ref, dst_ref.at[pl.ds(0, chunk * rpt), :], sem).wait()


def _scatter_rows(src, idx, n_out, rpt):
    N = idx.shape[0]
    chunk = min(PERM_CHUNK, N)
    init = jnp.zeros((n_out * rpt, LANES), src.dtype)
    return pl.pallas_call(
        functools.partial(_scatter_rows_kernel, chunk, rpt),
        grid_spec=pltpu.PrefetchScalarGridSpec(
            num_scalar_prefetch=1, grid=(N // chunk,),
            in_specs=[pl.BlockSpec((chunk * rpt, LANES), lambda i, idx: (i, 0)), pl.BlockSpec(memory_space=pl.ANY)],
            out_specs=pl.BlockSpec(memory_space=pl.ANY),
            scratch_shapes=[pltpu.SemaphoreType.DMA(())]),
        out_shape=jax.ShapeDtypeStruct(init.shape, src.dtype),
        input_output_aliases={2: 0},
        compiler_params=_cparams(),
        name="scatter_rows",
    )(idx, src, init)


def _ffn_kernel(ea_ref, eb_ref, nact_ref, x_ref, rwt_ref, wga_ref, wua_ref, wda_ref, wgb_ref, wub_ref, wdb_ref,
                o_ref):
    j = pl.program_id(0)

    @pl.when(j < nact_ref[0])
    def _():
        x = _tt_load(x_ref, FFN_BLOCK, wga_ref.shape[0])
        sa = _sigmoid(jnp.sum(x * rwt_ref[pl.ds(ea_ref[j], 1), :], axis=-1, keepdims=True))
        sb = _sigmoid(jnp.sum(x * rwt_ref[pl.ds(eb_ref[j], 1), :], axis=-1, keepdims=True))
        inv = 1.0 / (sa + sb)
        xb = x.astype(BF16)
        aa = _silu(_dot(xb, wga_ref[...])) * _dot(xb, wua_ref[...]) * (sa * inv)
        ab = _silu(_dot(xb, wgb_ref[...])) * _dot(xb, wub_ref[...]) * (sb * inv)
        _tt_store(o_ref, _dot(aa, wda_ref[...]) + _dot(ab, wdb_ref[...]))

    @pl.when(j >= nact_ref[0])
    def _():
        o_ref[...] = jnp.zeros_like(o_ref)


def _ffn(xs, blk_ea, blk_eb, n_active, router_wt, w_gate, w_up, w_down):
    D, F = w_gate.shape[1:]
    rpt = D // LANES
    nb = xs.shape[0] // (FFN_BLOCK * rpt)

    def wa(j, ea, eb, na):
        return (ea[j], 0, 0)

    def wb(j, ea, eb, na):
        return (eb[j], 0, 0)

    row = pl.BlockSpec((FFN_BLOCK * rpt, LANES), lambda j, ea, eb, na: (j, 0))
    return pl.pallas_call(
        _ffn_kernel,
        grid_spec=pltpu.PrefetchScalarGridSpec(
            num_scalar_prefetch=3, grid=(nb,),
            in_specs=[row, pl.BlockSpec(router_wt.shape, lambda j, ea, eb, na: (0, 0)),
                      pl.BlockSpec((None, D, F), wa), pl.BlockSpec((None, D, F), wa), pl.BlockSpec((None, F, D), wa),
                      pl.BlockSpec((None, D, F), wb), pl.BlockSpec((None, D, F), wb), pl.BlockSpec((None, F, D), wb)],
            out_specs=row),
        out_shape=jax.ShapeDtypeStruct(xs.shape, F32),
        compiler_params=_cparams(),
        name="moe_ffn",
    )(blk_ea, blk_eb, n_active, xs, router_wt, w_gate, w_up, w_down, w_gate, w_up, w_down)


def _moe(layer, x1, mod, g, router_w, router_bias, w_gate, w_up, w_down, T):
    N, D = x1.shape
    h2, cls, rank, cnt = _route(x1, mod, g, router_w, router_bias, T)
    counts = cnt.astype(jnp.int32)
    padded = (counts + FFN_BLOCK - 1) // FFN_BLOCK * FFN_BLOCK
    ends = jnp.cumsum(padded)
    starts = ends - padded
    dest = starts[cls] + rank
    nb = N // FFN_BLOCK + N_CLASSES
    n_active = (ends[-1] // FFN_BLOCK).astype(jnp.int32)
    blk = jnp.arange(nb, dtype=jnp.int32)
    blk_start = jnp.minimum(blk, n_active - 1) * FFN_BLOCK
    blk_cls = jnp.sum((ends[None, :] <= blk_start[:, None]).astype(jnp.int32), axis=1)
    blk_cls = jnp.minimum(blk_cls, N_CLASSES - 1)
    pair_a = jnp.array([p[0] for p in PAIRS], jnp.int32)
    pair_b = jnp.array([p[1] for p in PAIRS], jnp.int32)
    grp = blk_cls // len(PAIRS)
    blk_ea = grp * EXPERTS_PER_GROUP + pair_a[blk_cls % len(PAIRS)]
    blk_eb = grp * EXPERTS_PER_GROUP + pair_b[blk_cls % len(PAIRS)]
    rpt = D // LANES
    xs = _scatter_rows(h2, dest, nb * FFN_BLOCK, rpt)
    ys = _ffn(xs, blk_ea, blk_eb, n_active.reshape(1), jnp.transpose(router_w),
              _cast_bf16(w_gate, layer), _cast_bf16(w_up, layer), _cast_bf16(w_down, layer))
    return ys, dest


def _final_kernel(tm, dest_ref, x_ref, ys_ref, pm_ref, g_ref, o_ref, buf, sem):
    x = x_ref[...]
    x = x + pm_ref[5:6, :] * _gathered_rows(dest_ref, ys_ref, buf, sem, tm, x.shape[1])
    ms = jnp.mean(x * x, axis=-1, keepdims=True)
    o_ref[...] = x * lax.rsqrt(ms + EPS) * g_ref[...]


def _final(x, prev, g, T):
    ys, dest, mod = prev
    N, D = x.shape
    tm = min(512, T)
    row = pl.BlockSpec((tm, D), lambda i, *_: (i, 0))
    modspec = pl.BlockSpec((None, 6, D), lambda i, *_: ((i * tm) // T, 0, 0))
    return pl.pallas_call(
        functools.partial(_final_kernel, tm),
        grid_spec=pltpu.PrefetchScalarGridSpec(
            num_scalar_prefetch=1, grid=(N // tm,),
            in_specs=[row, pl.BlockSpec(memory_space=pl.ANY), modspec, _full((1, D))],
            out_specs=row,
            scratch_shapes=[pltpu.VMEM((2, tm * D // LANES, LANES), F32), pltpu.SemaphoreType.DMA((2,))]),
        out_shape=jax.ShapeDtypeStruct((N, D), F32),
        compiler_params=_cparams(),
        name="final_norm",
    )(dest, x, ys, mod, g.reshape(1, D))


def kernel(x, c, ada_w, ada_b, norm_g, final_norm_g, pool_w_in, pool_w_grp, pool_scale, pool_w_out, hgrn_w_in, hgrn_lb_logits, hgrn_norm_g, hgrn_w_out, swa_w_in, swa_sinks, swa_w_out, conv_w_in, conv_w, conv_w_out, router_w, router_bias, moe_w_gate, moe_w_up, moe_w_down):
    B, T, D = x.shape
    depth = ada_w.shape[0]
    n_mixers = 4
    mod = _ada(c, ada_w, ada_b).reshape(depth, B, 6, D)
    xt = x.reshape(B * T, D)
    prev = None
    for i in range(depth):
        m, j = i % n_mixers, i // n_mixers
        g1n = norm_g[i, 0]
        if m == 0:
            x1 = _pool_mixer(xt, prev, mod[i], g1n, pool_w_in[j], pool_w_grp[j], pool_scale[j], pool_w_out[j], T)
        elif m == 1:
            x1 = _hgrn_mixer(i, xt, prev, mod[i], g1n, hgrn_w_in[j], hgrn_lb_logits, hgrn_norm_g[j],
                             hgrn_w_out[j], T)
        elif m == 2:
            x1 = _swa_mixer(xt, prev, mod[i], g1n, swa_w_in[j], swa_sinks[j], swa_w_out[j], T)
        else:
            x1 = _conv_mixer(xt, prev, mod[i], g1n, conv_w_in[j], conv_w[j], conv_w_out[j], T)
        ys, dest = _moe(i, x1, mod[i], norm_g[i, 1], router_w, router_bias, moe_w_gate, moe_w_up, moe_w_down, T)
        xt = x1
        prev = (ys, dest, mod[i])
    out = _final(xt, prev, final_norm_g, T)
    return out.reshape(B, T, D)
```

```python
import functools

import jax
import jax.numpy as jnp
import numpy as np
from jax import lax
from jax.experimental import pallas as pl
from jax.experimental.pallas import tpu as pltpu

F32 = jnp.float32
BF16 = jnp.bfloat16
EPS = 1e-6
LOG2E = 1.4426950408889634

POOL_WINDOWS = (2, 4, 8, 16)
POOL_HALO = 16
HGRN_HEAD_DIM = 128
HGRN_CHUNK = 64
SWA_HEADS = 16
SWA_KV_HEADS = 4
SWA_WINDOW = 128
CONV_WIDTH = 3
CONV_HALO = 8
N_EXPERTS = 16
N_GROUPS = 4
EXPERTS_PER_GROUP = 4
PAIRS = ((0, 1), (0, 2), (0, 3), (1, 2), (1, 3), (2, 3))
N_CLASSES = N_GROUPS * len(PAIRS)
CLASS_ROWS = 32
LANES = 128
FFN_BLOCK = 256
PERM_CHUNK = 2048
VMEM_LIMIT = 52 * 1024 * 1024


def _cparams():
    return pltpu.CompilerParams(dimension_semantics=("arbitrary",), vmem_limit_bytes=VMEM_LIMIT)


def _full(shape):
    nd = len(shape)
    return pl.BlockSpec(shape, lambda i, *_: (0,) * nd)


def _norm_mod(x, g, sc, sh):
    ms = jnp.mean(x * x, axis=-1, keepdims=True)
    return x * lax.rsqrt(ms + EPS) * (g * (1.0 + sc)) + sh


def _sigmoid(x):
    return 1.0 / (1.0 + jnp.exp(-x))


def _silu(x):
    return x * _sigmoid(x)


def _dot(a, b):
    return jnp.dot(a.astype(BF16), b.astype(BF16), preferred_element_type=F32)


def _dot_nt(a, b):
    return lax.dot_general(a.astype(BF16), b.astype(BF16), (((1,), (1,)), ((), ())), preferred_element_type=F32)


def _dot_tn(a, b):
    return lax.dot_general(a.astype(BF16), b.astype(BF16), (((0,), (0,)), ((), ())), preferred_element_type=F32)


def _tt_load(ref, rows, d):
    n = d // LANES
    return jnp.concatenate([ref[pl.ds(j, rows, stride=n), :] for j in range(n)], axis=1)


def _tt_store(ref, val):
    rows, d = val.shape
    n = d // LANES
    for j in range(n):
        ref[pl.ds(j, rows, stride=n), :] = val[:, j * LANES:(j + 1) * LANES]


def _cast_kernel(w_ref, o_ref):
    o_ref[...] = w_ref[...].astype(BF16)


def _cast_bf16(w, layer):
    _, E, K, M = w.shape
    return pl.pallas_call(
        _cast_kernel,
        grid=(E,),
        in_specs=[pl.BlockSpec((None, None, K, M), lambda e: (layer, e, 0, 0))],
        out_specs=pl.BlockSpec((None, K, M), lambda e: (e, 0, 0)),
        out_shape=jax.ShapeDtypeStruct((E, K, M), BF16),
        compiler_params=_cparams(),
        name="cast_bf16",
    )(w)


def _ada_kernel(c_ref, w_ref, b_ref, o_ref):
    cond = _silu(c_ref[...])
    o_ref[...] = jnp.dot(cond, w_ref[...], preferred_element_type=F32, precision=lax.Precision.HIGHEST) + b_ref[...]


def _ada(c, ada_w, ada_b):
    L, D, D6 = ada_w.shape
    B = c.shape[0]
    bn = D6 // 4
    return pl.pallas_call(
        _ada_kernel,
        grid=(L, D6 // bn),
        in_specs=[pl.BlockSpec((B, D), lambda l, j: (0, 0)),
                  pl.BlockSpec((None, D, bn), lambda l, j: (l, 0, j)),
                  pl.BlockSpec((None, 1, bn), lambda l, j: (l, 0, j))],
        out_specs=pl.BlockSpec((None, B, bn), lambda l, j: (l, 0, j)),
        out_shape=jax.ShapeDtypeStruct((L, B, D6), F32),
        compiler_params=pltpu.CompilerParams(dimension_semantics=("arbitrary", "arbitrary"),
                                             vmem_limit_bytes=VMEM_LIMIT),
        name="ada_mod",
    )(c, ada_w, ada_b.reshape(L, 1, D6))


def _mixer_call(body, name, tm, T, x, prev, mod, g, weights, scratch, smem=()):
    N, D = x.shape
    rpt = D // LANES
    row = pl.BlockSpec((tm, D), lambda i, *_: (i, 0))
    modspec = pl.BlockSpec((None, 6, D), lambda i, *_: ((i * tm) // T, 0, 0))
    args, specs, prefetch = [x], [row], []
    scratch = list(scratch)
    if prev is not None:
        prefetch = [prev[1]]
        args += [prev[0], prev[2]]
        specs += [pl.BlockSpec(memory_space=pl.ANY), modspec]
        scratch += [pltpu.VMEM((2, tm * rpt, LANES), F32), pltpu.SemaphoreType.DMA((2,))]
    args += [mod, g.reshape(1, D)]
    specs += [modspec, _full((1, D))]
    for w in weights:
        args.append(w)
        specs.append(_full(w.shape))
    for s in smem:
        args.append(s)
        specs.append(pl.BlockSpec(memory_space=pltpu.SMEM))
    return pl.pallas_call(
        functools.partial(body, prev is not None, tm, T),
        grid_spec=pltpu.PrefetchScalarGridSpec(
            num_scalar_prefetch=len(prefetch), grid=(N // tm,), in_specs=specs, out_specs=row,
            scratch_shapes=scratch),
        out_shape=jax.ShapeDtypeStruct((N, D), F32),
        compiler_params=_cparams(),
        name=name,
    )(*prefetch, *args)


def _gathered_rows(dest_ref, ys_ref, buf, sem, tm, d):
    i = pl.program_id(0)
    rpt = d // LANES

    def request(tile, slot, lo, hi):
        def issue(r2, carry):
            for par in range(2):
                r = 2 * r2 + par
                src = _tok(ys_ref, dest_ref[tile * tm + r], rpt)
                pltpu.make_async_copy(src, _tok(buf.at[slot], r, rpt), sem.at[slot]).start(priority=par)
            return carry
        lax.fori_loop(lo // 2, hi // 2, issue, 0, unroll=4)

    @pl.when(i == 0)
    def _():
        request(0, 0, 0, tm)

    def prefetch(k, n):
        @pl.when(i + 1 < pl.num_programs(0))
        def _():
            request(i + 1, (i + 1) % 2, k * (tm // n), (k + 1) * (tm // n))

    slot = i % 2
    pltpu.make_async_copy(ys_ref.at[pl.ds(0, tm * rpt), :], buf.at[slot], sem.at[slot]).wait()
    return _tt_load(buf.at[slot], tm, d), prefetch


def _mixer_input(has_prev, tm, refs):
    if has_prev:
        dest_ref, x_ref, ys_ref, pm_ref, m_ref, g_ref = refs[:6]
        buf, sem = refs[-2:]
        x = x_ref[...]
        y, prefetch = _gathered_rows(dest_ref, ys_ref, buf, sem, tm, x.shape[1])
        x = x + pm_ref[5:6, :] * y
        rest = refs[6:-2]
    else:
        x_ref, m_ref, g_ref = refs[:3]
        x = x_ref[...]
        rest = refs[3:]
        prefetch = lambda k, n: None
    return x, m_ref, g_ref, rest, prefetch


def _pool_kernel(has_prev, tm, T, *refs):
    x, m_ref, g_ref, (win_ref, wgrp_ref, scale_ref, wout_ref, o_ref, tail_ref), prefetch = \
        _mixer_input(has_prev, tm, refs)
    i = pl.program_id(0)
    start = (i * tm) % T
    n_pre = 2 * len(POOL_WINDOWS)
    h = _norm_mod(x, g_ref[...], m_ref[1:2, :], m_ref[0:1, :])
    u = _dot(h, win_ref[...])

    @pl.when(start == 0)
    def _():
        tail_ref[...] = jnp.zeros_like(tail_ref)

    pos = start + lax.broadcasted_iota(jnp.int32, (tm, 1), 0)
    C = u.shape[1] // len(POOL_WINDOWS)
    ys = []
    for gi, w in enumerate(POOL_WINDOWS):
        prefetch(2 * gi, n_pre)
        ug = u[:, gi * C:(gi + 1) * C]
        s = jnp.concatenate([tail_ref[:, gi * C:(gi + 1) * C], ug], axis=0)
        k = 1
        while k < w:
            s = s + pltpu.roll(s, k, 0)
            k *= 2
        cnt = jnp.minimum(pos + 1, w).astype(F32)
        pooled = s[POOL_HALO:] / cnt - ug
        prefetch(2 * gi + 1, n_pre)
        ys.append(_dot(pooled, wgrp_ref[gi]))
    tail_ref[...] = u[tm - POOL_HALO:, :]
    y = jnp.concatenate(ys, axis=1) * scale_ref[...]
    o_ref[...] = x + m_ref[2:3, :] * _dot(y, wout_ref[...])


def _pool_mixer(x, prev, mod, g, w_in, w_grp, scale, w_out, T):
    D = x.shape[1]
    weights = [w_in.astype(BF16), w_grp.astype(BF16), scale.reshape(1, D), w_out.astype(BF16)]
    return _mixer_call(_pool_kernel, "mixer_pool", 512, T, x, prev, mod, g, weights,
                       [pltpu.VMEM((POOL_HALO, D), F32)])


def _conv_kernel(has_prev, tm, T, *refs):
    x, m_ref, g_ref, (win_ref, cw_ref, wout_ref, o_ref, tail_ref), prefetch = _mixer_input(has_prev, tm, refs)
    i = pl.program_id(0)
    D = x.shape[1]
    h = _norm_mod(x, g_ref[...], m_ref[1:2, :], m_ref[0:1, :]).astype(BF16)
    prefetch(0, 4)
    gate_b = _dot(h, win_ref[:, :D])
    prefetch(1, 4)
    z = _dot(h, win_ref[:, D:2 * D])
    prefetch(2, 4)
    z = z * _dot(h, win_ref[:, 2 * D:])
    prefetch(3, 4)

    @pl.when((i * tm) % T == 0)
    def _():
        tail_ref[...] = jnp.zeros_like(tail_ref)

    ze = jnp.concatenate([tail_ref[...], z], axis=0)
    zc = cw_ref[CONV_WIDTH - 1:CONV_WIDTH, :] * ze
    for j in range(1, CONV_WIDTH):
        zc = zc + cw_ref[CONV_WIDTH - 1 - j:CONV_WIDTH - j, :] * pltpu.roll(ze, j, 0)
    tail_ref[...] = z[tm - CONV_HALO:, :]
    y = gate_b * zc[CONV_HALO:]
    o_ref[...] = x + m_ref[2:3, :] * _dot(y, wout_ref[...])


def _conv_mixer(x, prev, mod, g, w_in, conv_w, w_out, T):
    D = x.shape[1]
    weights = [w_in.astype(BF16), conv_w, w_out.astype(BF16)]
    return _mixer_call(_conv_kernel, "mixer_conv", 512, T, x, prev, mod, g, weights,
                       [pltpu.VMEM((CONV_HALO, D), F32)])


def _swa_kernel(has_prev, tm, T, *refs):
    x, m_ref, g_ref, (win_ref, wout_ref, sink_ref, o_ref, q_scr, klo, khi, vlo, vhi, o_scr), prefetch = \
        _mixer_input(has_prev, tm, refs)
    i = pl.program_id(0)
    D = x.shape[1]
    W = SWA_WINDOW
    hd = D // SWA_HEADS
    G = SWA_HEADS // SWA_KV_HEADS
    kvd = SWA_KV_HEADS * hd
    assert 2 * hd == LANES
    seq_start = (i * tm) % T == 0
    kv_scr = (klo, khi, vlo, vhi)

    @pl.when(seq_start)
    def _():
        for r in kv_scr:
            r[0:W, :] = jnp.zeros((W, r.shape[1]), BF16)

    @pl.when(jnp.logical_not(seq_start))
    def _():
        for r in kv_scr:
            r[0:W, :] = r[tm:tm + W, :]

    h = _norm_mod(x, g_ref[...], m_ref[1:2, :], m_ref[0:1, :])
    qkv = _dot(h, win_ref[...])
    q_scr[...] = (qkv[:, :D] * (hd ** -0.5 * LOG2E)).astype(BF16)
    low = lax.broadcasted_iota(jnp.int32, (tm, LANES), 1) < hd
    for src, lo_ref, hi_ref in ((qkv[:, D:D + kvd], klo, khi), (qkv[:, D + kvd:], vlo, vhi)):
        for a in range(kvd // LANES):
            kg = src[:, a * LANES:(a + 1) * LANES]
            sw = pltpu.roll(kg, hd, 1)
            for par, (lo_v, hi_v) in enumerate(((kg, sw), (sw, kg))):
                c = (2 * a + par) * LANES
                lo_ref[W:, c:c + LANES] = jnp.where(low, lo_v, 0.0).astype(BF16)
                hi_ref[W:, c:c + LANES] = jnp.where(low, 0.0, hi_v).astype(BF16)
    qpos = lax.broadcasted_iota(jnp.int32, (W, 2 * W), 0)
    kj = lax.broadcasted_iota(jnp.int32, (W, 2 * W), 1)
    band = (kj > qpos) & (kj <= qpos + W)
    bias = jnp.where(band, 0.0, -jnp.inf)
    bias0 = jnp.where(band & (kj >= jnp.where(seq_start, W, 0)), 0.0, -jnp.inf)
    for j in range(tm // W):
        b = bias0 if j == 0 else bias
        for grp in range(SWA_HEADS // 2):
            prefetch(j * (SWA_HEADS // 2) + grp, (tm // W) * (SWA_HEADS // 2))
            hk = (2 * grp) // G
            qg = q_scr[j * W:(j + 1) * W, grp * LANES:(grp + 1) * LANES]
            pair = None
            for par, (k_ref, v_ref) in enumerate(((klo, vlo), (khi, vhi))):
                kk = k_ref[j * W:(j + 2) * W, hk * LANES:(hk + 1) * LANES]
                vv = v_ref[j * W:(j + 2) * W, hk * LANES:(hk + 1) * LANES]
                sink = sink_ref[2 * grp + par] * LOG2E
                s = lax.dot_general(qg, kk, (((1,), (1,)), ((), ())), preferred_element_type=F32) + b
                mx = jnp.maximum(jnp.max(s, axis=-1, keepdims=True), sink)
                p = jnp.exp2(s - mx)
                denom = jnp.sum(p, axis=-1, keepdims=True) + jnp.exp2(sink - mx)
                o = jnp.dot(p.astype(BF16), vv, preferred_element_type=F32) * (1.0 / denom)
                pair = o if pair is None else pair + o
            o_scr[j * W:(j + 1) * W, grp * LANES:(grp + 1) * LANES] = pair.astype(BF16)
    o_ref[...] = x + m_ref[2:3, :] * jnp.dot(o_scr[...], wout_ref[...], preferred_element_type=F32)


def _swa_mixer(x, prev, mod, g, w_in, sinks, w_out, T):
    D = x.shape[1]
    tm = 256
    rows = SWA_WINDOW + tm
    kv = pltpu.VMEM((rows, SWA_KV_HEADS * LANES), BF16)
    weights = [w_in.astype(BF16), w_out.astype(BF16)]
    return _mixer_call(_swa_kernel, "mixer_swa", tm, T, x, prev, mod, g, weights,
                       [pltpu.VMEM((tm, D), BF16), kv, kv, kv, kv, pltpu.VMEM((tm, D), BF16)], smem=[sinks])


HGRN_LEVELS = tuple(2 ** e for e in range(HGRN_CHUNK.bit_length() - 1, 0, -1))


def _hgrn_tri():
    tri = np.tril(np.ones((HGRN_CHUNK, HGRN_CHUNK), np.float32))
    return np.concatenate([tri, tri, tri], axis=1)


def _split3(a):
    hi = a.astype(BF16)
    r1 = a - hi.astype(F32)
    mid = r1.astype(BF16)
    lo = (r1 - mid.astype(F32)).astype(BF16)
    return hi, mid, lo


def _hgrn_kernel(layer, has_prev, tm, T, *refs):
    x, m_ref, g_ref, (win_ref, lbl_ref, ng_ref, wout_ref, tri_ref, o_ref, st_ref), prefetch = \
        _mixer_input(has_prev, tm, refs)
    i = pl.program_id(0)
    D = x.shape[1]
    dk = HGRN_HEAD_DIM
    C = HGRN_CHUNK

    @pl.when((i * tm) % T == 0)
    def _():
        st_ref[...] = jnp.zeros_like(st_ref)

    rows = [lbl_ref[j:j + 1, :] for j in range(lbl_ref.shape[0])]
    mx = functools.reduce(jnp.maximum, rows)
    es = [jnp.exp(r - mx) for r in rows]
    tot = functools.reduce(lambda a, b: a + b, es)
    lb = jnp.zeros_like(mx)
    for j in range(1, layer + 1):
        lb = lb + es[j] / tot

    h = _norm_mod(x, g_ref[...], m_ref[1:2, :], m_ref[0:1, :])
    proj = _dot(h, win_ref[...])
    row = lax.broadcasted_iota(jnp.int32, (C, C), 0)
    col = lax.broadcasted_iota(jnp.int32, (C, C), 1)
    pair_masks = [(row // B == col // B) & (row % B >= B // 2) & (col % B < B // 2) for B in HGRN_LEVELS]
    diag = row == col
    trow = lax.broadcasted_iota(jnp.int32, (tm, 1), 0)
    nc = tm // C
    n_units = (D // dk) * nc

    def block_row(a, B, r):
        a3 = a.reshape(tm // B, B, a.shape[1])
        return jnp.broadcast_to(a3[:, r:r + 1, :], a3.shape).reshape(a.shape)

    outs = []
    for hh in range(D // dk):
        sl = slice(hh * dk, (hh + 1) * dk)
        q = _silu(proj[:, sl])
        lbh = lb[:, sl]
        f = lbh + (1.0 - lbh) * _sigmoid(proj[:, D + hh * dk:D + (hh + 1) * dk])
        kk = 1.0 - f
        v = proj[:, 2 * D + hh * dk:2 * D + (hh + 1) * dk]
        gate = proj[:, 3 * D + hh * dk:3 * D + (hh + 1) * dk]
        lf = jnp.log2(f)
        parts = _split3(lf)
        rhs = jnp.concatenate(
            [jnp.concatenate([p[c * C:(c + 1) * C] for p in parts], axis=0) for c in range(nc)], axis=1)
        bb = jnp.dot(tri_ref[...], rhs, preferred_element_type=F32)
        b = jnp.concatenate([bb[:, c * dk:(c + 1) * dk] for c in range(nc)], axis=0)
        from_start = jnp.exp2(b)
        qs = (q * from_start).astype(BF16)
        ks = (kk * jnp.exp2(block_row(b, C, C - 1) - b)).astype(BF16)
        zs = []
        for B in HGRN_LEVELS:
            pos = trow % B
            if B >= 8:
                mid = block_row(b, B, B // 2 - 1)
                e = jnp.where(pos >= B // 2, b - mid, mid - b)
            elif B == 4:
                e = jnp.where(pos == 0, pltpu.roll(lf, tm - 1, 0),
                              jnp.where(pos == 1, 0.0, jnp.where(pos == 2, lf, lf + pltpu.roll(lf, 1, 0))))
            else:
                e = jnp.where(pos == 1, lf, 0.0)
            zs.append((jnp.where(pos >= B // 2, q, kk) * jnp.exp2(e)).astype(BF16))
        qk = jnp.sum(q * kk, axis=-1, keepdims=True)
        st = st_ref[hh]
        oc = []
        for c in range(nc):
            prefetch(hh * nc + c, n_units)
            rs = slice(c * C, (c + 1) * C)
            vc = v[rs].astype(BF16)
            sc = jnp.where(diag, qk[rs], 0.0)
            for z, mask in zip(zs, pair_masks):
                sc = sc + jnp.where(mask, _dot_nt(z[rs], z[rs]), 0.0)
            oc.append(_dot(sc, vc) + _dot_nt(qs[rs], st))
            st = st * from_start[(c + 1) * C - 1:(c + 1) * C, :] + _dot_tn(vc, ks[rs])
        st_ref[hh] = st
        o = jnp.concatenate(oc, axis=0)
        o = o * lax.rsqrt(jnp.mean(o * o, axis=-1, keepdims=True) + EPS) * ng_ref[...]
        outs.append(o * _silu(gate))
    y = jnp.concatenate(outs, axis=1)
    o_ref[...] = x + m_ref[2:3, :] * _dot(y, wout_ref[...])


def _hgrn_mixer(layer, x, prev, mod, g, w_in, lb_logits, norm_g, w_out, T):
    D = x.shape[1]
    dk = HGRN_HEAD_DIM
    weights = [w_in.astype(BF16), lb_logits, norm_g.reshape(1, dk), w_out.astype(BF16),
               jnp.asarray(_hgrn_tri(), BF16)]
    return _mixer_call(functools.partial(_hgrn_kernel, layer), "mixer_hgrn", 256, T, x, prev, mod, g, weights,
                       [pltpu.VMEM((D // dk, dk, dk), F32)])


def _split2(a):
    hi = a.astype(BF16)
    lo = (a - hi.astype(F32)).astype(BF16)
    return hi, lo


def _route_kernel(tm, x_ref, m_ref, g_ref, rw_ref, rb_ref, tri_ref, h_ref, cls_ref, rank_ref, cnt_ref, carry_ref):
    i = pl.program_id(0)

    @pl.when(i == 0)
    def _():
        carry_ref[...] = jnp.zeros_like(carry_ref)

    h = _norm_mod(x_ref[...], g_ref[...], m_ref[4:5, :], m_ref[3:4, :])
    _tt_store(h_ref, h)
    h_hi, h_lo = _split2(h)
    w_hi, w_lo = _split2(rw_ref[...])
    logits = (jnp.dot(h_hi, w_hi, preferred_element_type=F32) + jnp.dot(h_lo, w_hi, preferred_element_type=F32)
              + jnp.dot(h_hi, w_lo, preferred_element_type=F32))
    lt = jnp.transpose(logits)[:N_EXPERTS, :]
    score = _sigmoid(lt)
    sel = score + rb_ref[...]
    gscore, gsel = [], []
    for gi in range(N_GROUPS):
        r = [sel[gi * 4 + e:gi * 4 + e + 1, :] for e in range(EXPERTS_PER_GROUP)]
        m1 = functools.reduce(jnp.maximum, r)
        m2 = None
        for a in range(EXPERTS_PER_GROUP):
            for b2 in range(a + 1, EXPERTS_PER_GROUP):
                pm = jnp.minimum(r[a], r[b2])
                m2 = pm if m2 is None else jnp.maximum(m2, pm)
        gscore.append(m1 + m2)
        gsel.append(r)
    best = jnp.zeros((1, tm), jnp.int32)
    bs = gscore[0]
    for gi in range(1, N_GROUPS):
        better = gscore[gi] > bs
        best = jnp.where(better, gi, best)
        bs = jnp.where(better, gscore[gi], bs)
    r = [functools.reduce(lambda a, b2: a + b2,
                          [jnp.where(best == gi, gsel[gi][e], 0.0) for gi in range(N_GROUPS)])
         for e in range(EXPERTS_PER_GROUP)]
    keep = []
    for e in range(EXPERTS_PER_GROUP):
        beaten = jnp.zeros((1, tm), jnp.int32)
        for o in range(EXPERTS_PER_GROUP):
            if o != e:
                wins = (r[o] > r[e]) | ((r[o] == r[e]) & (o < e))
                beaten = beaten + wins.astype(jnp.int32)
        keep.append(beaten < 2)
    pair = jnp.zeros((1, tm), jnp.int32)
    for pi, (a, b2) in enumerate(PAIRS):
        pair = jnp.where(keep[a] & keep[b2], pi, pair)
    cls = best * len(PAIRS) + pair
    cls_ref[...] = cls.reshape(1, 1, tm)
    onehot = (lax.broadcasted_iota(jnp.int32, (CLASS_ROWS, tm), 0) == cls).astype(F32)
    before = jnp.dot(onehot.astype(BF16), tri_ref[...], preferred_element_type=F32) + carry_ref[:, 0:1]
    rank = jnp.sum(onehot * before, axis=0, keepdims=True)
    rank_ref[...] = rank.astype(jnp.int32).reshape(1, 1, tm)
    carry_ref[...] = carry_ref[...] + jnp.sum(onehot, axis=1, keepdims=True)
    cnt_ref[...] = carry_ref[...]


def _route(x1, mod, g, router_w, router_bias, T):
    N, D = x1.shape
    tm = 512
    nt = N // tm
    E = router_w.shape[1]
    rw = jnp.zeros((D, LANES), F32).at[:, :E].set(router_w)
    rb = router_bias.reshape(E, 1)
    tri = (jnp.arange(tm)[:, None] < jnp.arange(tm)[None, :]).astype(BF16)
    row = pl.BlockSpec((tm, D), lambda i: (i, 0))
    modspec = pl.BlockSpec((None, 6, D), lambda i: ((i * tm) // T, 0, 0))
    tok = pl.BlockSpec((1, 1, tm), lambda i: (i, 0, 0))
    h2, cls, rank, cnt = pl.pallas_call(
        functools.partial(_route_kernel, tm),
        grid=(nt,),
        in_specs=[row, modspec, _full((1, D)), _full((D, LANES)), _full((E, 1)), _full((tm, tm))],
        out_specs=[pl.BlockSpec((tm * D // LANES, LANES), lambda i: (i, 0)), tok, tok, _full((CLASS_ROWS, LANES))],
        out_shape=[jax.ShapeDtypeStruct((N * D // LANES, LANES), F32),
                   jax.ShapeDtypeStruct((nt, 1, tm), jnp.int32),
                   jax.ShapeDtypeStruct((nt, 1, tm), jnp.int32),
                   jax.ShapeDtypeStruct((CLASS_ROWS, LANES), F32)],
        scratch_shapes=[pltpu.VMEM((CLASS_ROWS, LANES), F32)],
        compiler_params=_cparams(),
        name="router",
    )(x1, mod, g.reshape(1, D), rw, rb, tri)
    return h2, cls.reshape(N), rank.reshape(N), cnt[:N_CLASSES, 0]


def _tok(ref, t, rpt):
    return ref.at[pl.ds(pl.multiple_of(t * rpt, rpt), rpt), :]


def _scatter_rows_kernel(chunk, rpt, idx_ref, src_ref, init_ref, dst_ref, sem):
    del init_ref
    base = pl.program_id(0) * chunk

    def issue(r, carry):
        pltpu.make_async_copy(_tok(src_ref, r, rpt), _tok(dst_ref, idx_ref[base + r], rpt), sem).start()
        return carry

    lax.fori_loop(0, chunk, issue, 0, unroll=8)
    pltpu.make_async_copy(src_ref, dst_ref.at[pl.ds(0, chunk * rpt), :], sem).wait()


def _scatter_rows(src, idx, n_out, rpt):
    N = idx.shape[0]
    chunk = min(PERM_CHUNK, N)
    init = jnp.zeros((n_out * rpt, LANES), src.dtype)
    return pl.pallas_call(
        functools.partial(_scatter_rows_kernel, chunk, rpt),
        grid_spec=pltpu.PrefetchScalarGridSpec(
            num_scalar_prefetch=1, grid=(N // chunk,),
            in_specs=[pl.BlockSpec((chunk * rpt, LANES), lambda i, idx: (i, 0)), pl.BlockSpec(memory_space=pl.ANY)],
            out_specs=pl.BlockSpec(memory_space=pl.ANY),
            scratch_shapes=[pltpu.SemaphoreType.DMA(())]),
        out_shape=jax.ShapeDtypeStruct(init.shape, src.dtype),
        input_output_aliases={2: 0},
        compiler_params=_cparams(),
        name="scatter_rows",
    )(idx, src, init)


def _ffn_kernel(ea_ref, eb_ref, nact_ref, x_ref, rwt_ref, wga_ref, wua_ref, wda_ref, wgb_ref, wub_ref, wdb_ref,
                o_ref):
    j = pl.program_id(0)

    @pl.when(j < nact_ref[0])
    def _():
        x = _tt_load(x_ref, FFN_BLOCK, wga_ref.shape[0])
        sa = _sigmoid(jnp.sum(x * rwt_ref[pl.ds(ea_ref[j], 1), :], axis=-1, keepdims=True))
        sb = _sigmoid(jnp.sum(x * rwt_ref[pl.ds(eb_ref[j], 1), :], axis=-1, keepdims=True))
        inv = 1.0 / (sa + sb)
        xb = x.astype(BF16)
        aa = _silu(_dot(xb, wga_ref[...])) * _dot(xb, wua_ref[...]) * (sa * inv)
        ab = _silu(_dot(xb, wgb_ref[...])) * _dot(xb, wub_ref[...]) * (sb * inv)
        _tt_store(o_ref, _dot(aa, wda_ref[...]) + _dot(ab, wdb_ref[...]))

    @pl.when(j >= nact_ref[0])
    def _():
        o_ref[...] = jnp.zeros_like(o_ref)


def _ffn(xs, blk_ea, blk_eb, n_active, router_wt, w_gate, w_up, w_down):
    D, F = w_gate.shape[1:]
    rpt = D // LANES
    nb = xs.shape[0] // (FFN_BLOCK * rpt)

    def wa(j, ea, eb, na):
        return (ea[j], 0, 0)

    def wb(j, ea, eb, na):
        return (eb[j], 0, 0)

    row = pl.BlockSpec((FFN_BLOCK * rpt, LANES), lambda j, ea, eb, na: (j, 0))
    return pl.pallas_call(
        _ffn_kernel,
        grid_spec=pltpu.PrefetchScalarGridSpec(
            num_scalar_prefetch=3, grid=(nb,),
            in_specs=[row, pl.BlockSpec(router_wt.shape, lambda j, ea, eb, na: (0, 0)),
                      pl.BlockSpec((None, D, F), wa), pl.BlockSpec((None, D, F), wa), pl.BlockSpec((None, F, D), wa),
                      pl.BlockSpec((None, D, F), wb), pl.BlockSpec((None, D, F), wb), pl.BlockSpec((None, F, D), wb)],
            out_specs=row),
        out_shape=jax.ShapeDtypeStruct(xs.shape, F32),
        compiler_params=_cparams(),
        name="moe_ffn",
    )(blk_ea, blk_eb, n_active, xs, router_wt, w_gate, w_up, w_down, w_gate, w_up, w_down)


def _moe(layer, x1, mod, g, router_w, router_bias, w_gate, w_up, w_down, T):
    N, D = x1.shape
    h2, cls, rank, cnt = _route(x1, mod, g, router_w, router_bias, T)
    counts = cnt.astype(jnp.int32)
    padded = (counts + FFN_BLOCK - 1) // FFN_BLOCK * FFN_BLOCK
    ends = jnp.cumsum(padded)
    starts = ends - padded
    dest = starts[cls] + rank
    nb = N // FFN_BLOCK + N_CLASSES
    n_active = (ends[-1] // FFN_BLOCK).astype(jnp.int32)
    blk = jnp.arange(nb, dtype=jnp.int32)
    blk_start = jnp.minimum(blk, n_active - 1) * FFN_BLOCK
    blk_cls = jnp.sum((ends[None, :] <= blk_start[:, None]).astype(jnp.int32), axis=1)
    blk_cls = jnp.minimum(blk_cls, N_CLASSES - 1)
    pair_a = jnp.array([p[0] for p in PAIRS], jnp.int32)
    pair_b = jnp.array([p[1] for p in PAIRS], jnp.int32)
    grp = blk_cls // len(PAIRS)
    blk_ea = grp * EXPERTS_PER_GROUP + pair_a[blk_cls % len(PAIRS)]
    blk_eb = grp * EXPERTS_PER_GROUP + pair_b[blk_cls % len(PAIRS)]
    rpt = D // LANES
    xs = _scatter_rows(h2, dest, nb * FFN_BLOCK, rpt)
    ys = _ffn(xs, blk_ea, blk_eb, n_active.reshape(1), jnp.transpose(router_w),
              _cast_bf16(w_gate, layer), _cast_bf16(w_up, layer), _cast_bf16(w_down, layer))
    return ys, dest


def _final_kernel(tm, dest_ref, x_ref, ys_ref, pm_ref, g_ref, o_ref, buf, sem):
    x = x_ref[...]
    y, prefetch = _gathered_rows(dest_ref, ys_ref, buf, sem, tm, x.shape[1])
    prefetch(0, 1)
    x = x + pm_ref[5:6, :] * y
    ms = jnp.mean(x * x, axis=-1, keepdims=True)
    o_ref[...] = x * lax.rsqrt(ms + EPS) * g_ref[...]


def _final(x, prev, g, T):
    ys, dest, mod = prev
    N, D = x.shape
    tm = min(512, T)
    row = pl.BlockSpec((tm, D), lambda i, *_: (i, 0))
    modspec = pl.BlockSpec((None, 6, D), lambda i, *_: ((i * tm) // T, 0, 0))
    return pl.pallas_call(
        functools.partial(_final_kernel, tm),
        grid_spec=pltpu.PrefetchScalarGridSpec(
            num_scalar_prefetch=1, grid=(N // tm,),
            in_specs=[row, pl.BlockSpec(memory_space=pl.ANY), modspec, _full((1, D))],
            out_specs=row,
            scratch_shapes=[pltpu.VMEM((2, tm * D // LANES, LANES), F32), pltpu.SemaphoreType.DMA((2,))]),
        out_shape=jax.ShapeDtypeStruct((N, D), F32),
        compiler_params=_cparams(),
        name="final_norm",
    )(dest, x, ys, mod, g.reshape(1, D))


def kernel(x, c, ada_w, ada_b, norm_g, final_norm_g, pool_w_in, pool_w_grp, pool_scale, pool_w_out, hgrn_w_in, hgrn_lb_logits, hgrn_norm_g, hgrn_w_out, swa_w_in, swa_sinks, swa_w_out, conv_w_in, conv_w, conv_w_out, router_w, router_bias, moe_w_gate, moe_w_up, moe_w_down):
    B, T, D = x.shape
    depth = ada_w.shape[0]
    n_mixers = 4
    mod = _ada(c, ada_w, ada_b).reshape(depth, B, 6, D)
    xt = x.reshape(B * T, D)
    prev = None
    for i in range(depth):
        m, j = i % n_mixers, i // n_mixers
        g1n = norm_g[i, 0]
        if m == 0:
            x1 = _pool_mixer(xt, prev, mod[i], g1n, pool_w_in[j], pool_w_grp[j], pool_scale[j], pool_w_out[j], T)
        elif m == 1:
            x1 = _hgrn_mixer(i, xt, prev, mod[i], g1n, hgrn_w_in[j], hgrn_lb_logits, hgrn_norm_g[j],
                             hgrn_w_out[j], T)
        elif m == 2:
            x1 = _swa_mixer(xt, prev, mod[i], g1n, swa_w_in[j], swa_sinks[j], swa_w_out[j], T)
        else:
            x1 = _conv_mixer(xt, prev, mod[i], g1n, conv_w_in[j], conv_w[j], conv_w_out[j], T)
        ys, dest = _moe(i, x1, mod[i], norm_g[i, 1], router_w, router_bias, moe_w_gate, moe_w_up, moe_w_down, T)
        xt = x1
        prev = (ys, dest, mod[i])
    out = _final(xt, prev, final_norm_g, T)
    return out.reshape(B, T, D)
```

```python
import functools

import jax
import jax.numpy as jnp
import numpy as np
from jax import lax
from jax.experimental import pallas as pl
from jax.experimental.pallas import tpu as pltpu

F32 = jnp.float32
BF16 = jnp.bfloat16
EPS = 1e-6
LOG2E = 1.4426950408889634

POOL_WINDOWS = (2, 4, 8, 16)
POOL_HALO = 16
HGRN_HEAD_DIM = 128
HGRN_CHUNK = 64
SWA_HEADS = 16
SWA_KV_HEADS = 4
SWA_WINDOW = 128
CONV_WIDTH = 3
CONV_HALO = 8
N_EXPERTS = 16
N_GROUPS = 4
EXPERTS_PER_GROUP = 4
PAIRS = ((0, 1), (0, 2), (0, 3), (1, 2), (1, 3), (2, 3))
N_CLASSES = N_GROUPS * len(PAIRS)
CLASS_ROWS = 32
LANES = 128
FFN_BLOCK = 256
PERM_CHUNK = 2048
VMEM_LIMIT = 52 * 1024 * 1024


def _cparams():
    return pltpu.CompilerParams(dimension_semantics=("arbitrary",), vmem_limit_bytes=VMEM_LIMIT)


def _full(shape):
    nd = len(shape)
    return pl.BlockSpec(shape, lambda i, *_: (0,) * nd)


def _norm_mod(x, g, sc, sh):
    ms = jnp.mean(x * x, axis=-1, keepdims=True)
    return x * lax.rsqrt(ms + EPS) * (g * (1.0 + sc)) + sh


def _sigmoid(x):
    return 1.0 / (1.0 + jnp.exp(-x))


def _silu(x):
    return x * _sigmoid(x)


def _dot(a, b):
    return jnp.dot(a.astype(BF16), b.astype(BF16), preferred_element_type=F32)


def _dot_nt(a, b):
    return lax.dot_general(a.astype(BF16), b.astype(BF16), (((1,), (1,)), ((), ())), preferred_element_type=F32)


def _dot_tn(a, b):
    return lax.dot_general(a.astype(BF16), b.astype(BF16), (((0,), (0,)), ((), ())), preferred_element_type=F32)


def _tt_load(ref, rows, d):
    n = d // LANES
    return jnp.concatenate([ref[pl.ds(j, rows, stride=n), :] for j in range(n)], axis=1)


def _tt_store(ref, val):
    rows, d = val.shape
    n = d // LANES
    for j in range(n):
        ref[pl.ds(j, rows, stride=n), :] = val[:, j * LANES:(j + 1) * LANES]


def _cast_kernel(w_ref, o_ref):
    o_ref[...] = w_ref[...].astype(BF16)


def _cast_bf16(w, layer):
    _, E, K, M = w.shape
    return pl.pallas_call(
        _cast_kernel,
        grid=(E,),
        in_specs=[pl.BlockSpec((None, None, K, M), lambda e: (layer, e, 0, 0))],
        out_specs=pl.BlockSpec((None, K, M), lambda e: (e, 0, 0)),
        out_shape=jax.ShapeDtypeStruct((E, K, M), BF16),
        compiler_params=_cparams(),
        name="cast_bf16",
    )(w)


def _ada_kernel(c_ref, w_ref, b_ref, o_ref):
    cond = _silu(c_ref[...])
    o_ref[...] = jnp.dot(cond, w_ref[...], preferred_element_type=F32, precision=lax.Precision.HIGHEST) + b_ref[...]


def _ada(c, ada_w, ada_b):
    L, D, D6 = ada_w.shape
    B = c.shape[0]
    bn = D6 // 4
    return pl.pallas_call(
        _ada_kernel,
        grid=(L, D6 // bn),
        in_specs=[pl.BlockSpec((B, D), lambda l, j: (0, 0)),
                  pl.BlockSpec((None, D, bn), lambda l, j: (l, 0, j)),
                  pl.BlockSpec((None, 1, bn), lambda l, j: (l, 0, j))],
        out_specs=pl.BlockSpec((None, B, bn), lambda l, j: (l, 0, j)),
        out_shape=jax.ShapeDtypeStruct((L, B, D6), F32),
        compiler_params=pltpu.CompilerParams(dimension_semantics=("arbitrary", "arbitrary"),
                                             vmem_limit_bytes=VMEM_LIMIT),
        name="ada_mod",
    )(c, ada_w, ada_b.reshape(L, 1, D6))


def _mixer_call(body, name, tm, T, x, prev, mod, g, weights, scratch, smem=()):
    N, D = x.shape
    rpt = D // LANES
    row = pl.BlockSpec((tm, D), lambda i, *_: (i, 0))
    modspec = pl.BlockSpec((None, 6, D), lambda i, *_: ((i * tm) // T, 0, 0))
    args, specs, prefetch = [x], [row], []
    scratch = list(scratch)
    if prev is not None:
        prefetch = [prev[1]]
        args += [prev[0], prev[2]]
        specs += [pl.BlockSpec(memory_space=pl.ANY), modspec]
        scratch += [pltpu.VMEM((2, tm * rpt, LANES), F32), pltpu.SemaphoreType.DMA((2,))]
    args += [mod, g.reshape(1, D)]
    specs += [modspec, _full((1, D))]
    for w in weights:
        args.append(w)
        specs.append(_full(w.shape))
    for s in smem:
        args.append(s)
        specs.append(pl.BlockSpec(memory_space=pltpu.SMEM))
    return pl.pallas_call(
        functools.partial(body, prev is not None, tm, T),
        grid_spec=pltpu.PrefetchScalarGridSpec(
            num_scalar_prefetch=len(prefetch), grid=(N // tm,), in_specs=specs, out_specs=row,
            scratch_shapes=scratch),
        out_shape=jax.ShapeDtypeStruct((N, D), F32),
        compiler_params=_cparams(),
        name=name,
    )(*prefetch, *args)


def _gathered_rows(dest_ref, ys_ref, buf, sem, tm, d):
    i = pl.program_id(0)
    last = pl.num_programs(0) - 1
    rpt = d // LANES

    def issue(tile, slot, r):
        src = _tok(ys_ref, dest_ref[tile * tm + r], rpt)
        pltpu.make_async_copy(src, _tok(buf.at[slot], r, rpt), sem.at[slot]).start()

    def wait(slot):
        pltpu.make_async_copy(ys_ref.at[pl.ds(0, tm * rpt), :], buf.at[slot], sem.at[slot]).wait()

    @pl.when(i == 0)
    def _():
        lax.fori_loop(0, tm, lambda r, c: (issue(0, 0, r), c)[1], 0, unroll=8)

    nxt = jnp.minimum(i + 1, last)

    def prefetch(k, n):
        for r in range(k * (tm // n), (k + 1) * (tm // n)):
            issue(nxt, (i + 1) % 2, r)

    def finish():
        @pl.when(i == last)
        def _():
            wait((i + 1) % 2)

    wait(i % 2)
    return _tt_load(buf.at[i % 2], tm, d), prefetch, finish


def _mixer_input(has_prev, tm, refs):
    if has_prev:
        dest_ref, x_ref, ys_ref, pm_ref, m_ref, g_ref = refs[:6]
        buf, sem = refs[-2:]
        x = x_ref[...]
        y, prefetch, finish = _gathered_rows(dest_ref, ys_ref, buf, sem, tm, x.shape[1])
        x = x + pm_ref[5:6, :] * y
        rest = refs[6:-2]
    else:
        x_ref, m_ref, g_ref = refs[:3]
        x = x_ref[...]
        rest = refs[3:]
        prefetch = lambda k, n: None
        finish = lambda: None
    return x, m_ref, g_ref, rest, prefetch, finish


def _pool_kernel(has_prev, tm, T, *refs):
    x, m_ref, g_ref, (win_ref, wgrp_ref, scale_ref, wout_ref, o_ref, tail_ref), prefetch, finish = \
        _mixer_input(has_prev, tm, refs)
    i = pl.program_id(0)
    start = (i * tm) % T
    n_pre = 2 * len(POOL_WINDOWS)
    h = _norm_mod(x, g_ref[...], m_ref[1:2, :], m_ref[0:1, :])
    u = _dot(h, win_ref[...])

    @pl.when(start == 0)
    def _():
        tail_ref[...] = jnp.zeros_like(tail_ref)

    pos = start + lax.broadcasted_iota(jnp.int32, (tm, 1), 0)
    C = u.shape[1] // len(POOL_WINDOWS)
    ys = []
    for gi, w in enumerate(POOL_WINDOWS):
        prefetch(2 * gi, n_pre)
        ug = u[:, gi * C:(gi + 1) * C]
        s = jnp.concatenate([tail_ref[:, gi * C:(gi + 1) * C], ug], axis=0)
        k = 1
        while k < w:
            s = s + pltpu.roll(s, k, 0)
            k *= 2
        cnt = jnp.minimum(pos + 1, w).astype(F32)
        pooled = s[POOL_HALO:] / cnt - ug
        prefetch(2 * gi + 1, n_pre)
        ys.append(_dot(pooled, wgrp_ref[gi]))
    tail_ref[...] = u[tm - POOL_HALO:, :]
    y = jnp.concatenate(ys, axis=1) * scale_ref[...]
    o_ref[...] = x + m_ref[2:3, :] * _dot(y, wout_ref[...])
    finish()


def _pool_mixer(x, prev, mod, g, w_in, w_grp, scale, w_out, T):
    D = x.shape[1]
    weights = [w_in.astype(BF16), w_grp.astype(BF16), scale.reshape(1, D), w_out.astype(BF16)]
    return _mixer_call(_pool_kernel, "mixer_pool", 512, T, x, prev, mod, g, weights,
                       [pltpu.VMEM((POOL_HALO, D), F32)])


def _conv_kernel(has_prev, tm, T, *refs):
    x, m_ref, g_ref, (win_ref, cw_ref, wout_ref, o_ref, tail_ref), prefetch, finish = \
        _mixer_input(has_prev, tm, refs)
    i = pl.program_id(0)
    D = x.shape[1]
    h = _norm_mod(x, g_ref[...], m_ref[1:2, :], m_ref[0:1, :]).astype(BF16)
    prefetch(0, 4)
    gate_b = _dot(h, win_ref[:, :D])
    prefetch(1, 4)
    z = _dot(h, win_ref[:, D:2 * D])
    prefetch(2, 4)
    z = z * _dot(h, win_ref[:, 2 * D:])
    prefetch(3, 4)

    @pl.when((i * tm) % T == 0)
    def _():
        tail_ref[...] = jnp.zeros_like(tail_ref)

    ze = jnp.concatenate([tail_ref[...], z], axis=0)
    zc = cw_ref[CONV_WIDTH - 1:CONV_WIDTH, :] * ze
    for j in range(1, CONV_WIDTH):
        zc = zc + cw_ref[CONV_WIDTH - 1 - j:CONV_WIDTH - j, :] * pltpu.roll(ze, j, 0)
    tail_ref[...] = z[tm - CONV_HALO:, :]
    y = gate_b * zc[CONV_HALO:]
    o_ref[...] = x + m_ref[2:3, :] * _dot(y, wout_ref[...])
    finish()


def _conv_mixer(x, prev, mod, g, w_in, conv_w, w_out, T):
    D = x.shape[1]
    weights = [w_in.astype(BF16), conv_w, w_out.astype(BF16)]
    return _mixer_call(_conv_kernel, "mixer_conv", 512, T, x, prev, mod, g, weights,
                       [pltpu.VMEM((CONV_HALO, D), F32)])


def _swa_kernel(has_prev, tm, T, *refs):
    x, m_ref, g_ref, (win_ref, wout_ref, sink_ref, o_ref, q_scr, klo, khi, vlo, vhi, o_scr), prefetch, finish = \
        _mixer_input(has_prev, tm, refs)
    i = pl.program_id(0)
    D = x.shape[1]
    W = SWA_WINDOW
    hd = D // SWA_HEADS
    G = SWA_HEADS // SWA_KV_HEADS
    kvd = SWA_KV_HEADS * hd
    assert 2 * hd == LANES
    seq_start = (i * tm) % T == 0
    kv_scr = (klo, khi, vlo, vhi)

    @pl.when(seq_start)
    def _():
        for r in kv_scr:
            r[0:W, :] = jnp.zeros((W, r.shape[1]), BF16)

    @pl.when(jnp.logical_not(seq_start))
    def _():
        for r in kv_scr:
            r[0:W, :] = r[tm:tm + W, :]

    h = _norm_mod(x, g_ref[...], m_ref[1:2, :], m_ref[0:1, :])
    qkv = _dot(h, win_ref[...])
    q_scr[...] = (qkv[:, :D] * (hd ** -0.5 * LOG2E)).astype(BF16)
    low = lax.broadcasted_iota(jnp.int32, (tm, LANES), 1) < hd
    for src, lo_ref, hi_ref in ((qkv[:, D:D + kvd], klo, khi), (qkv[:, D + kvd:], vlo, vhi)):
        for a in range(kvd // LANES):
            kg = src[:, a * LANES:(a + 1) * LANES]
            sw = pltpu.roll(kg, hd, 1)
            for par, (lo_v, hi_v) in enumerate(((kg, sw), (sw, kg))):
                c = (2 * a + par) * LANES
                lo_ref[W:, c:c + LANES] = jnp.where(low, lo_v, 0.0).astype(BF16)
                hi_ref[W:, c:c + LANES] = jnp.where(low, 0.0, hi_v).astype(BF16)
    qpos = lax.broadcasted_iota(jnp.int32, (W, 2 * W), 0)
    kj = lax.broadcasted_iota(jnp.int32, (W, 2 * W), 1)
    band = (kj > qpos) & (kj <= qpos + W)
    bias = jnp.where(band, 0.0, -jnp.inf)
    bias0 = jnp.where(band & (kj >= jnp.where(seq_start, W, 0)), 0.0, -jnp.inf)
    for j in range(tm // W):
        b = bias0 if j == 0 else bias
        for grp in range(SWA_HEADS // 2):
            prefetch(j * (SWA_HEADS // 2) + grp, (tm // W) * (SWA_HEADS // 2))
            hk = (2 * grp) // G
            qg = q_scr[j * W:(j + 1) * W, grp * LANES:(grp + 1) * LANES]
            pair = None
            for par, (k_ref, v_ref) in enumerate(((klo, vlo), (khi, vhi))):
                kk = k_ref[j * W:(j + 2) * W, hk * LANES:(hk + 1) * LANES]
                vv = v_ref[j * W:(j + 2) * W, hk * LANES:(hk + 1) * LANES]
                sink = sink_ref[2 * grp + par] * LOG2E
                s = lax.dot_general(qg, kk, (((1,), (1,)), ((), ())), preferred_element_type=F32) + b
                mx = jnp.maximum(jnp.max(s, axis=-1, keepdims=True), sink)
                p = jnp.exp2(s - mx)
                denom = jnp.sum(p, axis=-1, keepdims=True) + jnp.exp2(sink - mx)
                o = jnp.dot(p.astype(BF16), vv, preferred_element_type=F32) * (1.0 / denom)
                pair = o if pair is None else pair + o
            o_scr[j * W:(j + 1) * W, grp * LANES:(grp + 1) * LANES] = pair.astype(BF16)
    o_ref[...] = x + m_ref[2:3, :] * jnp.dot(o_scr[...], wout_ref[...], preferred_element_type=F32)
    finish()


def _swa_mixer(x, prev, mod, g, w_in, sinks, w_out, T):
    D = x.shape[1]
    tm = 256
    rows = SWA_WINDOW + tm
    kv = pltpu.VMEM((rows, SWA_KV_HEADS * LANES), BF16)
    weights = [w_in.astype(BF16), w_out.astype(BF16)]
    return _mixer_call(_swa_kernel, "mixer_swa", tm, T, x, prev, mod, g, weights,
                       [pltpu.VMEM((tm, D), BF16), kv, kv, kv, kv, pltpu.VMEM((tm, D), BF16)], smem=[sinks])


HGRN_LEVELS = tuple(2 ** e for e in range(HGRN_CHUNK.bit_length() - 1, 0, -1))


def _hgrn_tri():
    tri = np.tril(np.ones((HGRN_CHUNK, HGRN_CHUNK), np.float32))
    return np.concatenate([tri, tri, tri], axis=1)


def _split3(a):
    hi = a.astype(BF16)
    r1 = a - hi.astype(F32)
    mid = r1.astype(BF16)
    lo = (r1 - mid.astype(F32)).astype(BF16)
    return hi, mid, lo


def _hgrn_kernel(layer, has_prev, tm, T, *refs):
    x, m_ref, g_ref, (win_ref, lbl_ref, ng_ref, wout_ref, tri_ref, o_ref, st_ref), prefetch, finish = \
        _mixer_input(has_prev, tm, refs)
    i = pl.program_id(0)
    D = x.shape[1]
    dk = HGRN_HEAD_DIM
    C = HGRN_CHUNK

    @pl.when((i * tm) % T == 0)
    def _():
        st_ref[...] = jnp.zeros_like(st_ref)

    rows = [lbl_ref[j:j + 1, :] for j in range(lbl_ref.shape[0])]
    mx = functools.reduce(jnp.maximum, rows)
    es = [jnp.exp(r - mx) for r in rows]
    tot = functools.reduce(lambda a, b: a + b, es)
    lb = jnp.zeros_like(mx)
    for j in range(1, layer + 1):
        lb = lb + es[j] / tot

    h = _norm_mod(x, g_ref[...], m_ref[1:2, :], m_ref[0:1, :])
    proj = _dot(h, win_ref[...])
    row = lax.broadcasted_iota(jnp.int32, (C, C), 0)
    col = lax.broadcasted_iota(jnp.int32, (C, C), 1)
    pair_masks = [(row // B == col // B) & (row % B >= B // 2) & (col % B < B // 2) for B in HGRN_LEVELS]
    diag = row == col
    trow = lax.broadcasted_iota(jnp.int32, (tm, 1), 0)
    nc = tm // C
    n_units = (D // dk) * nc

    def block_row(a, B, r):
        a3 = a.reshape(tm // B, B, a.shape[1])
        return jnp.broadcast_to(a3[:, r:r + 1, :], a3.shape).reshape(a.shape)

    outs = []
    for hh in range(D // dk):
        sl = slice(hh * dk, (hh + 1) * dk)
        q = _silu(proj[:, sl])
        lbh = lb[:, sl]
        f = lbh + (1.0 - lbh) * _sigmoid(proj[:, D + hh * dk:D + (hh + 1) * dk])
        kk = 1.0 - f
        v = proj[:, 2 * D + hh * dk:2 * D + (hh + 1) * dk]
        gate = proj[:, 3 * D + hh * dk:3 * D + (hh + 1) * dk]
        lf = jnp.log2(f)
        parts = _split3(lf)
        rhs = jnp.concatenate(
            [jnp.concatenate([p[c * C:(c + 1) * C] for p in parts], axis=0) for c in range(nc)], axis=1)
        bb = jnp.dot(tri_ref[...], rhs, preferred_element_type=F32)
        b = jnp.concatenate([bb[:, c * dk:(c + 1) * dk] for c in range(nc)], axis=0)
        from_start = jnp.exp2(b)
        qs = (q * from_start).astype(BF16)
        ks = (kk * jnp.exp2(block_row(b, C, C - 1) - b)).astype(BF16)
        zs = []
        for B in HGRN_LEVELS:
            pos = trow % B
            if B >= 8:
                mid = block_row(b, B, B // 2 - 1)
                e = jnp.where(pos >= B // 2, b - mid, mid - b)
            elif B == 4:
                e = jnp.where(pos == 0, pltpu.roll(lf, tm - 1, 0),
                              jnp.where(pos == 1, 0.0, jnp.where(pos == 2, lf, lf + pltpu.roll(lf, 1, 0))))
            else:
                e = jnp.where(pos == 1, lf, 0.0)
            zs.append((jnp.where(pos >= B // 2, q, kk) * jnp.exp2(e)).astype(BF16))
        qk = jnp.sum(q * kk, axis=-1, keepdims=True)
        st = st_ref[hh]
        oc = []
        for c in range(nc):
            prefetch(hh * nc + c, n_units)
            rs = slice(c * C, (c + 1) * C)
            vc = v[rs].astype(BF16)
            sc = jnp.where(diag, qk[rs], 0.0)
            for z, mask in zip(zs, pair_masks):
                sc = sc + jnp.where(mask, _dot_nt(z[rs], z[rs]), 0.0)
            oc.append(_dot(sc, vc) + _dot_nt(qs[rs], st))
            st = st * from_start[(c + 1) * C - 1:(c + 1) * C, :] + _dot_tn(vc, ks[rs])
        st_ref[hh] = st
        o = jnp.concatenate(oc, axis=0)
        o = o * lax.rsqrt(jnp.mean(o * o, axis=-1, keepdims=True) + EPS) * ng_ref[...]
        outs.append(o * _silu(gate))
    y = jnp.concatenate(outs, axis=1)
    o_ref[...] = x + m_ref[2:3, :] * _dot(y, wout_ref[...])
    finish()


def _hgrn_mixer(layer, x, prev, mod, g, w_in, lb_logits, norm_g, w_out, T):
    D = x.shape[1]
    dk = HGRN_HEAD_DIM
    weights = [w_in.astype(BF16), lb_logits, norm_g.reshape(1, dk), w_out.astype(BF16),
               jnp.asarray(_hgrn_tri(), BF16)]
    return _mixer_call(functools.partial(_hgrn_kernel, layer), "mixer_hgrn", 256, T, x, prev, mod, g, weights,
                       [pltpu.VMEM((D // dk, dk, dk), F32)])


def _split2(a):
    hi = a.astype(BF16)
    lo = (a - hi.astype(F32)).astype(BF16)
    return hi, lo


def _route_kernel(tm, x_ref, m_ref, g_ref, rw_ref, rb_ref, tri_ref, h_ref, cls_ref, rank_ref, cnt_ref, carry_ref):
    i = pl.program_id(0)

    @pl.when(i == 0)
    def _():
        carry_ref[...] = jnp.zeros_like(carry_ref)

    h = _norm_mod(x_ref[...], g_ref[...], m_ref[4:5, :], m_ref[3:4, :])
    _tt_store(h_ref, h)
    h_hi, h_lo = _split2(h)
    w_hi, w_lo = _split2(rw_ref[...])
    logits = (jnp.dot(h_hi, w_hi, preferred_element_type=F32) + jnp.dot(h_lo, w_hi, preferred_element_type=F32)
              + jnp.dot(h_hi, w_lo, preferred_element_type=F32))
    lt = jnp.transpose(logits)[:N_EXPERTS, :]
    score = _sigmoid(lt)
    sel = score + rb_ref[...]
    gscore, gsel = [], []
    for gi in range(N_GROUPS):
        r = [sel[gi * 4 + e:gi * 4 + e + 1, :] for e in range(EXPERTS_PER_GROUP)]
        m1 = functools.reduce(jnp.maximum, r)
        m2 = None
        for a in range(EXPERTS_PER_GROUP):
            for b2 in range(a + 1, EXPERTS_PER_GROUP):
                pm = jnp.minimum(r[a], r[b2])
                m2 = pm if m2 is None else jnp.maximum(m2, pm)
        gscore.append(m1 + m2)
        gsel.append(r)
    best = jnp.zeros((1, tm), jnp.int32)
    bs = gscore[0]
    for gi in range(1, N_GROUPS):
        better = gscore[gi] > bs
        best = jnp.where(better, gi, best)
        bs = jnp.where(better, gscore[gi], bs)
    r = [functools.reduce(lambda a, b2: a + b2,
                          [jnp.where(best == gi, gsel[gi][e], 0.0) for gi in range(N_GROUPS)])
         for e in range(EXPERTS_PER_GROUP)]
    keep = []
    for e in range(EXPERTS_PER_GROUP):
        beaten = jnp.zeros((1, tm), jnp.int32)
        for o in range(EXPERTS_PER_GROUP):
            if o != e:
                wins = (r[o] > r[e]) | ((r[o] == r[e]) & (o < e))
                beaten = beaten + wins.astype(jnp.int32)
        keep.append(beaten < 2)
    pair = jnp.zeros((1, tm), jnp.int32)
    for pi, (a, b2) in enumerate(PAIRS):
        pair = jnp.where(keep[a] & keep[b2], pi, pair)
    cls = best * len(PAIRS) + pair
    cls_ref[...] = cls.reshape(1, 1, tm)
    onehot = (lax.broadcasted_iota(jnp.int32, (CLASS_ROWS, tm), 0) == cls).astype(F32)
    before = jnp.dot(onehot.astype(BF16), tri_ref[...], preferred_element_type=F32) + carry_ref[:, 0:1]
    rank = jnp.sum(onehot * before, axis=0, keepdims=True)
    rank_ref[...] = rank.astype(jnp.int32).reshape(1, 1, tm)
    carry_ref[...] = carry_ref[...] + jnp.sum(onehot, axis=1, keepdims=True)
    cnt_ref[...] = carry_ref[...]


def _route(x1, mod, g, router_w, router_bias, T):
    N, D = x1.shape
    tm = 512
    nt = N // tm
    E = router_w.shape[1]
    rw = jnp.zeros((D, LANES), F32).at[:, :E].set(router_w)
    rb = router_bias.reshape(E, 1)
    tri = (jnp.arange(tm)[:, None] < jnp.arange(tm)[None, :]).astype(BF16)
    row = pl.BlockSpec((tm, D), lambda i: (i, 0))
    modspec = pl.BlockSpec((None, 6, D), lambda i: ((i * tm) // T, 0, 0))
    tok = pl.BlockSpec((1, 1, tm), lambda i: (i, 0, 0))
    h2, cls, rank, cnt = pl.pallas_call(
        functools.partial(_route_kernel, tm),
        grid=(nt,),
        in_specs=[row, modspec, _full((1, D)), _full((D, LANES)), _full((E, 1)), _full((tm, tm))],
        out_specs=[pl.BlockSpec((tm * D // LANES, LANES), lambda i: (i, 0)), tok, tok, _full((CLASS_ROWS, LANES))],
        out_shape=[jax.ShapeDtypeStruct((N * D // LANES, LANES), F32),
                   jax.ShapeDtypeStruct((nt, 1, tm), jnp.int32),
                   jax.ShapeDtypeStruct((nt, 1, tm), jnp.int32),
                   jax.ShapeDtypeStruct((CLASS_ROWS, LANES), F32)],
        scratch_shapes=[pltpu.VMEM((CLASS_ROWS, LANES), F32)],
        compiler_params=_cparams(),
        name="router",
    )(x1, mod, g.reshape(1, D), rw, rb, tri)
    return h2, cls.reshape(N), rank.reshape(N), cnt[:N_CLASSES, 0]


def _tok(ref, t, rpt):
    return ref.at[pl.ds(pl.multiple_of(t * rpt, rpt), rpt), :]


def _scatter_rows_kernel(chunk, rpt, idx_ref, src_ref, init_ref, dst_ref, sem):
    del init_ref
    base = pl.program_id(0) * chunk

    def issue(r, carry):
        pltpu.make_async_copy(_tok(src_ref, r, rpt), _tok(dst_ref, idx_ref[base + r], rpt), sem).start()
        return carry

    lax.fori_loop(0, chunk, issue, 0, unroll=8)
    pltpu.make_async_copy(src_ref, dst_ref.at[pl.ds(0, chunk * rpt), :], sem).wait()


def _scatter_rows(src, idx, n_out, rpt):
    N = idx.shape[0]
    chunk = min(PERM_CHUNK, N)
    init = jnp.zeros((n_out * rpt, LANES), src.dtype)
    return pl.pallas_call(
        functools.partial(_scatter_rows_kernel, chunk, rpt),
        grid_spec=pltpu.PrefetchScalarGridSpec(
            num_scalar_prefetch=1, grid=(N // chunk,),
            in_specs=[pl.BlockSpec((chunk * rpt, LANES), lambda i, idx: (i, 0)), pl.BlockSpec(memory_space=pl.ANY)],
            out_specs=pl.BlockSpec(memory_space=pl.ANY),
            scratch_shapes=[pltpu.SemaphoreType.DMA(())]),
        out_shape=jax.ShapeDtypeStruct(init.shape, src.dtype),
        input_output_aliases={2: 0},
        compiler_params=_cparams(),
        name="scatter_rows",
    )(idx, src, init)


def _ffn_kernel(ea_ref, eb_ref, nact_ref, x_ref, rwt_ref, wga_ref, wua_ref, wda_ref, wgb_ref, wub_ref, wdb_ref,
                o_ref):
    j = pl.program_id(0)

    @pl.when(j < nact_ref[0])
    def _():
        x = _tt_load(x_ref, FFN_BLOCK, wga_ref.shape[0])
        sa = _sigmoid(jnp.sum(x * rwt_ref[pl.ds(ea_ref[j], 1), :], axis=-1, keepdims=True))
        sb = _sigmoid(jnp.sum(x * rwt_ref[pl.ds(eb_ref[j], 1), :], axis=-1, keepdims=True))
        inv = 1.0 / (sa + sb)
        xb = x.astype(BF16)
        aa = _silu(_dot(xb, wga_ref[...])) * _dot(xb, wua_ref[...]) * (sa * inv)
        ab = _silu(_dot(xb, wgb_ref[...])) * _dot(xb, wub_ref[...]) * (sb * inv)
        _tt_store(o_ref, _dot(aa, wda_ref[...]) + _dot(ab, wdb_ref[...]))

    @pl.when(j >= nact_ref[0])
    def _():
        o_ref[...] = jnp.zeros_like(o_ref)


def _ffn(xs, blk_ea, blk_eb, n_active, router_wt, w_gate, w_up, w_down):
    D, F = w_gate.shape[1:]
    rpt = D // LANES
    nb = xs.shape[0] // (FFN_BLOCK * rpt)

    def wa(j, ea, eb, na):
        return (ea[j], 0, 0)

    def wb(j, ea, eb, na):
        return (eb[j], 0, 0)

    row = pl.BlockSpec((FFN_BLOCK * rpt, LANES), lambda j, ea, eb, na: (j, 0))
    return pl.pallas_call(
        _ffn_kernel,
        grid_spec=pltpu.PrefetchScalarGridSpec(
            num_scalar_prefetch=3, grid=(nb,),
            in_specs=[row, pl.BlockSpec(router_wt.shape, lambda j, ea, eb, na: (0, 0)),
                      pl.BlockSpec((None, D, F), wa), pl.BlockSpec((None, D, F), wa), pl.BlockSpec((None, F, D), wa),
                      pl.BlockSpec((None, D, F), wb), pl.BlockSpec((None, D, F), wb), pl.BlockSpec((None, F, D), wb)],
            out_specs=row),
        out_shape=jax.ShapeDtypeStruct(xs.shape, F32),
        compiler_params=_cparams(),
        name="moe_ffn",
    )(blk_ea, blk_eb, n_active, xs, router_wt, w_gate, w_up, w_down, w_gate, w_up, w_down)


def _moe(layer, x1, mod, g, router_w, router_bias, w_gate, w_up, w_down, T):
    N, D = x1.shape
    h2, cls, rank, cnt = _route(x1, mod, g, router_w, router_bias, T)
    counts = cnt.astype(jnp.int32)
    padded = (counts + FFN_BLOCK - 1) // FFN_BLOCK * FFN_BLOCK
    ends = jnp.cumsum(padded)
    starts = ends - padded
    dest = starts[cls] + rank
    nb = N // FFN_BLOCK + N_CLASSES
    n_active = (ends[-1] // FFN_BLOCK).astype(jnp.int32)
    blk = jnp.arange(nb, dtype=jnp.int32)
    blk_start = jnp.minimum(blk, n_active - 1) * FFN_BLOCK
    blk_cls = jnp.sum((ends[None, :] <= blk_start[:, None]).astype(jnp.int32), axis=1)
    blk_cls = jnp.minimum(blk_cls, N_CLASSES - 1)
    pair_a = jnp.array([p[0] for p in PAIRS], jnp.int32)
    pair_b = jnp.array([p[1] for p in PAIRS], jnp.int32)
    grp = blk_cls // len(PAIRS)
    blk_ea = grp * EXPERTS_PER_GROUP + pair_a[blk_cls % len(PAIRS)]
    blk_eb = grp * EXPERTS_PER_GROUP + pair_b[blk_cls % len(PAIRS)]
    rpt = D // LANES
    xs = _scatter_rows(h2, dest, nb * FFN_BLOCK, rpt)
    ys = _ffn(xs, blk_ea, blk_eb, n_active.reshape(1), jnp.transpose(router_w),
              _cast_bf16(w_gate, layer), _cast_bf16(w_up, layer), _cast_bf16(w_down, layer))
    return ys, dest


def _final_kernel(tm, dest_ref, x_ref, ys_ref, pm_ref, g_ref, o_ref, buf, sem):
    x = x_ref[...]
    y, prefetch, finish = _gathered_rows(dest_ref, ys_ref, buf, sem, tm, x.shape[1])
    prefetch(0, 1)
    x = x + pm_ref[5:6, :] * y
    ms = jnp.mean(x * x, axis=-1, keepdims=True)
    o_ref[...] = x * lax.rsqrt(ms + EPS) * g_ref[...]
    finish()


def _final(x, prev, g, T):
    ys, dest, mod = prev
    N, D = x.shape
    tm = min(512, T)
    row = pl.BlockSpec((tm, D), lambda i, *_: (i, 0))
    modspec = pl.BlockSpec((None, 6, D), lambda i, *_: ((i * tm) // T, 0, 0))
    return pl.pallas_call(
        functools.partial(_final_kernel, tm),
        grid_spec=pltpu.PrefetchScalarGridSpec(
            num_scalar_prefetch=1, grid=(N // tm,),
            in_specs=[row, pl.BlockSpec(memory_space=pl.ANY), modspec, _full((1, D))],
            out_specs=row,
            scratch_shapes=[pltpu.VMEM((2, tm * D // LANES, LANES), F32), pltpu.SemaphoreType.DMA((2,))]),
        out_shape=jax.ShapeDtypeStruct((N, D), F32),
        compiler_params=_cparams(),
        name="final_norm",
    )(dest, x, ys, mod, g.reshape(1, D))


def kernel(x, c, ada_w, ada_b, norm_g, final_norm_g, pool_w_in, pool_w_grp, pool_scale, pool_w_out, hgrn_w_in, hgrn_lb_logits, hgrn_norm_g, hgrn_w_out, swa_w_in, swa_sinks, swa_w_out, conv_w_in, conv_w, conv_w_out, router_w, router_bias, moe_w_gate, moe_w_up, moe_w_down):
    B, T, D = x.shape
    depth = ada_w.shape[0]
    n_mixers = 4
    mod = _ada(c, ada_w, ada_b).reshape(depth, B, 6, D)
    xt = x.reshape(B * T, D)
    prev = None
    for i in range(depth):
        m, j = i % n_mixers, i // n_mixers
        g1n = norm_g[i, 0]
        if m == 0:
            x1 = _pool_mixer(xt, prev, mod[i], g1n, pool_w_in[j], pool_w_grp[j], pool_scale[j], pool_w_out[j], T)
        elif m == 1:
            x1 = _hgrn_mixer(i, xt, prev, mod[i], g1n, hgrn_w_in[j], hgrn_lb_logits, hgrn_norm_g[j],
                             hgrn_w_out[j], T)
        elif m == 2:
            x1 = _swa_mixer(xt, prev, mod[i], g1n, swa_w_in[j], swa_sinks[j], swa_w_out[j], T)
        else:
            x1 = _conv_mixer(xt, prev, mod[i], g1n, conv_w_in[j], conv_w[j], conv_w_out[j], T)
        ys, dest = _moe(i, x1, mod[i], norm_g[i, 1], router_w, router_bias, moe_w_gate, moe_w_up, moe_w_down, T)
        xt = x1
        prev = (ys, dest, mod[i])
    out = _final(xt, prev, final_norm_g, T)
    return out.reshape(B, T, D)
```

```python
import functools

import jax
import jax.numpy as jnp
import numpy as np
from jax import lax
from jax.experimental import pallas as pl
from jax.experimental.pallas import tpu as pltpu

F32 = jnp.float32
BF16 = jnp.bfloat16
EPS = 1e-6
LOG2E = 1.4426950408889634

POOL_WINDOWS = (2, 4, 8, 16)
POOL_HALO = 16
HGRN_HEAD_DIM = 128
HGRN_CHUNK = 64
SWA_HEADS = 16
SWA_KV_HEADS = 4
SWA_WINDOW = 128
CONV_WIDTH = 3
CONV_HALO = 8
N_EXPERTS = 16
N_GROUPS = 4
EXPERTS_PER_GROUP = 4
PAIRS = ((0, 1), (0, 2), (0, 3), (1, 2), (1, 3), (2, 3))
N_CLASSES = N_GROUPS * len(PAIRS)
CLASS_ROWS = 32
LANES = 128
FFN_BLOCK = 256
PERM_CHUNK = 2048
VMEM_LIMIT = 52 * 1024 * 1024


def _cparams():
    return pltpu.CompilerParams(dimension_semantics=("arbitrary",), vmem_limit_bytes=VMEM_LIMIT)


def _full(shape):
    nd = len(shape)
    return pl.BlockSpec(shape, lambda i, *_: (0,) * nd)


def _norm_mod(x, g, sc, sh):
    ms = jnp.mean(x * x, axis=-1, keepdims=True)
    return x * lax.rsqrt(ms + EPS) * (g * (1.0 + sc)) + sh


def _sigmoid(x):
    return 1.0 / (1.0 + jnp.exp(-x))


def _silu(x):
    return x * _sigmoid(x)


def _dot(a, b):
    return jnp.dot(a.astype(BF16), b.astype(BF16), preferred_element_type=F32)


def _dot_nt(a, b):
    return lax.dot_general(a.astype(BF16), b.astype(BF16), (((1,), (1,)), ((), ())), preferred_element_type=F32)


def _dot_tn(a, b):
    return lax.dot_general(a.astype(BF16), b.astype(BF16), (((0,), (0,)), ((), ())), preferred_element_type=F32)


def _tt_load(ref, rows, d):
    n = d // LANES
    return jnp.concatenate([ref[pl.ds(j, rows, stride=n), :] for j in range(n)], axis=1)


def _tt_store(ref, val):
    rows, d = val.shape
    n = d // LANES
    for j in range(n):
        ref[pl.ds(j, rows, stride=n), :] = val[:, j * LANES:(j + 1) * LANES]


def _cast_kernel(w_ref, o_ref):
    o_ref[...] = w_ref[...].astype(BF16)


def _cast_bf16(w, layer):
    _, E, K, M = w.shape
    return pl.pallas_call(
        _cast_kernel,
        grid=(E,),
        in_specs=[pl.BlockSpec((None, None, K, M), lambda e: (layer, e, 0, 0))],
        out_specs=pl.BlockSpec((None, K, M), lambda e: (e, 0, 0)),
        out_shape=jax.ShapeDtypeStruct((E, K, M), BF16),
        compiler_params=_cparams(),
        name="cast_bf16",
    )(w)


def _ada_kernel(c_ref, w_ref, b_ref, o_ref):
    cond = _silu(c_ref[...])
    o_ref[...] = jnp.dot(cond, w_ref[...], preferred_element_type=F32, precision=lax.Precision.HIGHEST) + b_ref[...]


def _ada(c, ada_w, ada_b):
    L, D, D6 = ada_w.shape
    B = c.shape[0]
    bn = D6 // 4
    return pl.pallas_call(
        _ada_kernel,
        grid=(L, D6 // bn),
        in_specs=[pl.BlockSpec((B, D), lambda l, j: (0, 0)),
                  pl.BlockSpec((None, D, bn), lambda l, j: (l, 0, j)),
                  pl.BlockSpec((None, 1, bn), lambda l, j: (l, 0, j))],
        out_specs=pl.BlockSpec((None, B, bn), lambda l, j: (l, 0, j)),
        out_shape=jax.ShapeDtypeStruct((L, B, D6), F32),
        compiler_params=pltpu.CompilerParams(dimension_semantics=("arbitrary", "arbitrary"),
                                             vmem_limit_bytes=VMEM_LIMIT),
        name="ada_mod",
    )(c, ada_w, ada_b.reshape(L, 1, D6))


def _mixer_call(body, name, tm, T, x, prev, mod, g, weights, scratch, smem=()):
    N, D = x.shape
    rpt = D // LANES
    row = pl.BlockSpec((tm, D), lambda i, *_: (i, 0))
    modspec = pl.BlockSpec((None, 6, D), lambda i, *_: ((i * tm) // T, 0, 0))
    args, specs, prefetch = [x], [row], []
    scratch = list(scratch)
    if prev is not None:
        prefetch = [prev[1]]
        args += [prev[0], prev[2]]
        specs += [pl.BlockSpec(memory_space=pl.ANY), modspec]
        scratch += [pltpu.VMEM((2, tm * rpt, LANES), F32), pltpu.SemaphoreType.DMA((2,))]
    args += [mod, g.reshape(1, D)]
    specs += [modspec, _full((1, D))]
    for w in weights:
        args.append(w)
        specs.append(_full(w.shape))
    for s in smem:
        args.append(s)
        specs.append(pl.BlockSpec(memory_space=pltpu.SMEM))
    return pl.pallas_call(
        functools.partial(body, prev is not None, tm, T),
        grid_spec=pltpu.PrefetchScalarGridSpec(
            num_scalar_prefetch=len(prefetch), grid=(N // tm,), in_specs=specs, out_specs=row,
            scratch_shapes=scratch),
        out_shape=jax.ShapeDtypeStruct((N, D), F32),
        compiler_params=_cparams(),
        name=name,
    )(*prefetch, *args)


def _gathered_rows(dest_ref, ys_ref, buf, sem, tm, d, burst=False):
    i = pl.program_id(0)
    last = pl.num_programs(0) - 1
    rpt = d // LANES

    def issue(tile, slot, r):
        src = _tok(ys_ref, dest_ref[tile * tm + r], rpt)
        pltpu.make_async_copy(src, _tok(buf.at[slot], r, rpt), sem.at[slot]).start()

    def wait(slot):
        pltpu.make_async_copy(ys_ref.at[pl.ds(0, tm * rpt), :], buf.at[slot], sem.at[slot]).wait()

    @pl.when(i == 0)
    def _():
        lax.fori_loop(0, tm, lambda r, c: (issue(0, 0, r), c)[1], 0, unroll=8)

    if burst:
        @pl.when(i < last)
        def _():
            lax.fori_loop(0, tm, lambda r, c: (issue(i + 1, (i + 1) % 2, r), c)[1], 0, unroll=8)

        wait(i % 2)
        return _tt_load(buf.at[i % 2], tm, d), None, None

    nxt = jnp.minimum(i + 1, last)

    def prefetch(k, n):
        for r in range(k * (tm // n), (k + 1) * (tm // n)):
            issue(nxt, (i + 1) % 2, r)

    def finish():
        @pl.when(i == last)
        def _():
            wait((i + 1) % 2)

    wait(i % 2)
    return _tt_load(buf.at[i % 2], tm, d), prefetch, finish


def _mixer_input(has_prev, tm, refs):
    if has_prev:
        dest_ref, x_ref, ys_ref, pm_ref, m_ref, g_ref = refs[:6]
        buf, sem = refs[-2:]
        x = x_ref[...]
        y, prefetch, finish = _gathered_rows(dest_ref, ys_ref, buf, sem, tm, x.shape[1])
        x = x + pm_ref[5:6, :] * y
        rest = refs[6:-2]
    else:
        x_ref, m_ref, g_ref = refs[:3]
        x = x_ref[...]
        rest = refs[3:]
        prefetch = lambda k, n: None
        finish = lambda: None
    return x, m_ref, g_ref, rest, prefetch, finish


def _pool_kernel(has_prev, tm, T, *refs):
    x, m_ref, g_ref, (win_ref, wgrp_ref, scale_ref, wout_ref, o_ref, tail_ref), prefetch, finish = \
        _mixer_input(has_prev, tm, refs)
    i = pl.program_id(0)
    start = (i * tm) % T
    n_pre = 2 * len(POOL_WINDOWS)
    h = _norm_mod(x, g_ref[...], m_ref[1:2, :], m_ref[0:1, :])
    u = _dot(h, win_ref[...])

    @pl.when(start == 0)
    def _():
        tail_ref[...] = jnp.zeros_like(tail_ref)

    pos = start + lax.broadcasted_iota(jnp.int32, (tm, 1), 0)
    C = u.shape[1] // len(POOL_WINDOWS)
    ys = []
    for gi, w in enumerate(POOL_WINDOWS):
        prefetch(2 * gi, n_pre)
        ug = u[:, gi * C:(gi + 1) * C]
        s = jnp.concatenate([tail_ref[:, gi * C:(gi + 1) * C], ug], axis=0)
        k = 1
        while k < w:
            s = s + pltpu.roll(s, k, 0)
            k *= 2
        cnt = jnp.minimum(pos + 1, w).astype(F32)
        pooled = s[POOL_HALO:] / cnt - ug
        prefetch(2 * gi + 1, n_pre)
        ys.append(_dot(pooled, wgrp_ref[gi]))
    tail_ref[...] = u[tm - POOL_HALO:, :]
    y = jnp.concatenate(ys, axis=1) * scale_ref[...]
    o_ref[...] = x + m_ref[2:3, :] * _dot(y, wout_ref[...])
    finish()


def _pool_mixer(x, prev, mod, g, w_in, w_grp, scale, w_out, T):
    D = x.shape[1]
    weights = [w_in.astype(BF16), w_grp.astype(BF16), scale.reshape(1, D), w_out.astype(BF16)]
    return _mixer_call(_pool_kernel, "mixer_pool", 512, T, x, prev, mod, g, weights,
                       [pltpu.VMEM((POOL_HALO, D), F32)])


def _conv_kernel(has_prev, tm, T, *refs):
    x, m_ref, g_ref, (win_ref, cw_ref, wout_ref, o_ref, tail_ref), prefetch, finish = \
        _mixer_input(has_prev, tm, refs)
    i = pl.program_id(0)
    D = x.shape[1]
    h = _norm_mod(x, g_ref[...], m_ref[1:2, :], m_ref[0:1, :]).astype(BF16)
    prefetch(0, 4)
    gate_b = _dot(h, win_ref[:, :D])
    prefetch(1, 4)
    z = _dot(h, win_ref[:, D:2 * D])
    prefetch(2, 4)
    z = z * _dot(h, win_ref[:, 2 * D:])
    prefetch(3, 4)

    @pl.when((i * tm) % T == 0)
    def _():
        tail_ref[...] = jnp.zeros_like(tail_ref)

    ze = jnp.concatenate([tail_ref[...], z], axis=0)
    zc = cw_ref[CONV_WIDTH - 1:CONV_WIDTH, :] * ze
    for j in range(1, CONV_WIDTH):
        zc = zc + cw_ref[CONV_WIDTH - 1 - j:CONV_WIDTH - j, :] * pltpu.roll(ze, j, 0)
    tail_ref[...] = z[tm - CONV_HALO:, :]
    y = gate_b * zc[CONV_HALO:]
    o_ref[...] = x + m_ref[2:3, :] * _dot(y, wout_ref[...])
    finish()


def _conv_mixer(x, prev, mod, g, w_in, conv_w, w_out, T):
    D = x.shape[1]
    weights = [w_in.astype(BF16), conv_w, w_out.astype(BF16)]
    return _mixer_call(_conv_kernel, "mixer_conv", 512, T, x, prev, mod, g, weights,
                       [pltpu.VMEM((CONV_HALO, D), F32)])


def _swa_kernel(has_prev, tm, T, *refs):
    x, m_ref, g_ref, (win_ref, wout_ref, sink_ref, o_ref, q_scr, klo, khi, vlo, vhi, o_scr), prefetch, finish = \
        _mixer_input(has_prev, tm, refs)
    i = pl.program_id(0)
    D = x.shape[1]
    W = SWA_WINDOW
    hd = D // SWA_HEADS
    G = SWA_HEADS // SWA_KV_HEADS
    kvd = SWA_KV_HEADS * hd
    assert 2 * hd == LANES
    seq_start = (i * tm) % T == 0
    kv_scr = (klo, khi, vlo, vhi)

    @pl.when(seq_start)
    def _():
        for r in kv_scr:
            r[0:W, :] = jnp.zeros((W, r.shape[1]), BF16)

    @pl.when(jnp.logical_not(seq_start))
    def _():
        for r in kv_scr:
            r[0:W, :] = r[tm:tm + W, :]

    h = _norm_mod(x, g_ref[...], m_ref[1:2, :], m_ref[0:1, :])
    qkv = _dot(h, win_ref[...])
    q_scr[...] = (qkv[:, :D] * (hd ** -0.5 * LOG2E)).astype(BF16)
    low = lax.broadcasted_iota(jnp.int32, (tm, LANES), 1) < hd
    for src, lo_ref, hi_ref in ((qkv[:, D:D + kvd], klo, khi), (qkv[:, D + kvd:], vlo, vhi)):
        for a in range(kvd // LANES):
            kg = src[:, a * LANES:(a + 1) * LANES]
            sw = pltpu.roll(kg, hd, 1)
            for par, (lo_v, hi_v) in enumerate(((kg, sw), (sw, kg))):
                c = (2 * a + par) * LANES
                lo_ref[W:, c:c + LANES] = jnp.where(low, lo_v, 0.0).astype(BF16)
                hi_ref[W:, c:c + LANES] = jnp.where(low, 0.0, hi_v).astype(BF16)
    qpos = lax.broadcasted_iota(jnp.int32, (W, 2 * W), 0)
    kj = lax.broadcasted_iota(jnp.int32, (W, 2 * W), 1)
    band = (kj > qpos) & (kj <= qpos + W)
    bias = jnp.where(band, 0.0, -jnp.inf)
    bias0 = jnp.where(band & (kj >= jnp.where(seq_start, W, 0)), 0.0, -jnp.inf)
    for j in range(tm // W):
        b = bias0 if j == 0 else bias
        prefetch(j, tm // W)
        for grp in range(SWA_HEADS // 2):
            hk = (2 * grp) // G
            qg = q_scr[j * W:(j + 1) * W, grp * LANES:(grp + 1) * LANES]
            pair = None
            for par, (k_ref, v_ref) in enumerate(((klo, vlo), (khi, vhi))):
                kk = k_ref[j * W:(j + 2) * W, hk * LANES:(hk + 1) * LANES]
                vv = v_ref[j * W:(j + 2) * W, hk * LANES:(hk + 1) * LANES]
                sink = sink_ref[2 * grp + par] * LOG2E
                s = lax.dot_general(qg, kk, (((1,), (1,)), ((), ())), preferred_element_type=F32) + b
                mx = jnp.maximum(jnp.max(s, axis=-1, keepdims=True), sink)
                p = jnp.exp2(s - mx)
                denom = jnp.sum(p, axis=-1, keepdims=True) + jnp.exp2(sink - mx)
                o = jnp.dot(p.astype(BF16), vv, preferred_element_type=F32) * (1.0 / denom)
                pair = o if pair is None else pair + o
            o_scr[j * W:(j + 1) * W, grp * LANES:(grp + 1) * LANES] = pair.astype(BF16)
    o_ref[...] = x + m_ref[2:3, :] * jnp.dot(o_scr[...], wout_ref[...], preferred_element_type=F32)
    finish()


def _swa_mixer(x, prev, mod, g, w_in, sinks, w_out, T):
    D = x.shape[1]
    tm = 256
    rows = SWA_WINDOW + tm
    kv = pltpu.VMEM((rows, SWA_KV_HEADS * LANES), BF16)
    weights = [w_in.astype(BF16), w_out.astype(BF16)]
    return _mixer_call(_swa_kernel, "mixer_swa", tm, T, x, prev, mod, g, weights,
                       [pltpu.VMEM((tm, D), BF16), kv, kv, kv, kv, pltpu.VMEM((tm, D), BF16)], smem=[sinks])


HGRN_LEVELS = tuple(2 ** e for e in range(HGRN_CHUNK.bit_length() - 1, 0, -1))


def _hgrn_tri():
    tri = np.tril(np.ones((HGRN_CHUNK, HGRN_CHUNK), np.float32))
    return np.concatenate([tri, tri, tri], axis=1)


def _split3(a):
    hi = a.astype(BF16)
    r1 = a - hi.astype(F32)
    mid = r1.astype(BF16)
    lo = (r1 - mid.astype(F32)).astype(BF16)
    return hi, mid, lo


def _hgrn_kernel(layer, has_prev, tm, T, *refs):
    x, m_ref, g_ref, (win_ref, lbl_ref, ng_ref, wout_ref, tri_ref, o_ref, st_ref), prefetch, finish = \
        _mixer_input(has_prev, tm, refs)
    i = pl.program_id(0)
    D = x.shape[1]
    dk = HGRN_HEAD_DIM
    C = HGRN_CHUNK

    @pl.when((i * tm) % T == 0)
    def _():
        st_ref[...] = jnp.zeros_like(st_ref)

    rows = [lbl_ref[j:j + 1, :] for j in range(lbl_ref.shape[0])]
    mx = functools.reduce(jnp.maximum, rows)
    es = [jnp.exp(r - mx) for r in rows]
    tot = functools.reduce(lambda a, b: a + b, es)
    lb = jnp.zeros_like(mx)
    for j in range(1, layer + 1):
        lb = lb + es[j] / tot

    h = _norm_mod(x, g_ref[...], m_ref[1:2, :], m_ref[0:1, :])
    proj = _dot(h, win_ref[...])
    row = lax.broadcasted_iota(jnp.int32, (C, C), 0)
    col = lax.broadcasted_iota(jnp.int32, (C, C), 1)
    pair_masks = [(row // B == col // B) & (row % B >= B // 2) & (col % B < B // 2) for B in HGRN_LEVELS]
    diag = row == col
    trow = lax.broadcasted_iota(jnp.int32, (tm, 1), 0)
    nc = tm // C
    n_units = (D // dk) * nc

    def block_row(a, B, r):
        a3 = a.reshape(tm // B, B, a.shape[1])
        return jnp.broadcast_to(a3[:, r:r + 1, :], a3.shape).reshape(a.shape)

    outs = []
    for hh in range(D // dk):
        sl = slice(hh * dk, (hh + 1) * dk)
        q = _silu(proj[:, sl])
        lbh = lb[:, sl]
        f = lbh + (1.0 - lbh) * _sigmoid(proj[:, D + hh * dk:D + (hh + 1) * dk])
        kk = 1.0 - f
        v = proj[:, 2 * D + hh * dk:2 * D + (hh + 1) * dk]
        gate = proj[:, 3 * D + hh * dk:3 * D + (hh + 1) * dk]
        lf = jnp.log2(f)
        parts = _split3(lf)
        rhs = jnp.concatenate(
            [jnp.concatenate([p[c * C:(c + 1) * C] for p in parts], axis=0) for c in range(nc)], axis=1)
        bb = jnp.dot(tri_ref[...], rhs, preferred_element_type=F32)
        b = jnp.concatenate([bb[:, c * dk:(c + 1) * dk] for c in range(nc)], axis=0)
        from_start = jnp.exp2(b)
        qs = (q * from_start).astype(BF16)
        ks = (kk * jnp.exp2(block_row(b, C, C - 1) - b)).astype(BF16)
        zs = []
        for B in HGRN_LEVELS:
            pos = trow % B
            if B >= 8:
                mid = block_row(b, B, B // 2 - 1)
                e = jnp.where(pos >= B // 2, b - mid, mid - b)
            elif B == 4:
                e = jnp.where(pos == 0, pltpu.roll(lf, tm - 1, 0),
                              jnp.where(pos == 1, 0.0, jnp.where(pos == 2, lf, lf + pltpu.roll(lf, 1, 0))))
            else:
                e = jnp.where(pos == 1, lf, 0.0)
            zs.append((jnp.where(pos >= B // 2, q, kk) * jnp.exp2(e)).astype(BF16))
        qk = jnp.sum(q * kk, axis=-1, keepdims=True)
        st = st_ref[hh]
        oc = []
        for c in range(nc):
            prefetch(hh * nc + c, n_units)
            rs = slice(c * C, (c + 1) * C)
            vc = v[rs].astype(BF16)
            sc = jnp.where(diag, qk[rs], 0.0)
            for z, mask in zip(zs, pair_masks):
                sc = sc + jnp.where(mask, _dot_nt(z[rs], z[rs]), 0.0)
            oc.append(_dot(sc, vc) + _dot_nt(qs[rs], st))
            st = st * from_start[(c + 1) * C - 1:(c + 1) * C, :] + _dot_tn(vc, ks[rs])
        st_ref[hh] = st
        o = jnp.concatenate(oc, axis=0)
        o = o * lax.rsqrt(jnp.mean(o * o, axis=-1, keepdims=True) + EPS) * ng_ref[...]
        outs.append(o * _silu(gate))
    y = jnp.concatenate(outs, axis=1)
    o_ref[...] = x + m_ref[2:3, :] * _dot(y, wout_ref[...])
    finish()


def _hgrn_mixer(layer, x, prev, mod, g, w_in, lb_logits, norm_g, w_out, T):
    D = x.shape[1]
    dk = HGRN_HEAD_DIM
    weights = [w_in.astype(BF16), lb_logits, norm_g.reshape(1, dk), w_out.astype(BF16),
               jnp.asarray(_hgrn_tri(), BF16)]
    return _mixer_call(functools.partial(_hgrn_kernel, layer), "mixer_hgrn", 256, T, x, prev, mod, g, weights,
                       [pltpu.VMEM((D // dk, dk, dk), F32)])


def _split2(a):
    hi = a.astype(BF16)
    lo = (a - hi.astype(F32)).astype(BF16)
    return hi, lo


def _route_kernel(tm, x_ref, m_ref, g_ref, rw_ref, rb_ref, tri_ref, h_ref, cls_ref, rank_ref, cnt_ref, carry_ref):
    i = pl.program_id(0)

    @pl.when(i == 0)
    def _():
        carry_ref[...] = jnp.zeros_like(carry_ref)

    h = _norm_mod(x_ref[...], g_ref[...], m_ref[4:5, :], m_ref[3:4, :])
    _tt_store(h_ref, h)
    h_hi, h_lo = _split2(h)
    w_hi, w_lo = _split2(rw_ref[...])
    logits = (jnp.dot(h_hi, w_hi, preferred_element_type=F32) + jnp.dot(h_lo, w_hi, preferred_element_type=F32)
              + jnp.dot(h_hi, w_lo, preferred_element_type=F32))
    lt = jnp.transpose(logits)[:N_EXPERTS, :]
    score = _sigmoid(lt)
    sel = score + rb_ref[...]
    gscore, gsel = [], []
    for gi in range(N_GROUPS):
        r = [sel[gi * 4 + e:gi * 4 + e + 1, :] for e in range(EXPERTS_PER_GROUP)]
        m1 = functools.reduce(jnp.maximum, r)
        m2 = None
        for a in range(EXPERTS_PER_GROUP):
            for b2 in range(a + 1, EXPERTS_PER_GROUP):
                pm = jnp.minimum(r[a], r[b2])
                m2 = pm if m2 is None else jnp.maximum(m2, pm)
        gscore.append(m1 + m2)
        gsel.append(r)
    best = jnp.zeros((1, tm), jnp.int32)
    bs = gscore[0]
    for gi in range(1, N_GROUPS):
        better = gscore[gi] > bs
        best = jnp.where(better, gi, best)
        bs = jnp.where(better, gscore[gi], bs)
    r = [functools.reduce(lambda a, b2: a + b2,
                          [jnp.where(best == gi, gsel[gi][e], 0.0) for gi in range(N_GROUPS)])
         for e in range(EXPERTS_PER_GROUP)]
    keep = []
    for e in range(EXPERTS_PER_GROUP):
        beaten = jnp.zeros((1, tm), jnp.int32)
        for o in range(EXPERTS_PER_GROUP):
            if o != e:
                wins = (r[o] > r[e]) | ((r[o] == r[e]) & (o < e))
                beaten = beaten + wins.astype(jnp.int32)
        keep.append(beaten < 2)
    pair = jnp.zeros((1, tm), jnp.int32)
    for pi, (a, b2) in enumerate(PAIRS):
        pair = jnp.where(keep[a] & keep[b2], pi, pair)
    cls = best * len(PAIRS) + pair
    cls_ref[...] = cls.reshape(1, 1, tm)
    onehot = (lax.broadcasted_iota(jnp.int32, (CLASS_ROWS, tm), 0) == cls).astype(F32)
    before = jnp.dot(onehot.astype(BF16), tri_ref[...], preferred_element_type=F32) + carry_ref[:, 0:1]
    rank = jnp.sum(onehot * before, axis=0, keepdims=True)
    rank_ref[...] = rank.astype(jnp.int32).reshape(1, 1, tm)
    carry_ref[...] = carry_ref[...] + jnp.sum(onehot, axis=1, keepdims=True)
    cnt_ref[...] = carry_ref[...]


def _route(x1, mod, g, router_w, router_bias, T):
    N, D = x1.shape
    tm = 512
    nt = N // tm
    E = router_w.shape[1]
    rw = jnp.zeros((D, LANES), F32).at[:, :E].set(router_w)
    rb = router_bias.reshape(E, 1)
    tri = (jnp.arange(tm)[:, None] < jnp.arange(tm)[None, :]).astype(BF16)
    row = pl.BlockSpec((tm, D), lambda i: (i, 0))
    modspec = pl.BlockSpec((None, 6, D), lambda i: ((i * tm) // T, 0, 0))
    tok = pl.BlockSpec((1, 1, tm), lambda i: (i, 0, 0))
    h2, cls, rank, cnt = pl.pallas_call(
        functools.partial(_route_kernel, tm),
        grid=(nt,),
        in_specs=[row, modspec, _full((1, D)), _full((D, LANES)), _full((E, 1)), _full((tm, tm))],
        out_specs=[pl.BlockSpec((tm * D // LANES, LANES), lambda i: (i, 0)), tok, tok, _full((CLASS_ROWS, LANES))],
        out_shape=[jax.ShapeDtypeStruct((N * D // LANES, LANES), F32),
                   jax.ShapeDtypeStruct((nt, 1, tm), jnp.int32),
                   jax.ShapeDtypeStruct((nt, 1, tm), jnp.int32),
                   jax.ShapeDtypeStruct((CLASS_ROWS, LANES), F32)],
        scratch_shapes=[pltpu.VMEM((CLASS_ROWS, LANES), F32)],
        compiler_params=_cparams(),
        name="router",
    )(x1, mod, g.reshape(1, D), rw, rb, tri)
    return h2, cls.reshape(N), rank.reshape(N), cnt[:N_CLASSES, 0]


def _tok(ref, t, rpt):
    return ref.at[pl.ds(pl.multiple_of(t * rpt, rpt), rpt), :]


def _scatter_rows_kernel(chunk, rpt, idx_ref, src_ref, init_ref, dst_ref, sem):
    del init_ref
    base = pl.program_id(0) * chunk

    def issue(r, carry):
        pltpu.make_async_copy(_tok(src_ref, r, rpt), _tok(dst_ref, idx_ref[base + r], rpt), sem).start()
        return carry

    lax.fori_loop(0, chunk, issue, 0, unroll=8)
    pltpu.make_async_copy(src_ref, dst_ref.at[pl.ds(0, chunk * rpt), :], sem).wait()


def _scatter_rows(src, idx, n_out, rpt):
    N = idx.shape[0]
    chunk = min(PERM_CHUNK, N)
    init = jnp.zeros((n_out * rpt, LANES), src.dtype)
    return pl.pallas_call(
        functools.partial(_scatter_rows_kernel, chunk, rpt),
        grid_spec=pltpu.PrefetchScalarGridSpec(
            num_scalar_prefetch=1, grid=(N // chunk,),
            in_specs=[pl.BlockSpec((chunk * rpt, LANES), lambda i, idx: (i, 0)), pl.BlockSpec(memory_space=pl.ANY)],
            out_specs=pl.BlockSpec(memory_space=pl.ANY),
            scratch_shapes=[pltpu.SemaphoreType.DMA(())]),
        out_shape=jax.ShapeDtypeStruct(init.shape, src.dtype),
        input_output_aliases={2: 0},
        compiler_params=_cparams(),
        name="scatter_rows",
    )(idx, src, init)


def _ffn_kernel(ea_ref, eb_ref, nact_ref, x_ref, rwt_ref, wga_ref, wua_ref, wda_ref, wgb_ref, wub_ref, wdb_ref,
                o_ref):
    j = pl.program_id(0)

    @pl.when(j < nact_ref[0])
    def _():
        x = _tt_load(x_ref, FFN_BLOCK, wga_ref.shape[0])
        sa = _sigmoid(jnp.sum(x * rwt_ref[pl.ds(ea_ref[j], 1), :], axis=-1, keepdims=True))
        sb = _sigmoid(jnp.sum(x * rwt_ref[pl.ds(eb_ref[j], 1), :], axis=-1, keepdims=True))
        inv = 1.0 / (sa + sb)
        xb = x.astype(BF16)
        aa = _silu(_dot(xb, wga_ref[...])) * _dot(xb, wua_ref[...]) * (sa * inv)
        ab = _silu(_dot(xb, wgb_ref[...])) * _dot(xb, wub_ref[...]) * (sb * inv)
        _tt_store(o_ref, _dot(aa, wda_ref[...]) + _dot(ab, wdb_ref[...]))

    @pl.when(j >= nact_ref[0])
    def _():
        o_ref[...] = jnp.zeros_like(o_ref)


def _ffn(xs, blk_ea, blk_eb, n_active, router_wt, w_gate, w_up, w_down):
    D, F = w_gate.shape[1:]
    rpt = D // LANES
    nb = xs.shape[0] // (FFN_BLOCK * rpt)

    def wa(j, ea, eb, na):
        return (ea[j], 0, 0)

    def wb(j, ea, eb, na):
        return (eb[j], 0, 0)

    row = pl.BlockSpec((FFN_BLOCK * rpt, LANES), lambda j, ea, eb, na: (j, 0))
    return pl.pallas_call(
        _ffn_kernel,
        grid_spec=pltpu.PrefetchScalarGridSpec(
            num_scalar_prefetch=3, grid=(nb,),
            in_specs=[row, pl.BlockSpec(router_wt.shape, lambda j, ea, eb, na: (0, 0)),
                      pl.BlockSpec((None, D, F), wa), pl.BlockSpec((None, D, F), wa), pl.BlockSpec((None, F, D), wa),
                      pl.BlockSpec((None, D, F), wb), pl.BlockSpec((None, D, F), wb), pl.BlockSpec((None, F, D), wb)],
            out_specs=row),
        out_shape=jax.ShapeDtypeStruct(xs.shape, F32),
        compiler_params=_cparams(),
        name="moe_ffn",
    )(blk_ea, blk_eb, n_active, xs, router_wt, w_gate, w_up, w_down, w_gate, w_up, w_down)


def _moe(layer, x1, mod, g, router_w, router_bias, w_gate, w_up, w_down, T):
    N, D = x1.shape
    h2, cls, rank, cnt = _route(x1, mod, g, router_w, router_bias, T)
    counts = cnt.astype(jnp.int32)
    padded = (counts + FFN_BLOCK - 1) // FFN_BLOCK * FFN_BLOCK
    ends = jnp.cumsum(padded)
    starts = ends - padded
    dest = starts[cls] + rank
    nb = N // FFN_BLOCK + N_CLASSES
    n_active = (ends[-1] // FFN_BLOCK).astype(jnp.int32)
    blk = jnp.arange(nb, dtype=jnp.int32)
    blk_start = jnp.minimum(blk, n_active - 1) * FFN_BLOCK
    blk_cls = jnp.sum((ends[None, :] <= blk_start[:, None]).astype(jnp.int32), axis=1)
    blk_cls = jnp.minimum(blk_cls, N_CLASSES - 1)
    pair_a = jnp.array([p[0] for p in PAIRS], jnp.int32)
    pair_b = jnp.array([p[1] for p in PAIRS], jnp.int32)
    grp = blk_cls // len(PAIRS)
    blk_ea = grp * EXPERTS_PER_GROUP + pair_a[blk_cls % len(PAIRS)]
    blk_eb = grp * EXPERTS_PER_GROUP + pair_b[blk_cls % len(PAIRS)]
    rpt = D // LANES
    xs = _scatter_rows(h2, dest, nb * FFN_BLOCK, rpt)
    ys = _ffn(xs, blk_ea, blk_eb, n_active.reshape(1), jnp.transpose(router_w),
              _cast_bf16(w_gate, layer), _cast_bf16(w_up, layer), _cast_bf16(w_down, layer))
    return ys, dest


def _final_kernel(tm, dest_ref, x_ref, ys_ref, pm_ref, g_ref, o_ref, buf, sem):
    x = x_ref[...]
    y, _, _ = _gathered_rows(dest_ref, ys_ref, buf, sem, tm, x.shape[1], burst=True)
    x = x + pm_ref[5:6, :] * y
    ms = jnp.mean(x * x, axis=-1, keepdims=True)
    o_ref[...] = x * lax.rsqrt(ms + EPS) * g_ref[...]


def _final(x, prev, g, T):
    ys, dest, mod = prev
    N, D = x.shape
    tm = min(512, T)
    row = pl.BlockSpec((tm, D), lambda i, *_: (i, 0))
    modspec = pl.BlockSpec((None, 6, D), lambda i, *_: ((i * tm) // T, 0, 0))
    return pl.pallas_call(
        functools.partial(_final_kernel, tm),
        grid_spec=pltpu.PrefetchScalarGridSpec(
            num_scalar_prefetch=1, grid=(N // tm,),
            in_specs=[row, pl.BlockSpec(memory_space=pl.ANY), modspec, _full((1, D))],
            out_specs=row,
            scratch_shapes=[pltpu.VMEM((2, tm * D // LANES, LANES), F32), pltpu.SemaphoreType.DMA((2,))]),
        out_shape=jax.ShapeDtypeStruct((N, D), F32),
        compiler_params=_cparams(),
        name="final_norm",
    )(dest, x, ys, mod, g.reshape(1, D))


def kernel(x, c, ada_w, ada_b, norm_g, final_norm_g, pool_w_in, pool_w_grp, pool_scale, pool_w_out, hgrn_w_in, hgrn_lb_logits, hgrn_norm_g, hgrn_w_out, swa_w_in, swa_sinks, swa_w_out, conv_w_in, conv_w, conv_w_out, router_w, router_bias, moe_w_gate, moe_w_up, moe_w_down):
    B, T, D = x.shape
    depth = ada_w.shape[0]
    n_mixers = 4
    mod = _ada(c, ada_w, ada_b).reshape(depth, B, 6, D)
    xt = x.reshape(B * T, D)
    prev = None
    for i in range(depth):
        m, j = i % n_mixers, i // n_mixers
        g1n = norm_g[i, 0]
        if m == 0:
            x1 = _pool_mixer(xt, prev, mod[i], g1n, pool_w_in[j], pool_w_grp[j], pool_scale[j], pool_w_out[j], T)
        elif m == 1:
            x1 = _hgrn_mixer(i, xt, prev, mod[i], g1n, hgrn_w_in[j], hgrn_lb_logits, hgrn_norm_g[j],
                             hgrn_w_out[j], T)
        elif m == 2:
            x1 = _swa_mixer(xt, prev, mod[i], g1n, swa_w_in[j], swa_sinks[j], swa_w_out[j], T)
        else:
            x1 = _conv_mixer(xt, prev, mod[i], g1n, conv_w_in[j], conv_w[j], conv_w_out[j], T)
        ys, dest = _moe(i, x1, mod[i], norm_g[i, 1], router_w, router_bias, moe_w_gate, moe_w_up, moe_w_down, T)
        xt = x1
        prev = (ys, dest, mod[i])
    out = _final(xt, prev, final_norm_g, T)
    return out.reshape(B, T, D)
```

```python
import functools

import jax
import jax.numpy as jnp
import numpy as np
from jax import lax
from jax.experimental import pallas as pl
from jax.experimental.pallas import tpu as pltpu

F32 = jnp.float32
BF16 = jnp.bfloat16
EPS = 1e-6
LOG2E = 1.4426950408889634

POOL_WINDOWS = (2, 4, 8, 16)
POOL_HALO = 16
HGRN_HEAD_DIM = 128
HGRN_CHUNK = 64
SWA_HEADS = 16
SWA_KV_HEADS = 4
SWA_WINDOW = 128
CONV_WIDTH = 3
CONV_HALO = 8
N_EXPERTS = 16
N_GROUPS = 4
EXPERTS_PER_GROUP = 4
PAIRS = ((0, 1), (0, 2), (0, 3), (1, 2), (1, 3), (2, 3))
N_CLASSES = N_GROUPS * len(PAIRS)
CLASS_ROWS = 32
LANES = 128
FFN_BLOCK = 256
PERM_CHUNK = 2048
VMEM_LIMIT = 52 * 1024 * 1024


def _cparams():
    return pltpu.CompilerParams(dimension_semantics=("arbitrary",), vmem_limit_bytes=VMEM_LIMIT)


def _full(shape):
    nd = len(shape)
    return pl.BlockSpec(shape, lambda i, *_: (0,) * nd)


def _norm_mod(x, g, sc, sh):
    ms = jnp.mean(x * x, axis=-1, keepdims=True)
    return x * lax.rsqrt(ms + EPS) * (g * (1.0 + sc)) + sh


def _sigmoid(x):
    return 1.0 / (1.0 + jnp.exp(-x))


def _silu(x):
    return x * _sigmoid(x)


def _dot(a, b):
    return jnp.dot(a.astype(BF16), b.astype(BF16), preferred_element_type=F32)


def _dot_nt(a, b):
    return lax.dot_general(a.astype(BF16), b.astype(BF16), (((1,), (1,)), ((), ())), preferred_element_type=F32)


def _dot_tn(a, b):
    return lax.dot_general(a.astype(BF16), b.astype(BF16), (((0,), (0,)), ((), ())), preferred_element_type=F32)


def _tt_load(ref, rows, d):
    n = d // LANES
    return jnp.concatenate([ref[pl.ds(j, rows, stride=n), :] for j in range(n)], axis=1)


def _tt_store(ref, val):
    rows, d = val.shape
    n = d // LANES
    for j in range(n):
        ref[pl.ds(j, rows, stride=n), :] = val[:, j * LANES:(j + 1) * LANES]


def _cast_kernel(w_ref, o_ref):
    o_ref[...] = w_ref[...].astype(BF16)


def _cast_bf16(w, layer):
    _, E, K, M = w.shape
    return pl.pallas_call(
        _cast_kernel,
        grid=(E,),
        in_specs=[pl.BlockSpec((None, None, K, M), lambda e: (layer, e, 0, 0))],
        out_specs=pl.BlockSpec((None, K, M), lambda e: (e, 0, 0)),
        out_shape=jax.ShapeDtypeStruct((E, K, M), BF16),
        compiler_params=_cparams(),
        name="cast_bf16",
    )(w)


def _ada_kernel(c_ref, w_ref, b_ref, o_ref):
    cond = _silu(c_ref[...])
    o_ref[...] = jnp.dot(cond, w_ref[...], preferred_element_type=F32, precision=lax.Precision.HIGHEST) + b_ref[...]


def _ada(c, ada_w, ada_b):
    L, D, D6 = ada_w.shape
    B = c.shape[0]
    bn = D6 // 4
    return pl.pallas_call(
        _ada_kernel,
        grid=(L, D6 // bn),
        in_specs=[pl.BlockSpec((B, D), lambda l, j: (0, 0)),
                  pl.BlockSpec((None, D, bn), lambda l, j: (l, 0, j)),
                  pl.BlockSpec((None, 1, bn), lambda l, j: (l, 0, j))],
        out_specs=pl.BlockSpec((None, B, bn), lambda l, j: (l, 0, j)),
        out_shape=jax.ShapeDtypeStruct((L, B, D6), F32),
        compiler_params=pltpu.CompilerParams(dimension_semantics=("arbitrary", "arbitrary"),
                                             vmem_limit_bytes=VMEM_LIMIT),
        name="ada_mod",
    )(c, ada_w, ada_b.reshape(L, 1, D6))


def _mixer_call(body, name, tm, T, x, prev, mod, g, weights, scratch, smem=()):
    N, D = x.shape
    rpt = D // LANES
    row = pl.BlockSpec((tm, D), lambda i, *_: (i, 0))
    modspec = pl.BlockSpec((None, 6, D), lambda i, *_: ((i * tm) // T, 0, 0))
    args, specs, prefetch = [x], [row], []
    scratch = list(scratch)
    if prev is not None:
        prefetch = [prev[1]]
        args += [prev[0], prev[2]]
        specs += [pl.BlockSpec(memory_space=pl.ANY), modspec]
        scratch += [pltpu.VMEM((2, tm * rpt, LANES), F32), pltpu.SemaphoreType.DMA((2,))]
    args += [mod, g.reshape(1, D)]
    specs += [modspec, _full((1, D))]
    for w in weights:
        args.append(w)
        specs.append(_full(w.shape))
    for s in smem:
        args.append(s)
        specs.append(pl.BlockSpec(memory_space=pltpu.SMEM))
    return pl.pallas_call(
        functools.partial(body, prev is not None, tm, T),
        grid_spec=pltpu.PrefetchScalarGridSpec(
            num_scalar_prefetch=len(prefetch), grid=(N // tm,), in_specs=specs, out_specs=row,
            scratch_shapes=scratch),
        out_shape=jax.ShapeDtypeStruct((N, D), F32),
        compiler_params=_cparams(),
        name=name,
    )(*prefetch, *args)


def _gathered_rows(dest_ref, ys_ref, buf, sem, tm, d, burst=False):
    i = pl.program_id(0)
    last = pl.num_programs(0) - 1
    rpt = d // LANES

    def issue(tile, slot, r):
        src = _tok(ys_ref, dest_ref[tile * tm + r], rpt)
        pltpu.make_async_copy(src, _tok(buf.at[slot], r, rpt), sem.at[slot]).start()

    def wait(slot):
        pltpu.make_async_copy(ys_ref.at[pl.ds(0, tm * rpt), :], buf.at[slot], sem.at[slot]).wait()

    @pl.when(i == 0)
    def _():
        lax.fori_loop(0, tm, lambda r, c: (issue(0, 0, r), c)[1], 0, unroll=8)

    if burst:
        @pl.when(i < last)
        def _():
            lax.fori_loop(0, tm, lambda r, c: (issue(i + 1, (i + 1) % 2, r), c)[1], 0, unroll=8)

        wait(i % 2)
        return _tt_load(buf.at[i % 2], tm, d), (lambda k, n: None), (lambda: None)

    nxt = jnp.minimum(i + 1, last)

    def prefetch(k, n):
        for r in range(k * (tm // n), (k + 1) * (tm // n)):
            issue(nxt, (i + 1) % 2, r)

    def finish():
        @pl.when(i == last)
        def _():
            wait((i + 1) % 2)

    wait(i % 2)
    return _tt_load(buf.at[i % 2], tm, d), prefetch, finish


def _mixer_input(has_prev, tm, refs, burst=False):
    if has_prev:
        dest_ref, x_ref, ys_ref, pm_ref, m_ref, g_ref = refs[:6]
        buf, sem = refs[-2:]
        x = x_ref[...]
        y, prefetch, finish = _gathered_rows(dest_ref, ys_ref, buf, sem, tm, x.shape[1], burst)
        x = x + pm_ref[5:6, :] * y
        rest = refs[6:-2]
    else:
        x_ref, m_ref, g_ref = refs[:3]
        x = x_ref[...]
        rest = refs[3:]
        prefetch = lambda k, n: None
        finish = lambda: None
    return x, m_ref, g_ref, rest, prefetch, finish


def _pool_kernel(has_prev, tm, T, *refs):
    x, m_ref, g_ref, (win_ref, wgrp_ref, scale_ref, wout_ref, o_ref, tail_ref), prefetch, finish = \
        _mixer_input(has_prev, tm, refs)
    i = pl.program_id(0)
    start = (i * tm) % T
    n_pre = 2 * len(POOL_WINDOWS)
    h = _norm_mod(x, g_ref[...], m_ref[1:2, :], m_ref[0:1, :])
    u = _dot(h, win_ref[...])

    @pl.when(start == 0)
    def _():
        tail_ref[...] = jnp.zeros_like(tail_ref)

    pos = start + lax.broadcasted_iota(jnp.int32, (tm, 1), 0)
    C = u.shape[1] // len(POOL_WINDOWS)
    ys = []
    for gi, w in enumerate(POOL_WINDOWS):
        prefetch(2 * gi, n_pre)
        ug = u[:, gi * C:(gi + 1) * C]
        s = jnp.concatenate([tail_ref[:, gi * C:(gi + 1) * C], ug], axis=0)
        k = 1
        while k < w:
            s = s + pltpu.roll(s, k, 0)
            k *= 2
        cnt = jnp.minimum(pos + 1, w).astype(F32)
        pooled = s[POOL_HALO:] / cnt - ug
        prefetch(2 * gi + 1, n_pre)
        ys.append(_dot(pooled, wgrp_ref[gi]))
    tail_ref[...] = u[tm - POOL_HALO:, :]
    y = jnp.concatenate(ys, axis=1) * scale_ref[...]
    o_ref[...] = x + m_ref[2:3, :] * _dot(y, wout_ref[...])
    finish()


def _pool_mixer(x, prev, mod, g, w_in, w_grp, scale, w_out, T):
    D = x.shape[1]
    weights = [w_in.astype(BF16), w_grp.astype(BF16), scale.reshape(1, D), w_out.astype(BF16)]
    return _mixer_call(_pool_kernel, "mixer_pool", 512, T, x, prev, mod, g, weights,
                       [pltpu.VMEM((POOL_HALO, D), F32)])


def _conv_kernel(has_prev, tm, T, *refs):
    x, m_ref, g_ref, (win_ref, cw_ref, wout_ref, o_ref, tail_ref), prefetch, finish = \
        _mixer_input(has_prev, tm, refs)
    i = pl.program_id(0)
    D = x.shape[1]
    h = _norm_mod(x, g_ref[...], m_ref[1:2, :], m_ref[0:1, :]).astype(BF16)
    prefetch(0, 4)
    gate_b = _dot(h, win_ref[:, :D])
    prefetch(1, 4)
    z = _dot(h, win_ref[:, D:2 * D])
    prefetch(2, 4)
    z = z * _dot(h, win_ref[:, 2 * D:])
    prefetch(3, 4)

    @pl.when((i * tm) % T == 0)
    def _():
        tail_ref[...] = jnp.zeros_like(tail_ref)

    ze = jnp.concatenate([tail_ref[...], z], axis=0)
    zc = cw_ref[CONV_WIDTH - 1:CONV_WIDTH, :] * ze
    for j in range(1, CONV_WIDTH):
        zc = zc + cw_ref[CONV_WIDTH - 1 - j:CONV_WIDTH - j, :] * pltpu.roll(ze, j, 0)
    tail_ref[...] = z[tm - CONV_HALO:, :]
    y = gate_b * zc[CONV_HALO:]
    o_ref[...] = x + m_ref[2:3, :] * _dot(y, wout_ref[...])
    finish()


def _conv_mixer(x, prev, mod, g, w_in, conv_w, w_out, T):
    D = x.shape[1]
    weights = [w_in.astype(BF16), conv_w, w_out.astype(BF16)]
    return _mixer_call(_conv_kernel, "mixer_conv", 512, T, x, prev, mod, g, weights,
                       [pltpu.VMEM((CONV_HALO, D), F32)])


def _swa_kernel(has_prev, tm, T, *refs):
    x, m_ref, g_ref, (win_ref, wout_ref, sink_ref, o_ref, q_scr, klo, khi, vlo, vhi, o_scr), _, _ = \
        _mixer_input(has_prev, tm, refs, burst=True)
    i = pl.program_id(0)
    D = x.shape[1]
    W = SWA_WINDOW
    hd = D // SWA_HEADS
    G = SWA_HEADS // SWA_KV_HEADS
    kvd = SWA_KV_HEADS * hd
    assert 2 * hd == LANES
    seq_start = (i * tm) % T == 0
    kv_scr = (klo, khi, vlo, vhi)

    @pl.when(seq_start)
    def _():
        for r in kv_scr:
            r[0:W, :] = jnp.zeros((W, r.shape[1]), BF16)

    @pl.when(jnp.logical_not(seq_start))
    def _():
        for r in kv_scr:
            r[0:W, :] = r[tm:tm + W, :]

    h = _norm_mod(x, g_ref[...], m_ref[1:2, :], m_ref[0:1, :])
    qkv = _dot(h, win_ref[...])
    q_scr[...] = (qkv[:, :D] * (hd ** -0.5 * LOG2E)).astype(BF16)
    low = lax.broadcasted_iota(jnp.int32, (tm, LANES), 1) < hd
    for src, lo_ref, hi_ref in ((qkv[:, D:D + kvd], klo, khi), (qkv[:, D + kvd:], vlo, vhi)):
        for a in range(kvd // LANES):
            kg = src[:, a * LANES:(a + 1) * LANES]
            sw = pltpu.roll(kg, hd, 1)
            for par, (lo_v, hi_v) in enumerate(((kg, sw), (sw, kg))):
                c = (2 * a + par) * LANES
                lo_ref[W:, c:c + LANES] = jnp.where(low, lo_v, 0.0).astype(BF16)
                hi_ref[W:, c:c + LANES] = jnp.where(low, 0.0, hi_v).astype(BF16)
    qpos = lax.broadcasted_iota(jnp.int32, (W, 2 * W), 0)
    kj = lax.broadcasted_iota(jnp.int32, (W, 2 * W), 1)
    band = (kj > qpos) & (kj <= qpos + W)
    bias = jnp.where(band, 0.0, -jnp.inf)
    bias0 = jnp.where(band & (kj >= jnp.where(seq_start, W, 0)), 0.0, -jnp.inf)
    for j in range(tm // W):
        b = bias0 if j == 0 else bias
        for grp in range(SWA_HEADS // 2):
            hk = (2 * grp) // G
            qg = q_scr[j * W:(j + 1) * W, grp * LANES:(grp + 1) * LANES]
            pair = None
            for par, (k_ref, v_ref) in enumerate(((klo, vlo), (khi, vhi))):
                kk = k_ref[j * W:(j + 2) * W, hk * LANES:(hk + 1) * LANES]
                vv = v_ref[j * W:(j + 2) * W, hk * LANES:(hk + 1) * LANES]
                sink = sink_ref[2 * grp + par] * LOG2E
                s = lax.dot_general(qg, kk, (((1,), (1,)), ((), ())), preferred_element_type=F32) + b
                mx = jnp.maximum(jnp.max(s, axis=-1, keepdims=True), sink)
                p = jnp.exp2(s - mx)
                denom = jnp.sum(p, axis=-1, keepdims=True) + jnp.exp2(sink - mx)
                o = jnp.dot(p.astype(BF16), vv, preferred_element_type=F32) * (1.0 / denom)
                pair = o if pair is None else pair + o
            o_scr[j * W:(j + 1) * W, grp * LANES:(grp + 1) * LANES] = pair.astype(BF16)
    o_ref[...] = x + m_ref[2:3, :] * jnp.dot(o_scr[...], wout_ref[...], preferred_element_type=F32)


def _swa_mixer(x, prev, mod, g, w_in, sinks, w_out, T):
    D = x.shape[1]
    tm = 256
    rows = SWA_WINDOW + tm
    kv = pltpu.VMEM((rows, SWA_KV_HEADS * LANES), BF16)
    weights = [w_in.astype(BF16), w_out.astype(BF16)]
    return _mixer_call(_swa_kernel, "mixer_swa", tm, T, x, prev, mod, g, weights,
                       [pltpu.VMEM((tm, D), BF16), kv, kv, kv, kv, pltpu.VMEM((tm, D), BF16)], smem=[sinks])


HGRN_LEVELS = tuple(2 ** e for e in range(HGRN_CHUNK.bit_length() - 1, 0, -1))


def _hgrn_tri():
    tri = np.tril(np.ones((HGRN_CHUNK, HGRN_CHUNK), np.float32))
    return np.concatenate([tri, tri, tri], axis=1)


def _split3(a):
    hi = a.astype(BF16)
    r1 = a - hi.astype(F32)
    mid = r1.astype(BF16)
    lo = (r1 - mid.astype(F32)).astype(BF16)
    return hi, mid, lo


def _hgrn_kernel(layer, has_prev, tm, T, *refs):
    x, m_ref, g_ref, (win_ref, lbl_ref, ng_ref, wout_ref, tri_ref, o_ref, st_ref), prefetch, finish = \
        _mixer_input(has_prev, tm, refs)
    i = pl.program_id(0)
    D = x.shape[1]
    dk = HGRN_HEAD_DIM
    C = HGRN_CHUNK

    @pl.when((i * tm) % T == 0)
    def _():
        st_ref[...] = jnp.zeros_like(st_ref)

    rows = [lbl_ref[j:j + 1, :] for j in range(lbl_ref.shape[0])]
    mx = functools.reduce(jnp.maximum, rows)
    es = [jnp.exp(r - mx) for r in rows]
    tot = functools.reduce(lambda a, b: a + b, es)
    lb = jnp.zeros_like(mx)
    for j in range(1, layer + 1):
        lb = lb + es[j] / tot

    h = _norm_mod(x, g_ref[...], m_ref[1:2, :], m_ref[0:1, :])
    proj = _dot(h, win_ref[...])
    row = lax.broadcasted_iota(jnp.int32, (C, C), 0)
    col = lax.broadcasted_iota(jnp.int32, (C, C), 1)
    pair_masks = [(row // B == col // B) & (row % B >= B // 2) & (col % B < B // 2) for B in HGRN_LEVELS]
    diag = row == col
    trow = lax.broadcasted_iota(jnp.int32, (tm, 1), 0)
    nc = tm // C
    n_units = (D // dk) * nc

    def block_row(a, B, r):
        a3 = a.reshape(tm // B, B, a.shape[1])
        return jnp.broadcast_to(a3[:, r:r + 1, :], a3.shape).reshape(a.shape)

    outs = []
    for hh in range(D // dk):
        sl = slice(hh * dk, (hh + 1) * dk)
        q = _silu(proj[:, sl])
        lbh = lb[:, sl]
        f = lbh + (1.0 - lbh) * _sigmoid(proj[:, D + hh * dk:D + (hh + 1) * dk])
        kk = 1.0 - f
        v = proj[:, 2 * D + hh * dk:2 * D + (hh + 1) * dk]
        gate = proj[:, 3 * D + hh * dk:3 * D + (hh + 1) * dk]
        lf = jnp.log2(f)
        parts = _split3(lf)
        rhs = jnp.concatenate(
            [jnp.concatenate([p[c * C:(c + 1) * C] for p in parts], axis=0) for c in range(nc)], axis=1)
        bb = jnp.dot(tri_ref[...], rhs, preferred_element_type=F32)
        b = jnp.concatenate([bb[:, c * dk:(c + 1) * dk] for c in range(nc)], axis=0)
        from_start = jnp.exp2(b)
        qs = (q * from_start).astype(BF16)
        ks = (kk * jnp.exp2(block_row(b, C, C - 1) - b)).astype(BF16)
        zs = []
        for B in HGRN_LEVELS:
            pos = trow % B
            if B >= 8:
                mid = block_row(b, B, B // 2 - 1)
                e = jnp.where(pos >= B // 2, b - mid, mid - b)
            elif B == 4:
                e = jnp.where(pos == 0, pltpu.roll(lf, tm - 1, 0),
                              jnp.where(pos == 1, 0.0, jnp.where(pos == 2, lf, lf + pltpu.roll(lf, 1, 0))))
            else:
                e = jnp.where(pos == 1, lf, 0.0)
            zs.append((jnp.where(pos >= B // 2, q, kk) * jnp.exp2(e)).astype(BF16))
        qk = jnp.sum(q * kk, axis=-1, keepdims=True)
        st = st_ref[hh]
        oc = []
        for c in range(nc):
            prefetch(hh * nc + c, n_units)
            rs = slice(c * C, (c + 1) * C)
            vc = v[rs].astype(BF16)
            sc = jnp.where(diag, qk[rs], 0.0)
            for z, mask in zip(zs, pair_masks):
                sc = sc + jnp.where(mask, _dot_nt(z[rs], z[rs]), 0.0)
            oc.append(_dot(sc, vc) + _dot_nt(qs[rs], st))
            st = st * from_start[(c + 1) * C - 1:(c + 1) * C, :] + _dot_tn(vc, ks[rs])
        st_ref[hh] = st
        o = jnp.concatenate(oc, axis=0)
        o = o * lax.rsqrt(jnp.mean(o * o, axis=-1, keepdims=True) + EPS) * ng_ref[...]
        outs.append(o * _silu(gate))
    y = jnp.concatenate(outs, axis=1)
    o_ref[...] = x + m_ref[2:3, :] * _dot(y, wout_ref[...])
    finish()


def _hgrn_mixer(layer, x, prev, mod, g, w_in, lb_logits, norm_g, w_out, T):
    D = x.shape[1]
    dk = HGRN_HEAD_DIM
    weights = [w_in.astype(BF16), lb_logits, norm_g.reshape(1, dk), w_out.astype(BF16),
               jnp.asarray(_hgrn_tri(), BF16)]
    return _mixer_call(functools.partial(_hgrn_kernel, layer), "mixer_hgrn", 256, T, x, prev, mod, g, weights,
                       [pltpu.VMEM((D // dk, dk, dk), F32)])


def _split2(a):
    hi = a.astype(BF16)
    lo = (a - hi.astype(F32)).astype(BF16)
    return hi, lo


def _route_kernel(tm, x_ref, m_ref, g_ref, rw_ref, rb_ref, tri_ref, h_ref, cls_ref, rank_ref, cnt_ref, carry_ref):
    i = pl.program_id(0)

    @pl.when(i == 0)
    def _():
        carry_ref[...] = jnp.zeros_like(carry_ref)

    h = _norm_mod(x_ref[...], g_ref[...], m_ref[4:5, :], m_ref[3:4, :])
    _tt_store(h_ref, h)
    h_hi, h_lo = _split2(h)
    w_hi, w_lo = _split2(rw_ref[...])
    logits = (jnp.dot(h_hi, w_hi, preferred_element_type=F32) + jnp.dot(h_lo, w_hi, preferred_element_type=F32)
              + jnp.dot(h_hi, w_lo, preferred_element_type=F32))
    lt = jnp.transpose(logits)[:N_EXPERTS, :]
    score = _sigmoid(lt)
    sel = score + rb_ref[...]
    gscore, gsel = [], []
    for gi in range(N_GROUPS):
        r = [sel[gi * 4 + e:gi * 4 + e + 1, :] for e in range(EXPERTS_PER_GROUP)]
        m1 = functools.reduce(jnp.maximum, r)
        m2 = None
        for a in range(EXPERTS_PER_GROUP):
            for b2 in range(a + 1, EXPERTS_PER_GROUP):
                pm = jnp.minimum(r[a], r[b2])
                m2 = pm if m2 is None else jnp.maximum(m2, pm)
        gscore.append(m1 + m2)
        gsel.append(r)
    best = jnp.zeros((1, tm), jnp.int32)
    bs = gscore[0]
    for gi in range(1, N_GROUPS):
        better = gscore[gi] > bs
        best = jnp.where(better, gi, best)
        bs = jnp.where(better, gscore[gi], bs)
    r = [functools.reduce(lambda a, b2: a + b2,
                          [jnp.where(best == gi, gsel[gi][e], 0.0) for gi in range(N_GROUPS)])
         for e in range(EXPERTS_PER_GROUP)]
    keep = []
    for e in range(EXPERTS_PER_GROUP):
        beaten = jnp.zeros((1, tm), jnp.int32)
        for o in range(EXPERTS_PER_GROUP):
            if o != e:
                wins = (r[o] > r[e]) | ((r[o] == r[e]) & (o < e))
                beaten = beaten + wins.astype(jnp.int32)
        keep.append(beaten < 2)
    pair = jnp.zeros((1, tm), jnp.int32)
    for pi, (a, b2) in enumerate(PAIRS):
        pair = jnp.where(keep[a] & keep[b2], pi, pair)
    cls = best * len(PAIRS) + pair
    cls_ref[...] = cls.reshape(1, 1, tm)
    onehot = (lax.broadcasted_iota(jnp.int32, (CLASS_ROWS, tm), 0) == cls).astype(F32)
    before = jnp.dot(onehot.astype(BF16), tri_ref[...], preferred_element_type=F32) + carry_ref[:, 0:1]
    rank = jnp.sum(onehot * before, axis=0, keepdims=True)
    rank_ref[...] = rank.astype(jnp.int32).reshape(1, 1, tm)
    carry_ref[...] = carry_ref[...] + jnp.sum(onehot, axis=1, keepdims=True)
    cnt_ref[...] = carry_ref[...]


def _route(x1, mod, g, router_w, router_bias, T):
    N, D = x1.shape
    tm = 512
    nt = N // tm
    E = router_w.shape[1]
    rw = jnp.zeros((D, LANES), F32).at[:, :E].set(router_w)
    rb = router_bias.reshape(E, 1)
    tri = (jnp.arange(tm)[:, None] < jnp.arange(tm)[None, :]).astype(BF16)
    row = pl.BlockSpec((tm, D), lambda i: (i, 0))
    modspec = pl.BlockSpec((None, 6, D), lambda i: ((i * tm) // T, 0, 0))
    tok = pl.BlockSpec((1, 1, tm), lambda i: (i, 0, 0))
    h2, cls, rank, cnt = pl.pallas_call(
        functools.partial(_route_kernel, tm),
        grid=(nt,),
        in_specs=[row, modspec, _full((1, D)), _full((D, LANES)), _full((E, 1)), _full((tm, tm))],
        out_specs=[pl.BlockSpec((tm * D // LANES, LANES), lambda i: (i, 0)), tok, tok, _full((CLASS_ROWS, LANES))],
        out_shape=[jax.ShapeDtypeStruct((N * D // LANES, LANES), F32),
                   jax.ShapeDtypeStruct((nt, 1, tm), jnp.int32),
                   jax.ShapeDtypeStruct((nt, 1, tm), jnp.int32),
                   jax.ShapeDtypeStruct((CLASS_ROWS, LANES), F32)],
        scratch_shapes=[pltpu.VMEM((CLASS_ROWS, LANES), F32)],
        compiler_params=_cparams(),
        name="router",
    )(x1, mod, g.reshape(1, D), rw, rb, tri)
    return h2, cls.reshape(N), rank.reshape(N), cnt[:N_CLASSES, 0]


def _tok(ref, t, rpt):
    return ref.at[pl.ds(pl.multiple_of(t * rpt, rpt), rpt), :]


def _scatter_rows_kernel(chunk, rpt, idx_ref, src_ref, init_ref, dst_ref, sem):
    del init_ref
    base = pl.program_id(0) * chunk

    def issue(r2, carry):
        for par in range(2):
            r = 2 * r2 + par
            dst = _tok(dst_ref, idx_ref[base + r], rpt)
            pltpu.make_async_copy(_tok(src_ref, r, rpt), dst, sem).start(priority=par)
        return carry

    lax.fori_loop(0, chunk // 2, issue, 0, unroll=4)
    pltpu.make_async_copy(src_ref, dst_ref.at[pl.ds(0, chunk * rpt), :], sem).wait()


def _scatter_rows(src, idx, n_out, rpt, init=None):
    N = idx.shape[0]
    chunk = min(PERM_CHUNK, N)
    if init is None:
        init = jnp.zeros((n_out * rpt, LANES), src.dtype)
    return pl.pallas_call(
        functools.partial(_scatter_rows_kernel, chunk, rpt),
        grid_spec=pltpu.PrefetchScalarGridSpec(
            num_scalar_prefetch=1, grid=(N // chunk,),
            in_specs=[pl.BlockSpec((chunk * rpt, LANES), lambda i, idx: (i, 0)), pl.BlockSpec(memory_space=pl.ANY)],
            out_specs=pl.BlockSpec(memory_space=pl.ANY),
            scratch_shapes=[pltpu.SemaphoreType.DMA(())]),
        out_shape=jax.ShapeDtypeStruct(init.shape, src.dtype),
        input_output_aliases={2: 0},
        compiler_params=_cparams(),
        name="scatter_rows",
    )(idx, src, init)


def _ffn_kernel(ea_ref, eb_ref, nact_ref, x_ref, rwt_ref, wga_ref, wua_ref, wda_ref, wgb_ref, wub_ref, wdb_ref,
                o_ref):
    j = pl.program_id(0)

    @pl.when(j < nact_ref[0])
    def _():
        x = _tt_load(x_ref, FFN_BLOCK, wga_ref.shape[0])
        sa = _sigmoid(jnp.sum(x * rwt_ref[pl.ds(ea_ref[j], 1), :], axis=-1, keepdims=True))
        sb = _sigmoid(jnp.sum(x * rwt_ref[pl.ds(eb_ref[j], 1), :], axis=-1, keepdims=True))
        inv = 1.0 / (sa + sb)
        xb = x.astype(BF16)
        aa = _silu(_dot(xb, wga_ref[...])) * _dot(xb, wua_ref[...]) * (sa * inv)
        ab = _silu(_dot(xb, wgb_ref[...])) * _dot(xb, wub_ref[...]) * (sb * inv)
        _tt_store(o_ref, _dot(aa, wda_ref[...]) + _dot(ab, wdb_ref[...]))

    @pl.when(j >= nact_ref[0])
    def _():
        o_ref[...] = jnp.zeros_like(o_ref)


def _ffn(xs, blk_ea, blk_eb, n_active, router_wt, w_gate, w_up, w_down):
    D, F = w_gate.shape[1:]
    rpt = D // LANES
    nb = xs.shape[0] // (FFN_BLOCK * rpt)

    def wa(j, ea, eb, na):
        return (ea[j], 0, 0)

    def wb(j, ea, eb, na):
        return (eb[j], 0, 0)

    row = pl.BlockSpec((FFN_BLOCK * rpt, LANES), lambda j, ea, eb, na: (j, 0))
    return pl.pallas_call(
        _ffn_kernel,
        grid_spec=pltpu.PrefetchScalarGridSpec(
            num_scalar_prefetch=3, grid=(nb,),
            in_specs=[row, pl.BlockSpec(router_wt.shape, lambda j, ea, eb, na: (0, 0)),
                      pl.BlockSpec((None, D, F), wa), pl.BlockSpec((None, D, F), wa), pl.BlockSpec((None, F, D), wa),
                      pl.BlockSpec((None, D, F), wb), pl.BlockSpec((None, D, F), wb), pl.BlockSpec((None, F, D), wb)],
            out_specs=row),
        out_shape=jax.ShapeDtypeStruct(xs.shape, F32),
        compiler_params=_cparams(),
        name="moe_ffn",
    )(blk_ea, blk_eb, n_active, xs, router_wt, w_gate, w_up, w_down, w_gate, w_up, w_down)


def _moe(layer, x1, mod, g, router_w, router_bias, w_gate, w_up, w_down, T, xs_spare):
    N, D = x1.shape
    h2, cls, rank, cnt = _route(x1, mod, g, router_w, router_bias, T)
    counts = cnt.astype(jnp.int32)
    padded = (counts + FFN_BLOCK - 1) // FFN_BLOCK * FFN_BLOCK
    ends = jnp.cumsum(padded)
    starts = ends - padded
    dest = starts[cls] + rank
    nb = N // FFN_BLOCK + N_CLASSES
    n_active = (ends[-1] // FFN_BLOCK).astype(jnp.int32)
    blk = jnp.arange(nb, dtype=jnp.int32)
    blk_start = jnp.minimum(blk, n_active - 1) * FFN_BLOCK
    blk_cls = jnp.sum((ends[None, :] <= blk_start[:, None]).astype(jnp.int32), axis=1)
    blk_cls = jnp.minimum(blk_cls, N_CLASSES - 1)
    pair_a = jnp.array([p[0] for p in PAIRS], jnp.int32)
    pair_b = jnp.array([p[1] for p in PAIRS], jnp.int32)
    grp = blk_cls // len(PAIRS)
    blk_ea = grp * EXPERTS_PER_GROUP + pair_a[blk_cls % len(PAIRS)]
    blk_eb = grp * EXPERTS_PER_GROUP + pair_b[blk_cls % len(PAIRS)]
    rpt = D // LANES
    xs = _scatter_rows(h2, dest, nb * FFN_BLOCK, rpt, xs_spare)
    ys = _ffn(xs, blk_ea, blk_eb, n_active.reshape(1), jnp.transpose(router_w),
              _cast_bf16(w_gate, layer), _cast_bf16(w_up, layer), _cast_bf16(w_down, layer))
    return ys, dest, xs


def _final_kernel(tm, dest_ref, x_ref, ys_ref, pm_ref, g_ref, o_ref, buf, sem):
    x = x_ref[...]
    y, _, _ = _gathered_rows(dest_ref, ys_ref, buf, sem, tm, x.shape[1], burst=True)
    x = x + pm_ref[5:6, :] * y
    ms = jnp.mean(x * x, axis=-1, keepdims=True)
    o_ref[...] = x * lax.rsqrt(ms + EPS) * g_ref[...]


def _final(x, prev, g, T):
    ys, dest, mod = prev
    N, D = x.shape
    tm = min(512, T)
    row = pl.BlockSpec((tm, D), lambda i, *_: (i, 0))
    modspec = pl.BlockSpec((None, 6, D), lambda i, *_: ((i * tm) // T, 0, 0))
    return pl.pallas_call(
        functools.partial(_final_kernel, tm),
        grid_spec=pltpu.PrefetchScalarGridSpec(
            num_scalar_prefetch=1, grid=(N // tm,),
            in_specs=[row, pl.BlockSpec(memory_space=pl.ANY), modspec, _full((1, D))],
            out_specs=row,
            scratch_shapes=[pltpu.VMEM((2, tm * D // LANES, LANES), F32), pltpu.SemaphoreType.DMA((2,))]),
        out_shape=jax.ShapeDtypeStruct((N, D), F32),
        compiler_params=_cparams(),
        name="final_norm",
    )(dest, x, ys, mod, g.reshape(1, D))


def kernel(x, c, ada_w, ada_b, norm_g, final_norm_g, pool_w_in, pool_w_grp, pool_scale, pool_w_out, hgrn_w_in, hgrn_lb_logits, hgrn_norm_g, hgrn_w_out, swa_w_in, swa_sinks, swa_w_out, conv_w_in, conv_w, conv_w_out, router_w, router_bias, moe_w_gate, moe_w_up, moe_w_down):
    B, T, D = x.shape
    depth = ada_w.shape[0]
    n_mixers = 4
    mod = _ada(c, ada_w, ada_b).reshape(depth, B, 6, D)
    xt = x.reshape(B * T, D)
    prev = None
    xs_spare = None
    for i in range(depth):
        m, j = i % n_mixers, i // n_mixers
        g1n = norm_g[i, 0]
        if m == 0:
            x1 = _pool_mixer(xt, prev, mod[i], g1n, pool_w_in[j], pool_w_grp[j], pool_scale[j], pool_w_out[j], T)
        elif m == 1:
            x1 = _hgrn_mixer(i, xt, prev, mod[i], g1n, hgrn_w_in[j], hgrn_lb_logits, hgrn_norm_g[j],
                             hgrn_w_out[j], T)
        elif m == 2:
            x1 = _swa_mixer(xt, prev, mod[i], g1n, swa_w_in[j], swa_sinks[j], swa_w_out[j], T)
        else:
            x1 = _conv_mixer(xt, prev, mod[i], g1n, conv_w_in[j], conv_w[j], conv_w_out[j], T)
        ys, dest, xs_spare = _moe(i, x1, mod[i], norm_g[i, 1], router_w, router_bias,
                                  moe_w_gate, moe_w_up, moe_w_down, T, xs_spare)
        xt = x1
        prev = (ys, dest, mod[i])
    out = _final(xt, prev, final_norm_g, T)
    return out.reshape(B, T, D)
```

```python
import functools

import jax
import jax.numpy as jnp
import numpy as np
from jax import lax
from jax.experimental import pallas as pl
from jax.experimental.pallas import tpu as pltpu

F32 = jnp.float32
BF16 = jnp.bfloat16
EPS = 1e-6
LOG2E = 1.4426950408889634

POOL_WINDOWS = (2, 4, 8, 16)
POOL_HALO = 16
HGRN_HEAD_DIM = 128
HGRN_CHUNK = 128
SWA_HEADS = 16
SWA_KV_HEADS = 4
SWA_WINDOW = 128
CONV_WIDTH = 3
CONV_HALO = 8
N_EXPERTS = 16
N_GROUPS = 4
EXPERTS_PER_GROUP = 4
PAIRS = ((0, 1), (0, 2), (0, 3), (1, 2), (1, 3), (2, 3))
N_CLASSES = N_GROUPS * len(PAIRS)
CLASS_ROWS = 32
LANES = 128
FFN_BLOCK = 256
PERM_CHUNK = 2048
VMEM_LIMIT = 52 * 1024 * 1024


def _cparams():
    return pltpu.CompilerParams(dimension_semantics=("arbitrary",), vmem_limit_bytes=VMEM_LIMIT)


def _full(shape):
    nd = len(shape)
    return pl.BlockSpec(shape, lambda i, *_: (0,) * nd)


def _norm_mod(x, g, sc, sh):
    ms = jnp.mean(x * x, axis=-1, keepdims=True)
    return x * lax.rsqrt(ms + EPS) * (g * (1.0 + sc)) + sh


def _sigmoid(x):
    return 1.0 / (1.0 + jnp.exp(-x))


def _silu(x):
    return x * _sigmoid(x)


def _dot(a, b):
    return jnp.dot(a.astype(BF16), b.astype(BF16), preferred_element_type=F32)


def _dot_nt(a, b):
    return lax.dot_general(a.astype(BF16), b.astype(BF16), (((1,), (1,)), ((), ())), preferred_element_type=F32)


def _dot_tn(a, b):
    return lax.dot_general(a.astype(BF16), b.astype(BF16), (((0,), (0,)), ((), ())), preferred_element_type=F32)


def _tt_load(ref, rows, d):
    n = d // LANES
    return jnp.concatenate([ref[pl.ds(j, rows, stride=n), :] for j in range(n)], axis=1)


def _tt_store(ref, val):
    rows, d = val.shape
    n = d // LANES
    for j in range(n):
        ref[pl.ds(j, rows, stride=n), :] = val[:, j * LANES:(j + 1) * LANES]


def _cast_kernel(w_ref, o_ref):
    o_ref[...] = w_ref[...].astype(BF16)


def _cast_bf16(w, layer):
    _, E, K, M = w.shape
    return pl.pallas_call(
        _cast_kernel,
        grid=(E,),
        in_specs=[pl.BlockSpec((None, None, K, M), lambda e: (layer, e, 0, 0))],
        out_specs=pl.BlockSpec((None, K, M), lambda e: (e, 0, 0)),
        out_shape=jax.ShapeDtypeStruct((E, K, M), BF16),
        compiler_params=_cparams(),
        name="cast_bf16",
    )(w)


def _ada_kernel(c_ref, w_ref, b_ref, o_ref):
    cond = _silu(c_ref[...])
    o_ref[...] = jnp.dot(cond, w_ref[...], preferred_element_type=F32, precision=lax.Precision.HIGHEST) + b_ref[...]


def _ada(c, ada_w, ada_b):
    L, D, D6 = ada_w.shape
    B = c.shape[0]
    bn = D6 // 4
    return pl.pallas_call(
        _ada_kernel,
        grid=(L, D6 // bn),
        in_specs=[pl.BlockSpec((B, D), lambda l, j: (0, 0)),
                  pl.BlockSpec((None, D, bn), lambda l, j: (l, 0, j)),
                  pl.BlockSpec((None, 1, bn), lambda l, j: (l, 0, j))],
        out_specs=pl.BlockSpec((None, B, bn), lambda l, j: (l, 0, j)),
        out_shape=jax.ShapeDtypeStruct((L, B, D6), F32),
        compiler_params=pltpu.CompilerParams(dimension_semantics=("arbitrary", "arbitrary"),
                                             vmem_limit_bytes=VMEM_LIMIT),
        name="ada_mod",
    )(c, ada_w, ada_b.reshape(L, 1, D6))


def _mixer_call(body, name, tm, T, x, prev, mod, g, weights, scratch, smem=()):
    N, D = x.shape
    rpt = D // LANES
    row = pl.BlockSpec((tm, D), lambda i, *_: (i, 0))
    modspec = pl.BlockSpec((None, 6, D), lambda i, *_: ((i * tm) // T, 0, 0))
    args, specs, prefetch = [x], [row], []
    scratch = list(scratch)
    if prev is not None:
        prefetch = [prev[1]]
        args += [prev[0], prev[2]]
        specs += [pl.BlockSpec(memory_space=pl.ANY), modspec]
        scratch += [pltpu.VMEM((2, tm * rpt, LANES), F32), pltpu.SemaphoreType.DMA((2,))]
    args += [mod, g.reshape(1, D)]
    specs += [modspec, _full((1, D))]
    for w in weights:
        args.append(w)
        specs.append(_full(w.shape))
    for s in smem:
        args.append(s)
        specs.append(pl.BlockSpec(memory_space=pltpu.SMEM))
    return pl.pallas_call(
        functools.partial(body, prev is not None, tm, T),
        grid_spec=pltpu.PrefetchScalarGridSpec(
            num_scalar_prefetch=len(prefetch), grid=(N // tm,), in_specs=specs, out_specs=row,
            scratch_shapes=scratch),
        out_shape=jax.ShapeDtypeStruct((N, D), F32),
        compiler_params=_cparams(),
        name=name,
    )(*prefetch, *args)


def _gathered_rows(dest_ref, ys_ref, buf, sem, tm, d, burst=False):
    i = pl.program_id(0)
    last = pl.num_programs(0) - 1
    rpt = d // LANES

    def issue(tile, slot, r, priority):
        src = _tok(ys_ref, dest_ref[tile * tm + r], rpt)
        pltpu.make_async_copy(src, _tok(buf.at[slot], r, rpt), sem.at[slot]).start(priority=priority)

    def issue_all(tile, slot):
        def pair(r2, carry):
            issue(tile, slot, 2 * r2, 0)
            issue(tile, slot, 2 * r2 + 1, 1)
            return carry
        lax.fori_loop(0, tm // 2, pair, 0, unroll=4)

    def wait(slot):
        pltpu.make_async_copy(ys_ref.at[pl.ds(0, tm * rpt), :], buf.at[slot], sem.at[slot]).wait()

    @pl.when(i == 0)
    def _():
        issue_all(0, 0)

    if burst:
        @pl.when(i < last)
        def _():
            issue_all(i + 1, (i + 1) % 2)

        wait(i % 2)
        return _tt_load(buf.at[i % 2], tm, d), (lambda k, n: None), (lambda: None)

    nxt = jnp.minimum(i + 1, last)

    def prefetch(k, n):
        for r in range(k * (tm // n), (k + 1) * (tm // n)):
            issue(nxt, (i + 1) % 2, r, r % 2)

    def finish():
        @pl.when(i == last)
        def _():
            wait((i + 1) % 2)

    wait(i % 2)
    return _tt_load(buf.at[i % 2], tm, d), prefetch, finish


def _mixer_input(has_prev, tm, refs, burst=False):
    if has_prev:
        dest_ref, x_ref, ys_ref, pm_ref, m_ref, g_ref = refs[:6]
        buf, sem = refs[-2:]
        x = x_ref[...]
        y, prefetch, finish = _gathered_rows(dest_ref, ys_ref, buf, sem, tm, x.shape[1], burst)
        x = x + pm_ref[5:6, :] * y
        rest = refs[6:-2]
    else:
        x_ref, m_ref, g_ref = refs[:3]
        x = x_ref[...]
        rest = refs[3:]
        prefetch = lambda k, n: None
        finish = lambda: None
    return x, m_ref, g_ref, rest, prefetch, finish


def _pool_kernel(has_prev, tm, T, *refs):
    x, m_ref, g_ref, (win_ref, wgrp_ref, scale_ref, wout_ref, o_ref, tail_ref), prefetch, finish = \
        _mixer_input(has_prev, tm, refs)
    i = pl.program_id(0)
    start = (i * tm) % T
    n_pre = 2 * len(POOL_WINDOWS)
    h = _norm_mod(x, g_ref[...], m_ref[1:2, :], m_ref[0:1, :])
    u = _dot(h, win_ref[...])

    @pl.when(start == 0)
    def _():
        tail_ref[...] = jnp.zeros_like(tail_ref)

    pos = start + lax.broadcasted_iota(jnp.int32, (tm, 1), 0)
    C = u.shape[1] // len(POOL_WINDOWS)
    ys = []
    for gi, w in enumerate(POOL_WINDOWS):
        prefetch(2 * gi, n_pre)
        ug = u[:, gi * C:(gi + 1) * C]
        s = jnp.concatenate([tail_ref[:, gi * C:(gi + 1) * C], ug], axis=0)
        k = 1
        while k < w:
            s = s + pltpu.roll(s, k, 0)
            k *= 2
        cnt = jnp.minimum(pos + 1, w).astype(F32)
        pooled = s[POOL_HALO:] / cnt - ug
        prefetch(2 * gi + 1, n_pre)
        ys.append(_dot(pooled, wgrp_ref[gi]))
    tail_ref[...] = u[tm - POOL_HALO:, :]
    y = jnp.concatenate(ys, axis=1) * scale_ref[...]
    o_ref[...] = x + m_ref[2:3, :] * _dot(y, wout_ref[...])
    finish()


def _pool_mixer(x, prev, mod, g, w_in, w_grp, scale, w_out, T):
    D = x.shape[1]
    weights = [w_in.astype(BF16), w_grp.astype(BF16), scale.reshape(1, D), w_out.astype(BF16)]
    return _mixer_call(_pool_kernel, "mixer_pool", 512, T, x, prev, mod, g, weights,
                       [pltpu.VMEM((POOL_HALO, D), F32)])


def _conv_kernel(has_prev, tm, T, *refs):
    x, m_ref, g_ref, (win_ref, cw_ref, wout_ref, o_ref, tail_ref), prefetch, finish = \
        _mixer_input(has_prev, tm, refs)
    i = pl.program_id(0)
    D = x.shape[1]
    h = _norm_mod(x, g_ref[...], m_ref[1:2, :], m_ref[0:1, :]).astype(BF16)
    prefetch(0, 4)
    gate_b = _dot(h, win_ref[:, :D])
    prefetch(1, 4)
    z = _dot(h, win_ref[:, D:2 * D])
    prefetch(2, 4)
    z = z * _dot(h, win_ref[:, 2 * D:])
    prefetch(3, 4)

    @pl.when((i * tm) % T == 0)
    def _():
        tail_ref[...] = jnp.zeros_like(tail_ref)

    ze = jnp.concatenate([tail_ref[...], z], axis=0)
    zc = cw_ref[CONV_WIDTH - 1:CONV_WIDTH, :] * ze
    for j in range(1, CONV_WIDTH):
        zc = zc + cw_ref[CONV_WIDTH - 1 - j:CONV_WIDTH - j, :] * pltpu.roll(ze, j, 0)
    tail_ref[...] = z[tm - CONV_HALO:, :]
    y = gate_b * zc[CONV_HALO:]
    o_ref[...] = x + m_ref[2:3, :] * _dot(y, wout_ref[...])
    finish()


def _conv_mixer(x, prev, mod, g, w_in, conv_w, w_out, T):
    D = x.shape[1]
    weights = [w_in.astype(BF16), conv_w, w_out.astype(BF16)]
    return _mixer_call(_conv_kernel, "mixer_conv", 512, T, x, prev, mod, g, weights,
                       [pltpu.VMEM((CONV_HALO, D), F32)])


def _swa_kernel(has_prev, tm, T, *refs):
    x, m_ref, g_ref, (win_ref, wout_ref, sink_ref, o_ref, q_scr, klo, khi, vlo, vhi, o_scr), _, _ = \
        _mixer_input(has_prev, tm, refs, burst=True)
    i = pl.program_id(0)
    D = x.shape[1]
    W = SWA_WINDOW
    hd = D // SWA_HEADS
    G = SWA_HEADS // SWA_KV_HEADS
    kvd = SWA_KV_HEADS * hd
    assert 2 * hd == LANES
    seq_start = (i * tm) % T == 0
    kv_scr = (klo, khi, vlo, vhi)

    @pl.when(seq_start)
    def _():
        for r in kv_scr:
            r[0:W, :] = jnp.zeros((W, r.shape[1]), BF16)

    @pl.when(jnp.logical_not(seq_start))
    def _():
        for r in kv_scr:
            r[0:W, :] = r[tm:tm + W, :]

    h = _norm_mod(x, g_ref[...], m_ref[1:2, :], m_ref[0:1, :])
    qkv = _dot(h, win_ref[...])
    q_scr[...] = (qkv[:, :D] * (hd ** -0.5 * LOG2E)).astype(BF16)
    low = lax.broadcasted_iota(jnp.int32, (tm, LANES), 1) < hd
    for src, lo_ref, hi_ref in ((qkv[:, D:D + kvd], klo, khi), (qkv[:, D + kvd:], vlo, vhi)):
        for a in range(kvd // LANES):
            kg = src[:, a * LANES:(a + 1) * LANES]
            sw = pltpu.roll(kg, hd, 1)
            for par, (lo_v, hi_v) in enumerate(((kg, sw), (sw, kg))):
                c = (2 * a + par) * LANES
                lo_ref[W:, c:c + LANES] = jnp.where(low, lo_v, 0.0).astype(BF16)
                hi_ref[W:, c:c + LANES] = jnp.where(low, 0.0, hi_v).astype(BF16)
    qpos = lax.broadcasted_iota(jnp.int32, (W, 2 * W), 0)
    kj = lax.broadcasted_iota(jnp.int32, (W, 2 * W), 1)
    band = (kj > qpos) & (kj <= qpos + W)
    bias = jnp.where(band, 0.0, -jnp.inf)
    bias0 = jnp.where(band & (kj >= jnp.where(seq_start, W, 0)), 0.0, -jnp.inf)
    for j in range(tm // W):
        b = bias0 if j == 0 else bias
        for grp in range(SWA_HEADS // 2):
            hk = (2 * grp) // G
            qg = q_scr[j * W:(j + 1) * W, grp * LANES:(grp + 1) * LANES]
            pair = None
            for par, (k_ref, v_ref) in enumerate(((klo, vlo), (khi, vhi))):
                kk = k_ref[j * W:(j + 2) * W, hk * LANES:(hk + 1) * LANES]
                vv = v_ref[j * W:(j + 2) * W, hk * LANES:(hk + 1) * LANES]
                sink = sink_ref[2 * grp + par] * LOG2E
                s = lax.dot_general(qg, kk, (((1,), (1,)), ((), ())), preferred_element_type=F32) + b
                mx = jnp.maximum(jnp.max(s, axis=-1, keepdims=True), sink)
                p = jnp.exp2(s - mx)
                denom = jnp.sum(p, axis=-1, keepdims=True) + jnp.exp2(sink - mx)
                o = jnp.dot(p.astype(BF16), vv, preferred_element_type=F32) * (1.0 / denom)
                pair = o if pair is None else pair + o
            o_scr[j * W:(j + 1) * W, grp * LANES:(grp + 1) * LANES] = pair.astype(BF16)
    o_ref[...] = x + m_ref[2:3, :] * jnp.dot(o_scr[...], wout_ref[...], preferred_element_type=F32)


def _swa_mixer(x, prev, mod, g, w_in, sinks, w_out, T):
    D = x.shape[1]
    tm = 256
    rows = SWA_WINDOW + tm
    kv = pltpu.VMEM((rows, SWA_KV_HEADS * LANES), BF16)
    weights = [w_in.astype(BF16), w_out.astype(BF16)]
    return _mixer_call(_swa_kernel, "mixer_swa", tm, T, x, prev, mod, g, weights,
                       [pltpu.VMEM((tm, D), BF16), kv, kv, kv, kv, pltpu.VMEM((tm, D), BF16)], smem=[sinks])


HGRN_LEVELS = tuple(2 ** e for e in range(HGRN_CHUNK.bit_length() - 1, 0, -1))


def _hgrn_tri():
    tri = np.tril(np.ones((HGRN_CHUNK, HGRN_CHUNK), np.float32))
    return np.concatenate([tri, tri, tri], axis=1)


def _split3(a):
    hi = a.astype(BF16)
    r1 = a - hi.astype(F32)
    mid = r1.astype(BF16)
    lo = (r1 - mid.astype(F32)).astype(BF16)
    return hi, mid, lo


def _hgrn_kernel(layer, has_prev, tm, T, *refs):
    x, m_ref, g_ref, (win_ref, lbl_ref, ng_ref, wout_ref, tri_ref, o_ref, st_ref), prefetch, finish = \
        _mixer_input(has_prev, tm, refs)
    i = pl.program_id(0)
    D = x.shape[1]
    dk = HGRN_HEAD_DIM
    C = HGRN_CHUNK

    @pl.when((i * tm) % T == 0)
    def _():
        st_ref[...] = jnp.zeros_like(st_ref)

    rows = [lbl_ref[j:j + 1, :] for j in range(lbl_ref.shape[0])]
    mx = functools.reduce(jnp.maximum, rows)
    es = [jnp.exp(r - mx) for r in rows]
    tot = functools.reduce(lambda a, b: a + b, es)
    lb = jnp.zeros_like(mx)
    for j in range(1, layer + 1):
        lb = lb + es[j] / tot

    h = _norm_mod(x, g_ref[...], m_ref[1:2, :], m_ref[0:1, :])
    proj = _dot(h, win_ref[...])
    row = lax.broadcasted_iota(jnp.int32, (C, C), 0)
    col = lax.broadcasted_iota(jnp.int32, (C, C), 1)
    pair_masks = [(row // B == col // B) & (row % B >= B // 2) & (col % B < B // 2) for B in HGRN_LEVELS]
    diag = row == col
    trow = lax.broadcasted_iota(jnp.int32, (tm, 1), 0)
    nc = tm // C
    n_units = (D // dk) * nc

    def block_row(a, B, r):
        a3 = a.reshape(tm // B, B, a.shape[1])
        return jnp.broadcast_to(a3[:, r:r + 1, :], a3.shape).reshape(a.shape)

    outs = []
    for hh in range(D // dk):
        sl = slice(hh * dk, (hh + 1) * dk)
        q = _silu(proj[:, sl])
        lbh = lb[:, sl]
        f = lbh + (1.0 - lbh) * _sigmoid(proj[:, D + hh * dk:D + (hh + 1) * dk])
        kk = 1.0 - f
        v = proj[:, 2 * D + hh * dk:2 * D + (hh + 1) * dk]
        gate = proj[:, 3 * D + hh * dk:3 * D + (hh + 1) * dk]
        lf = jnp.log2(f)
        parts = _split3(lf)
        rhs = jnp.concatenate(
            [jnp.concatenate([p[c * C:(c + 1) * C] for p in parts], axis=0) for c in range(nc)], axis=1)
        bb = jnp.dot(tri_ref[...], rhs, preferred_element_type=F32)
        b = jnp.concatenate([bb[:, c * dk:(c + 1) * dk] for c in range(nc)], axis=0)
        from_start = jnp.exp2(b)
        qs = (q * from_start).astype(BF16)
        ks = (kk * jnp.exp2(block_row(b, C, C - 1) - b)).astype(BF16)
        zs = []
        for B in HGRN_LEVELS:
            pos = trow % B
            if B >= 8:
                e = (b - block_row(b, B, B // 2 - 1)) * jnp.where(pos >= B // 2, 1.0, -1.0)
            elif B == 4:
                e = jnp.where(pos == 0, pltpu.roll(lf, tm - 1, 0),
                              jnp.where(pos == 1, 0.0, jnp.where(pos == 2, lf, lf + pltpu.roll(lf, 1, 0))))
            else:
                e = jnp.where(pos == 1, lf, 0.0)
            zs.append((jnp.where(pos >= B // 2, q, kk) * jnp.exp2(e)).astype(BF16))
        qk = jnp.sum(q * kk, axis=-1, keepdims=True)
        st = st_ref[hh]
        oc = []
        for c in range(nc):
            prefetch(hh * nc + c, n_units)
            rs = slice(c * C, (c + 1) * C)
            vc = v[rs].astype(BF16)
            sc = jnp.where(diag, qk[rs], 0.0)
            for z, mask in zip(zs, pair_masks):
                sc = sc + jnp.where(mask, _dot_nt(z[rs], z[rs]), 0.0)
            oc.append(_dot(sc, vc) + _dot_nt(qs[rs], st))
            st = st * from_start[(c + 1) * C - 1:(c + 1) * C, :] + _dot_tn(vc, ks[rs])
        st_ref[hh] = st
        o = jnp.concatenate(oc, axis=0)
        o = o * lax.rsqrt(jnp.mean(o * o, axis=-1, keepdims=True) + EPS) * ng_ref[...]
        outs.append(o * _silu(gate))
    y = jnp.concatenate(outs, axis=1)
    o_ref[...] = x + m_ref[2:3, :] * _dot(y, wout_ref[...])
    finish()


def _hgrn_mixer(layer, x, prev, mod, g, w_in, lb_logits, norm_g, w_out, T):
    D = x.shape[1]
    dk = HGRN_HEAD_DIM
    weights = [w_in.astype(BF16), lb_logits, norm_g.reshape(1, dk), w_out.astype(BF16),
               jnp.asarray(_hgrn_tri(), BF16)]
    return _mixer_call(functools.partial(_hgrn_kernel, layer), "mixer_hgrn", 256, T, x, prev, mod, g, weights,
                       [pltpu.VMEM((D // dk, dk, dk), F32)])


def _split2(a):
    hi = a.astype(BF16)
    lo = (a - hi.astype(F32)).astype(BF16)
    return hi, lo


def _route_kernel(tm, x_ref, m_ref, g_ref, rw_ref, rb_ref, tri_ref, h_ref, cls_ref, rank_ref, cnt_ref, carry_ref):
    i = pl.program_id(0)

    @pl.when(i == 0)
    def _():
        carry_ref[...] = jnp.zeros_like(carry_ref)

    h = _norm_mod(x_ref[...], g_ref[...], m_ref[4:5, :], m_ref[3:4, :])
    _tt_store(h_ref, h)
    h_hi, h_lo = _split2(h)
    w_hi, w_lo = _split2(rw_ref[...])
    logits = (jnp.dot(h_hi, w_hi, preferred_element_type=F32) + jnp.dot(h_lo, w_hi, preferred_element_type=F32)
              + jnp.dot(h_hi, w_lo, preferred_element_type=F32))
    lt = jnp.transpose(logits)[:N_EXPERTS, :]
    score = _sigmoid(lt)
    sel = score + rb_ref[...]
    gscore, gsel = [], []
    for gi in range(N_GROUPS):
        r = [sel[gi * 4 + e:gi * 4 + e + 1, :] for e in range(EXPERTS_PER_GROUP)]
        m1 = functools.reduce(jnp.maximum, r)
        m2 = None
        for a in range(EXPERTS_PER_GROUP):
            for b2 in range(a + 1, EXPERTS_PER_GROUP):
                pm = jnp.minimum(r[a], r[b2])
                m2 = pm if m2 is None else jnp.maximum(m2, pm)
        gscore.append(m1 + m2)
        gsel.append(r)
    best = jnp.zeros((1, tm), jnp.int32)
    bs = gscore[0]
    for gi in range(1, N_GROUPS):
        better = gscore[gi] > bs
        best = jnp.where(better, gi, best)
        bs = jnp.where(better, gscore[gi], bs)
    r = [functools.reduce(lambda a, b2: a + b2,
                          [jnp.where(best == gi, gsel[gi][e], 0.0) for gi in range(N_GROUPS)])
         for e in range(EXPERTS_PER_GROUP)]
    keep = []
    for e in range(EXPERTS_PER_GROUP):
        beaten = jnp.zeros((1, tm), jnp.int32)
        for o in range(EXPERTS_PER_GROUP):
            if o != e:
                wins = (r[o] > r[e]) | ((r[o] == r[e]) & (o < e))
                beaten = beaten + wins.astype(jnp.int32)
        keep.append(beaten < 2)
    pair = jnp.zeros((1, tm), jnp.int32)
    for pi, (a, b2) in enumerate(PAIRS):
        pair = jnp.where(keep[a] & keep[b2], pi, pair)
    cls = best * len(PAIRS) + pair
    cls_ref[...] = cls.reshape(1, 1, tm)
    onehot = (lax.broadcasted_iota(jnp.int32, (CLASS_ROWS, tm), 0) == cls).astype(F32)
    before = jnp.dot(onehot.astype(BF16), tri_ref[...], preferred_element_type=F32) + carry_ref[:, 0:1]
    rank = jnp.sum(onehot * before, axis=0, keepdims=True)
    rank_ref[...] = rank.astype(jnp.int32).reshape(1, 1, tm)
    carry_ref[...] = carry_ref[...] + jnp.sum(onehot, axis=1, keepdims=True)
    cnt_ref[...] = carry_ref[...]


def _route(x1, mod, g, router_w, router_bias, T):
    N, D = x1.shape
    tm = 512
    nt = N // tm
    E = router_w.shape[1]
    rw = jnp.zeros((D, LANES), F32).at[:, :E].set(router_w)
    rb = router_bias.reshape(E, 1)
    tri = (jnp.arange(tm)[:, None] < jnp.arange(tm)[None, :]).astype(BF16)
    row = pl.BlockSpec((tm, D), lambda i: (i, 0))
    modspec = pl.BlockSpec((None, 6, D), lambda i: ((i * tm) // T, 0, 0))
    tok = pl.BlockSpec((1, 1, tm), lambda i: (i, 0, 0))
    h2, cls, rank, cnt = pl.pallas_call(
        functools.partial(_route_kernel, tm),
        grid=(nt,),
        in_specs=[row, modspec, _full((1, D)), _full((D, LANES)), _full((E, 1)), _full((tm, tm))],
        out_specs=[pl.BlockSpec((tm * D // LANES, LANES), lambda i: (i, 0)), tok, tok, _full((CLASS_ROWS, LANES))],
        out_shape=[jax.ShapeDtypeStruct((N * D // LANES, LANES), F32),
                   jax.ShapeDtypeStruct((nt, 1, tm), jnp.int32),
                   jax.ShapeDtypeStruct((nt, 1, tm), jnp.int32),
                   jax.ShapeDtypeStruct((CLASS_ROWS, LANES), F32)],
        scratch_shapes=[pltpu.VMEM((CLASS_ROWS, LANES), F32)],
        compiler_params=_cparams(),
        name="router",
    )(x1, mod, g.reshape(1, D), rw, rb, tri)
    return h2, cls.reshape(N), rank.reshape(N), cnt[:N_CLASSES, 0]


def _tok(ref, t, rpt):
    return ref.at[pl.ds(pl.multiple_of(t * rpt, rpt), rpt), :]


def _scatter_rows_kernel(chunk, rpt, idx_ref, src_ref, init_ref, dst_ref, sem):
    del init_ref
    base = pl.program_id(0) * chunk

    def issue(r2, carry):
        for par in range(2):
            r = 2 * r2 + par
            dst = _tok(dst_ref, idx_ref[base + r], rpt)
            pltpu.make_async_copy(_tok(src_ref, r, rpt), dst, sem).start(priority=par)
        return carry

    lax.fori_loop(0, chunk // 2, issue, 0, unroll=4)
    pltpu.make_async_copy(src_ref, dst_ref.at[pl.ds(0, chunk * rpt), :], sem).wait()


def _scatter_rows(src, idx, n_out, rpt, init=None):
    N = idx.shape[0]
    chunk = min(PERM_CHUNK, N)
    if init is None:
        init = jnp.zeros((n_out * rpt, LANES), src.dtype)
    return pl.pallas_call(
        functools.partial(_scatter_rows_kernel, chunk, rpt),
        grid_spec=pltpu.PrefetchScalarGridSpec(
            num_scalar_prefetch=1, grid=(N // chunk,),
            in_specs=[pl.BlockSpec((chunk * rpt, LANES), lambda i, idx: (i, 0)), pl.BlockSpec(memory_space=pl.ANY)],
            out_specs=pl.BlockSpec(memory_space=pl.ANY),
            scratch_shapes=[pltpu.SemaphoreType.DMA(())]),
        out_shape=jax.ShapeDtypeStruct(init.shape, src.dtype),
        input_output_aliases={2: 0},
        compiler_params=_cparams(),
        name="scatter_rows",
    )(idx, src, init)


def _ffn_kernel(ea_ref, eb_ref, nact_ref, x_ref, rwt_ref, wga_ref, wua_ref, wda_ref, wgb_ref, wub_ref, wdb_ref,
                o_ref):
    j = pl.program_id(0)

    @pl.when(j < nact_ref[0])
    def _():
        x = _tt_load(x_ref, FFN_BLOCK, wga_ref.shape[0])
        sa = _sigmoid(jnp.sum(x * rwt_ref[pl.ds(ea_ref[j], 1), :], axis=-1, keepdims=True))
        sb = _sigmoid(jnp.sum(x * rwt_ref[pl.ds(eb_ref[j], 1), :], axis=-1, keepdims=True))
        inv = 1.0 / (sa + sb)
        xb = x.astype(BF16)
        aa = _silu(_dot(xb, wga_ref[...])) * _dot(xb, wua_ref[...]) * (sa * inv)
        ab = _silu(_dot(xb, wgb_ref[...])) * _dot(xb, wub_ref[...]) * (sb * inv)
        _tt_store(o_ref, _dot(aa, wda_ref[...]) + _dot(ab, wdb_ref[...]))

    @pl.when(j >= nact_ref[0])
    def _():
        o_ref[...] = jnp.zeros_like(o_ref)


def _ffn(xs, blk_ea, blk_eb, n_active, router_wt, w_gate, w_up, w_down):
    D, F = w_gate.shape[1:]
    rpt = D // LANES
    nb = xs.shape[0] // (FFN_BLOCK * rpt)

    def wa(j, ea, eb, na):
        return (ea[j], 0, 0)

    def wb(j, ea, eb, na):
        return (eb[j], 0, 0)

    row = pl.BlockSpec((FFN_BLOCK * rpt, LANES), lambda j, ea, eb, na: (j, 0))
    return pl.pallas_call(
        _ffn_kernel,
        grid_spec=pltpu.PrefetchScalarGridSpec(
            num_scalar_prefetch=3, grid=(nb,),
            in_specs=[row, pl.BlockSpec(router_wt.shape, lambda j, ea, eb, na: (0, 0)),
                      pl.BlockSpec((None, D, F), wa), pl.BlockSpec((None, D, F), wa), pl.BlockSpec((None, F, D), wa),
                      pl.BlockSpec((None, D, F), wb), pl.BlockSpec((None, D, F), wb), pl.BlockSpec((None, F, D), wb)],
            out_specs=row),
        out_shape=jax.ShapeDtypeStruct(xs.shape, F32),
        compiler_params=_cparams(),
        name="moe_ffn",
    )(blk_ea, blk_eb, n_active, xs, router_wt, w_gate, w_up, w_down, w_gate, w_up, w_down)


def _moe(layer, x1, mod, g, router_w, router_bias, w_gate, w_up, w_down, T, xs_spare):
    N, D = x1.shape
    h2, cls, rank, cnt = _route(x1, mod, g, router_w, router_bias, T)
    counts = cnt.astype(jnp.int32)
    padded = (counts + FFN_BLOCK - 1) // FFN_BLOCK * FFN_BLOCK
    ends = jnp.cumsum(padded)
    starts = ends - padded
    dest = starts[cls] + rank
    nb = N // FFN_BLOCK + N_CLASSES
    n_active = (ends[-1] // FFN_BLOCK).astype(jnp.int32)
    blk = jnp.arange(nb, dtype=jnp.int32)
    blk_start = jnp.minimum(blk, n_active - 1) * FFN_BLOCK
    blk_cls = jnp.sum((ends[None, :] <= blk_start[:, None]).astype(jnp.int32), axis=1)
    blk_cls = jnp.minimum(blk_cls, N_CLASSES - 1)
    pair_a = jnp.array([p[0] for p in PAIRS], jnp.int32)
    pair_b = jnp.array([p[1] for p in PAIRS], jnp.int32)
    grp = blk_cls // len(PAIRS)
    blk_ea = grp * EXPERTS_PER_GROUP + pair_a[blk_cls % len(PAIRS)]
    blk_eb = grp * EXPERTS_PER_GROUP + pair_b[blk_cls % len(PAIRS)]
    rpt = D // LANES
    xs = _scatter_rows(h2, dest, nb * FFN_BLOCK, rpt, xs_spare)
    ys = _ffn(xs, blk_ea, blk_eb, n_active.reshape(1), jnp.transpose(router_w),
              _cast_bf16(w_gate, layer), _cast_bf16(w_up, layer), _cast_bf16(w_down, layer))
    return ys, dest, xs


def _final_kernel(tm, dest_ref, x_ref, ys_ref, pm_ref, g_ref, o_ref, buf, sem):
    x = x_ref[...]
    y, _, _ = _gathered_rows(dest_ref, ys_ref, buf, sem, tm, x.shape[1], burst=True)
    x = x + pm_ref[5:6, :] * y
    ms = jnp.mean(x * x, axis=-1, keepdims=True)
    o_ref[...] = x * lax.rsqrt(ms + EPS) * g_ref[...]


def _final(x, prev, g, T):
    ys, dest, mod = prev
    N, D = x.shape
    tm = min(512, T)
    row = pl.BlockSpec((tm, D), lambda i, *_: (i, 0))
    modspec = pl.BlockSpec((None, 6, D), lambda i, *_: ((i * tm) // T, 0, 0))
    return pl.pallas_call(
        functools.partial(_final_kernel, tm),
        grid_spec=pltpu.PrefetchScalarGridSpec(
            num_scalar_prefetch=1, grid=(N // tm,),
            in_specs=[row, pl.BlockSpec(memory_space=pl.ANY), modspec, _full((1, D))],
            out_specs=row,
            scratch_shapes=[pltpu.VMEM((2, tm * D // LANES, LANES), F32), pltpu.SemaphoreType.DMA((2,))]),
        out_shape=jax.ShapeDtypeStruct((N, D), F32),
        compiler_params=_cparams(),
        name="final_norm",
    )(dest, x, ys, mod, g.reshape(1, D))


def kernel(x, c, ada_w, ada_b, norm_g, final_norm_g, pool_w_in, pool_w_grp, pool_scale, pool_w_out, hgrn_w_in, hgrn_lb_logits, hgrn_norm_g, hgrn_w_out, swa_w_in, swa_sinks, swa_w_out, conv_w_in, conv_w, conv_w_out, router_w, router_bias, moe_w_gate, moe_w_up, moe_w_down):
    B, T, D = x.shape
    depth = ada_w.shape[0]
    n_mixers = 4
    mod = _ada(c, ada_w, ada_b).reshape(depth, B, 6, D)
    xt = x.reshape(B * T, D)
    prev = None
    xs_spare = None
    for i in range(depth):
        m, j = i % n_mixers, i // n_mixers
        g1n = norm_g[i, 0]
        if m == 0:
            x1 = _pool_mixer(xt, prev, mod[i], g1n, pool_w_in[j], pool_w_grp[j], pool_scale[j], pool_w_out[j], T)
        elif m == 1:
            x1 = _hgrn_mixer(i, xt, prev, mod[i], g1n, hgrn_w_in[j], hgrn_lb_logits, hgrn_norm_g[j],
                             hgrn_w_out[j], T)
        elif m == 2:
            x1 = _swa_mixer(xt, prev, mod[i], g1n, swa_w_in[j], swa_sinks[j], swa_w_out[j], T)
        else:
            x1 = _conv_mixer(xt, prev, mod[i], g1n, conv_w_in[j], conv_w[j], conv_w_out[j], T)
        ys, dest, xs_spare = _moe(i, x1, mod[i], norm_g[i, 1], router_w, router_bias,
                                  moe_w_gate, moe_w_up, moe_w_down, T, xs_spare)
        xt = x1
        prev = (ys, dest, mod[i])
    out = _final(xt, prev, final_norm_g, T)
    return out.reshape(B, T, D)
```

```python
import functools

import jax
import jax.numpy as jnp
import numpy as np
from jax import lax
from jax.experimental import pallas as pl
from jax.experimental.pallas import tpu as pltpu

F32 = jnp.float32
BF16 = jnp.bfloat16
EPS = 1e-6
LOG2E = 1.4426950408889634

POOL_WINDOWS = (2, 4, 8, 16)
POOL_HALO = 16
HGRN_HEAD_DIM = 128
HGRN_CHUNK = 128
SWA_HEADS = 16
SWA_KV_HEADS = 4
SWA_WINDOW = 128
CONV_WIDTH = 3
CONV_HALO = 8
N_EXPERTS = 16
N_GROUPS = 4
EXPERTS_PER_GROUP = 4
PAIRS = ((0, 1), (0, 2), (0, 3), (1, 2), (1, 3), (2, 3))
N_CLASSES = N_GROUPS * len(PAIRS)
CLASS_ROWS = 32
LANES = 128
FFN_BLOCK = 256
PERM_CHUNK = 2048
VMEM_LIMIT = 52 * 1024 * 1024


def _cparams():
    return pltpu.CompilerParams(dimension_semantics=("arbitrary",), vmem_limit_bytes=VMEM_LIMIT)


def _full(shape):
    nd = len(shape)
    return pl.BlockSpec(shape, lambda i, *_: (0,) * nd)


def _norm_mod(x, g, sc, sh):
    ms = jnp.mean(x * x, axis=-1, keepdims=True)
    return x * lax.rsqrt(ms + EPS) * (g * (1.0 + sc)) + sh


def _sigmoid(x):
    return 1.0 / (1.0 + jnp.exp(-x))


def _silu(x):
    return x * _sigmoid(x)


def _dot(a, b):
    return jnp.dot(a.astype(BF16), b.astype(BF16), preferred_element_type=F32)


def _dot_nt(a, b):
    return lax.dot_general(a.astype(BF16), b.astype(BF16), (((1,), (1,)), ((), ())), preferred_element_type=F32)


def _dot_tn(a, b):
    return lax.dot_general(a.astype(BF16), b.astype(BF16), (((0,), (0,)), ((), ())), preferred_element_type=F32)


def _tt_load(ref, rows, d):
    n = d // LANES
    return jnp.concatenate([ref[pl.ds(j, rows, stride=n), :] for j in range(n)], axis=1)


def _tt_store(ref, val):
    rows, d = val.shape
    n = d // LANES
    for j in range(n):
        ref[pl.ds(j, rows, stride=n), :] = val[:, j * LANES:(j + 1) * LANES]


def _cast_kernel(w_ref, o_ref):
    o_ref[...] = w_ref[...].astype(BF16)


def _cast_bf16(w, layer):
    _, E, K, M = w.shape
    return pl.pallas_call(
        _cast_kernel,
        grid=(E,),
        in_specs=[pl.BlockSpec((None, None, K, M), lambda e: (layer, e, 0, 0))],
        out_specs=pl.BlockSpec((None, K, M), lambda e: (e, 0, 0)),
        out_shape=jax.ShapeDtypeStruct((E, K, M), BF16),
        compiler_params=_cparams(),
        name="cast_bf16",
    )(w)


def _ada_kernel(c_ref, w_ref, b_ref, o_ref):
    cond = _silu(c_ref[...])
    o_ref[...] = jnp.dot(cond, w_ref[...], preferred_element_type=F32, precision=lax.Precision.HIGHEST) + b_ref[...]


def _ada(c, ada_w, ada_b):
    L, D, D6 = ada_w.shape
    B = c.shape[0]
    bn = D6 // 4
    return pl.pallas_call(
        _ada_kernel,
        grid=(L, D6 // bn),
        in_specs=[pl.BlockSpec((B, D), lambda l, j: (0, 0)),
                  pl.BlockSpec((None, D, bn), lambda l, j: (l, 0, j)),
                  pl.BlockSpec((None, 1, bn), lambda l, j: (l, 0, j))],
        out_specs=pl.BlockSpec((None, B, bn), lambda l, j: (l, 0, j)),
        out_shape=jax.ShapeDtypeStruct((L, B, D6), F32),
        compiler_params=pltpu.CompilerParams(dimension_semantics=("arbitrary", "arbitrary"),
                                             vmem_limit_bytes=VMEM_LIMIT),
        name="ada_mod",
    )(c, ada_w, ada_b.reshape(L, 1, D6))


def _mixer_call(body, name, tm, T, x, prev, mod, g, weights, scratch, smem=()):
    N, D = x.shape
    rpt = D // LANES
    row = pl.BlockSpec((tm, D), lambda i, *_: (i, 0))
    modspec = pl.BlockSpec((None, 6, D), lambda i, *_: ((i * tm) // T, 0, 0))
    args, specs, prefetch = [x], [row], []
    scratch = list(scratch)
    if prev is not None:
        prefetch = [prev[1]]
        args += [prev[0], prev[2]]
        specs += [pl.BlockSpec(memory_space=pl.ANY), modspec]
        scratch += [pltpu.VMEM((2, tm * rpt, LANES), F32), pltpu.SemaphoreType.DMA((2,))]
    args += [mod, g.reshape(1, D)]
    specs += [modspec, _full((1, D))]
    for w in weights:
        args.append(w)
        specs.append(_full(w.shape))
    for s in smem:
        args.append(s)
        specs.append(pl.BlockSpec(memory_space=pltpu.SMEM))
    return pl.pallas_call(
        functools.partial(body, prev is not None, tm, T),
        grid_spec=pltpu.PrefetchScalarGridSpec(
            num_scalar_prefetch=len(prefetch), grid=(N // tm,), in_specs=specs, out_specs=row,
            scratch_shapes=scratch),
        out_shape=jax.ShapeDtypeStruct((N, D), F32),
        compiler_params=_cparams(),
        name=name,
    )(*prefetch, *args)


def _gathered_rows(dest_ref, ys_ref, buf, sem, tm, d, burst=False):
    i = pl.program_id(0)
    last = pl.num_programs(0) - 1
    rpt = d // LANES

    def issue(tile, slot, r, priority):
        src = _tok(ys_ref, dest_ref[tile * tm + r], rpt)
        pltpu.make_async_copy(src, _tok(buf.at[slot], r, rpt), sem.at[slot]).start(priority=priority)

    def issue_all(tile, slot):
        def pair(r2, carry):
            issue(tile, slot, 2 * r2, 0)
            issue(tile, slot, 2 * r2 + 1, 1)
            return carry
        lax.fori_loop(0, tm // 2, pair, 0, unroll=4)

    def wait(slot):
        pltpu.make_async_copy(ys_ref.at[pl.ds(0, tm * rpt), :], buf.at[slot], sem.at[slot]).wait()

    @pl.when(i == 0)
    def _():
        issue_all(0, 0)

    if burst:
        @pl.when(i < last)
        def _():
            issue_all(i + 1, (i + 1) % 2)

        wait(i % 2)
        return _tt_load(buf.at[i % 2], tm, d), (lambda k, n: None), (lambda: None)

    nxt = jnp.minimum(i + 1, last)

    def prefetch(k, n):
        for r in range(k * (tm // n), (k + 1) * (tm // n)):
            issue(nxt, (i + 1) % 2, r, r % 2)

    def finish():
        @pl.when(i == last)
        def _():
            wait((i + 1) % 2)

    wait(i % 2)
    return _tt_load(buf.at[i % 2], tm, d), prefetch, finish


def _mixer_input(has_prev, tm, refs, burst=False):
    if has_prev:
        dest_ref, x_ref, ys_ref, pm_ref, m_ref, g_ref = refs[:6]
        buf, sem = refs[-2:]
        x = x_ref[...]
        y, prefetch, finish = _gathered_rows(dest_ref, ys_ref, buf, sem, tm, x.shape[1], burst)
        x = x + pm_ref[5:6, :] * y
        rest = refs[6:-2]
    else:
        x_ref, m_ref, g_ref = refs[:3]
        x = x_ref[...]
        rest = refs[3:]
        prefetch = lambda k, n: None
        finish = lambda: None
    return x, m_ref, g_ref, rest, prefetch, finish


def _pool_kernel(has_prev, tm, T, *refs):
    x, m_ref, g_ref, (win_ref, wgrp_ref, scale_ref, wout_ref, o_ref, tail_ref), prefetch, finish = \
        _mixer_input(has_prev, tm, refs)
    i = pl.program_id(0)
    start = (i * tm) % T
    n_pre = 2 * len(POOL_WINDOWS)
    h = _norm_mod(x, g_ref[...], m_ref[1:2, :], m_ref[0:1, :])
    u = _dot(h, win_ref[...])

    @pl.when(start == 0)
    def _():
        tail_ref[...] = jnp.zeros_like(tail_ref)

    pos = start + lax.broadcasted_iota(jnp.int32, (tm, 1), 0)
    C = u.shape[1] // len(POOL_WINDOWS)
    ys = []
    for gi, w in enumerate(POOL_WINDOWS):
        prefetch(2 * gi, n_pre)
        ug = u[:, gi * C:(gi + 1) * C]
        s = jnp.concatenate([tail_ref[:, gi * C:(gi + 1) * C], ug], axis=0)
        k = 1
        while k < w:
            s = s + pltpu.roll(s, k, 0)
            k *= 2
        cnt = jnp.minimum(pos + 1, w).astype(F32)
        pooled = s[POOL_HALO:] / cnt - ug
        prefetch(2 * gi + 1, n_pre)
        ys.append(_dot(pooled, wgrp_ref[gi]))
    tail_ref[...] = u[tm - POOL_HALO:, :]
    y = jnp.concatenate(ys, axis=1) * scale_ref[...]
    o_ref[...] = x + m_ref[2:3, :] * _dot(y, wout_ref[...])
    finish()


def _pool_mixer(x, prev, mod, g, w_in, w_grp, scale, w_out, T):
    D = x.shape[1]
    weights = [w_in.astype(BF16), w_grp.astype(BF16), scale.reshape(1, D), w_out.astype(BF16)]
    return _mixer_call(_pool_kernel, "mixer_pool", 512, T, x, prev, mod, g, weights,
                       [pltpu.VMEM((POOL_HALO, D), F32)])


def _conv_kernel(has_prev, tm, T, *refs):
    x, m_ref, g_ref, (win_ref, cw_ref, wout_ref, o_ref, tail_ref), prefetch, finish = \
        _mixer_input(has_prev, tm, refs)
    i = pl.program_id(0)
    D = x.shape[1]
    h = _norm_mod(x, g_ref[...], m_ref[1:2, :], m_ref[0:1, :]).astype(BF16)
    prefetch(0, 4)
    gate_b = _dot(h, win_ref[:, :D])
    prefetch(1, 4)
    z = _dot(h, win_ref[:, D:2 * D])
    prefetch(2, 4)
    z = z * _dot(h, win_ref[:, 2 * D:])
    prefetch(3, 4)

    @pl.when((i * tm) % T == 0)
    def _():
        tail_ref[...] = jnp.zeros_like(tail_ref)

    ze = jnp.concatenate([tail_ref[...], z], axis=0)
    zc = cw_ref[CONV_WIDTH - 1:CONV_WIDTH, :] * ze
    for j in range(1, CONV_WIDTH):
        zc = zc + cw_ref[CONV_WIDTH - 1 - j:CONV_WIDTH - j, :] * pltpu.roll(ze, j, 0)
    tail_ref[...] = z[tm - CONV_HALO:, :]
    y = gate_b * zc[CONV_HALO:]
    o_ref[...] = x + m_ref[2:3, :] * _dot(y, wout_ref[...])
    finish()


def _conv_mixer(x, prev, mod, g, w_in, conv_w, w_out, T):
    D = x.shape[1]
    weights = [w_in.astype(BF16), conv_w, w_out.astype(BF16)]
    return _mixer_call(_conv_kernel, "mixer_conv", 512, T, x, prev, mod, g, weights,
                       [pltpu.VMEM((CONV_HALO, D), F32)])


def _swa_kernel(has_prev, tm, T, *refs):
    x, m_ref, g_ref, (win_ref, wout_ref, sink_ref, o_ref, q_scr, klo, khi, vlo, vhi, o_scr), prefetch, finish = \
        _mixer_input(has_prev, tm, refs)
    i = pl.program_id(0)
    D = x.shape[1]
    W = SWA_WINDOW
    hd = D // SWA_HEADS
    G = SWA_HEADS // SWA_KV_HEADS
    kvd = SWA_KV_HEADS * hd
    assert 2 * hd == LANES
    seq_start = (i * tm) % T == 0
    kv_scr = (klo, khi, vlo, vhi)

    @pl.when(seq_start)
    def _():
        for r in kv_scr:
            r[0:W, :] = jnp.zeros((W, r.shape[1]), BF16)

    @pl.when(jnp.logical_not(seq_start))
    def _():
        for r in kv_scr:
            r[0:W, :] = r[tm:tm + W, :]

    h = _norm_mod(x, g_ref[...], m_ref[1:2, :], m_ref[0:1, :]).astype(BF16)
    pieces = []
    n_pieces = 4
    pw = (D + 2 * kvd) // n_pieces
    for part in range(n_pieces):
        prefetch(part, n_pieces)
        pieces.append(_dot(h, win_ref[:, part * pw:(part + 1) * pw]))
    qkv = jnp.concatenate(pieces, axis=1)
    q_scr[...] = (qkv[:, :D] * (hd ** -0.5 * LOG2E)).astype(BF16)
    low = lax.broadcasted_iota(jnp.int32, (tm, LANES), 1) < hd
    for src, lo_ref, hi_ref in ((qkv[:, D:D + kvd], klo, khi), (qkv[:, D + kvd:], vlo, vhi)):
        for a in range(kvd // LANES):
            kg = src[:, a * LANES:(a + 1) * LANES]
            sw = pltpu.roll(kg, hd, 1)
            for par, (lo_v, hi_v) in enumerate(((kg, sw), (sw, kg))):
                c = (2 * a + par) * LANES
                lo_ref[W:, c:c + LANES] = jnp.where(low, lo_v, 0.0).astype(BF16)
                hi_ref[W:, c:c + LANES] = jnp.where(low, 0.0, hi_v).astype(BF16)
    qpos = lax.broadcasted_iota(jnp.int32, (W, 2 * W), 0)
    kj = lax.broadcasted_iota(jnp.int32, (W, 2 * W), 1)
    band = (kj > qpos) & (kj <= qpos + W)
    bias = jnp.where(band, 0.0, -jnp.inf)
    bias0 = jnp.where(band & (kj >= jnp.where(seq_start, W, 0)), 0.0, -jnp.inf)
    for j in range(tm // W):
        b = bias0 if j == 0 else bias
        for grp in range(SWA_HEADS // 2):
            hk = (2 * grp) // G
            qg = q_scr[j * W:(j + 1) * W, grp * LANES:(grp + 1) * LANES]
            pair = None
            for par, (k_ref, v_ref) in enumerate(((klo, vlo), (khi, vhi))):
                kk = k_ref[j * W:(j + 2) * W, hk * LANES:(hk + 1) * LANES]
                vv = v_ref[j * W:(j + 2) * W, hk * LANES:(hk + 1) * LANES]
                sink = sink_ref[2 * grp + par] * LOG2E
                s = lax.dot_general(qg, kk, (((1,), (1,)), ((), ())), preferred_element_type=F32) + b
                mx = jnp.maximum(jnp.max(s, axis=-1, keepdims=True), sink)
                p = jnp.exp2(s - mx)
                denom = jnp.sum(p, axis=-1, keepdims=True) + jnp.exp2(sink - mx)
                o = jnp.dot(p.astype(BF16), vv, preferred_element_type=F32) * (1.0 / denom)
                pair = o if pair is None else pair + o
            o_scr[j * W:(j + 1) * W, grp * LANES:(grp + 1) * LANES] = pair.astype(BF16)
    o_ref[...] = x + m_ref[2:3, :] * jnp.dot(o_scr[...], wout_ref[...], preferred_element_type=F32)
    finish()


def _swa_mixer(x, prev, mod, g, w_in, sinks, w_out, T):
    D = x.shape[1]
    tm = 256
    rows = SWA_WINDOW + tm
    kv = pltpu.VMEM((rows, SWA_KV_HEADS * LANES), BF16)
    weights = [w_in.astype(BF16), w_out.astype(BF16)]
    return _mixer_call(_swa_kernel, "mixer_swa", tm, T, x, prev, mod, g, weights,
                       [pltpu.VMEM((tm, D), BF16), kv, kv, kv, kv, pltpu.VMEM((tm, D), BF16)], smem=[sinks])


HGRN_LEVELS = tuple(2 ** e for e in range(HGRN_CHUNK.bit_length() - 1, 0, -1))


def _hgrn_tri():
    tri = np.tril(np.ones((HGRN_CHUNK, HGRN_CHUNK), np.float32))
    return np.concatenate([tri, tri, tri], axis=1)


def _split3(a):
    hi = a.astype(BF16)
    r1 = a - hi.astype(F32)
    mid = r1.astype(BF16)
    lo = (r1 - mid.astype(F32)).astype(BF16)
    return hi, mid, lo


def _hgrn_kernel(layer, has_prev, tm, T, *refs):
    x, m_ref, g_ref, (win_ref, lbl_ref, ng_ref, wout_ref, tri_ref, o_ref, st_ref), prefetch, finish = \
        _mixer_input(has_prev, tm, refs)
    i = pl.program_id(0)
    D = x.shape[1]
    dk = HGRN_HEAD_DIM
    C = HGRN_CHUNK

    @pl.when((i * tm) % T == 0)
    def _():
        st_ref[...] = jnp.zeros_like(st_ref)

    rows = [lbl_ref[j:j + 1, :] for j in range(lbl_ref.shape[0])]
    mx = functools.reduce(jnp.maximum, rows)
    es = [jnp.exp(r - mx) for r in rows]
    tot = functools.reduce(lambda a, b: a + b, es)
    lb = jnp.zeros_like(mx)
    for j in range(1, layer + 1):
        lb = lb + es[j] / tot

    h = _norm_mod(x, g_ref[...], m_ref[1:2, :], m_ref[0:1, :]).astype(BF16)
    proj = []
    for part in range(4):
        prefetch(part, 4)
        proj.append(_dot(h, win_ref[:, part * D:(part + 1) * D]))
    pq, pf, pv, pg = proj
    row = lax.broadcasted_iota(jnp.int32, (C, C), 0)
    col = lax.broadcasted_iota(jnp.int32, (C, C), 1)
    pair_masks = [(row // B == col // B) & (row % B >= B // 2) & (col % B < B // 2) for B in HGRN_LEVELS]
    diag = row == col
    trow = lax.broadcasted_iota(jnp.int32, (tm, 1), 0)
    nc = tm // C

    def block_row(a, B, r):
        a3 = a.reshape(tm // B, B, a.shape[1])
        return jnp.broadcast_to(a3[:, r:r + 1, :], a3.shape).reshape(a.shape)

    outs = []
    for hh in range(D // dk):
        sl = slice(hh * dk, (hh + 1) * dk)
        q = _silu(pq[:, sl])
        lbh = lb[:, sl]
        f = lbh + (1.0 - lbh) * _sigmoid(pf[:, sl])
        kk = 1.0 - f
        v = pv[:, sl]
        gate = pg[:, sl]
        lf = jnp.log2(f)
        parts = _split3(lf)
        rhs = jnp.concatenate(
            [jnp.concatenate([p[c * C:(c + 1) * C] for p in parts], axis=0) for c in range(nc)], axis=1)
        bb = jnp.dot(tri_ref[...], rhs, preferred_element_type=F32)
        b = jnp.concatenate([bb[:, c * dk:(c + 1) * dk] for c in range(nc)], axis=0)
        from_start = jnp.exp2(b)
        qs = (q * from_start).astype(BF16)
        ks = (kk * jnp.exp2(block_row(b, C, C - 1) - b)).astype(BF16)
        zs = []
        for B in HGRN_LEVELS:
            pos = trow % B
            if B >= 8:
                e = (b - block_row(b, B, B // 2 - 1)) * jnp.where(pos >= B // 2, 1.0, -1.0)
            elif B == 4:
                e = jnp.where(pos == 0, pltpu.roll(lf, tm - 1, 0),
                              jnp.where(pos == 1, 0.0, jnp.where(pos == 2, lf, lf + pltpu.roll(lf, 1, 0))))
            else:
                e = jnp.where(pos == 1, lf, 0.0)
            zs.append((jnp.where(pos >= B // 2, q, kk) * jnp.exp2(e)).astype(BF16))
        qk = jnp.sum(q * kk, axis=-1, keepdims=True)
        st = st_ref[hh]
        oc = []
        for c in range(nc):
            rs = slice(c * C, (c + 1) * C)
            vc = v[rs].astype(BF16)
            sc = jnp.where(diag, qk[rs], 0.0)
            for z, mask in zip(zs, pair_masks):
                sc = sc + jnp.where(mask, _dot_nt(z[rs], z[rs]), 0.0)
            oc.append(_dot(sc, vc) + _dot_nt(qs[rs], st))
            st = st * from_start[(c + 1) * C - 1:(c + 1) * C, :] + _dot_tn(vc, ks[rs])
        st_ref[hh] = st
        o = jnp.concatenate(oc, axis=0)
        o = o * lax.rsqrt(jnp.mean(o * o, axis=-1, keepdims=True) + EPS) * ng_ref[...]
        outs.append(o * _silu(gate))
    y = jnp.concatenate(outs, axis=1)
    o_ref[...] = x + m_ref[2:3, :] * _dot(y, wout_ref[...])
    finish()


def _hgrn_mixer(layer, x, prev, mod, g, w_in, lb_logits, norm_g, w_out, T):
    D = x.shape[1]
    dk = HGRN_HEAD_DIM
    weights = [w_in.astype(BF16), lb_logits, norm_g.reshape(1, dk), w_out.astype(BF16),
               jnp.asarray(_hgrn_tri(), BF16)]
    return _mixer_call(functools.partial(_hgrn_kernel, layer), "mixer_hgrn", 256, T, x, prev, mod, g, weights,
                       [pltpu.VMEM((D // dk, dk, dk), F32)])


def _split2(a):
    hi = a.astype(BF16)
    lo = (a - hi.astype(F32)).astype(BF16)
    return hi, lo


def _route_kernel(tm, x_ref, m_ref, g_ref, rw_ref, rb_ref, tri_ref, h_ref, cls_ref, rank_ref, cnt_ref, carry_ref):
    i = pl.program_id(0)

    @pl.when(i == 0)
    def _():
        carry_ref[...] = jnp.zeros_like(carry_ref)

    h = _norm_mod(x_ref[...], g_ref[...], m_ref[4:5, :], m_ref[3:4, :])
    _tt_store(h_ref, h)
    h_hi, h_lo = _split2(h)
    w_hi, w_lo = _split2(rw_ref[...])
    logits = (jnp.dot(h_hi, w_hi, preferred_element_type=F32) + jnp.dot(h_lo, w_hi, preferred_element_type=F32)
              + jnp.dot(h_hi, w_lo, preferred_element_type=F32))
    lt = jnp.transpose(logits)[:N_EXPERTS, :]
    score = _sigmoid(lt)
    sel = score + rb_ref[...]
    gscore, gsel = [], []
    for gi in range(N_GROUPS):
        r = [sel[gi * 4 + e:gi * 4 + e + 1, :] for e in range(EXPERTS_PER_GROUP)]
        m1 = functools.reduce(jnp.maximum, r)
        m2 = None
        for a in range(EXPERTS_PER_GROUP):
            for b2 in range(a + 1, EXPERTS_PER_GROUP):
                pm = jnp.minimum(r[a], r[b2])
                m2 = pm if m2 is None else jnp.maximum(m2, pm)
        gscore.append(m1 + m2)
        gsel.append(r)
    best = jnp.zeros((1, tm), jnp.int32)
    bs = gscore[0]
    for gi in range(1, N_GROUPS):
        better = gscore[gi] > bs
        best = jnp.where(better, gi, best)
        bs = jnp.where(better, gscore[gi], bs)
    r = [functools.reduce(lambda a, b2: a + b2,
                          [jnp.where(best == gi, gsel[gi][e], 0.0) for gi in range(N_GROUPS)])
         for e in range(EXPERTS_PER_GROUP)]
    keep = []
    for e in range(EXPERTS_PER_GROUP):
        beaten = jnp.zeros((1, tm), jnp.int32)
        for o in range(EXPERTS_PER_GROUP):
            if o != e:
                wins = (r[o] > r[e]) | ((r[o] == r[e]) & (o < e))
                beaten = beaten + wins.astype(jnp.int32)
        keep.append(beaten < 2)
    pair = jnp.zeros((1, tm), jnp.int32)
    for pi, (a, b2) in enumerate(PAIRS):
        pair = jnp.where(keep[a] & keep[b2], pi, pair)
    cls = best * len(PAIRS) + pair
    cls_ref[...] = cls.reshape(1, 1, tm)
    onehot = (lax.broadcasted_iota(jnp.int32, (CLASS_ROWS, tm), 0) == cls).astype(F32)
    before = jnp.dot(onehot.astype(BF16), tri_ref[...], preferred_element_type=F32) + carry_ref[:, 0:1]
    rank = jnp.sum(onehot * before, axis=0, keepdims=True)
    rank_ref[...] = rank.astype(jnp.int32).reshape(1, 1, tm)
    carry_ref[...] = carry_ref[...] + jnp.sum(onehot, axis=1, keepdims=True)
    cnt_ref[...] = carry_ref[...]


def _route(x1, mod, g, router_w, router_bias, T):
    N, D = x1.shape
    tm = 512
    nt = N // tm
    E = router_w.shape[1]
    rw = jnp.zeros((D, LANES), F32).at[:, :E].set(router_w)
    rb = router_bias.reshape(E, 1)
    tri = (jnp.arange(tm)[:, None] < jnp.arange(tm)[None, :]).astype(BF16)
    row = pl.BlockSpec((tm, D), lambda i: (i, 0))
    modspec = pl.BlockSpec((None, 6, D), lambda i: ((i * tm) // T, 0, 0))
    tok = pl.BlockSpec((1, 1, tm), lambda i: (i, 0, 0))
    h2, cls, rank, cnt = pl.pallas_call(
        functools.partial(_route_kernel, tm),
        grid=(nt,),
        in_specs=[row, modspec, _full((1, D)), _full((D, LANES)), _full((E, 1)), _full((tm, tm))],
        out_specs=[pl.BlockSpec((tm * D // LANES, LANES), lambda i: (i, 0)), tok, tok, _full((CLASS_ROWS, LANES))],
        out_shape=[jax.ShapeDtypeStruct((N * D // LANES, LANES), F32),
                   jax.ShapeDtypeStruct((nt, 1, tm), jnp.int32),
                   jax.ShapeDtypeStruct((nt, 1, tm), jnp.int32),
                   jax.ShapeDtypeStruct((CLASS_ROWS, LANES), F32)],
        scratch_shapes=[pltpu.VMEM((CLASS_ROWS, LANES), F32)],
        compiler_params=_cparams(),
        name="router",
    )(x1, mod, g.reshape(1, D), rw, rb, tri)
    return h2, cls.reshape(N), rank.reshape(N), cnt[:N_CLASSES, 0]


def _tok(ref, t, rpt):
    return ref.at[pl.ds(pl.multiple_of(t * rpt, rpt), rpt), :]


def _scatter_rows_kernel(chunk, rpt, idx_ref, src_ref, init_ref, dst_ref, sem):
    del init_ref
    base = pl.program_id(0) * chunk

    def issue(r2, carry):
        for par in range(2):
            r = 2 * r2 + par
            dst = _tok(dst_ref, idx_ref[base + r], rpt)
            pltpu.make_async_copy(_tok(src_ref, r, rpt), dst, sem).start(priority=par)
        return carry

    lax.fori_loop(0, chunk // 2, issue, 0, unroll=4)
    pltpu.make_async_copy(src_ref, dst_ref.at[pl.ds(0, chunk * rpt), :], sem).wait()


def _scatter_rows(src, idx, n_out, rpt, init=None):
    N = idx.shape[0]
    chunk = min(PERM_CHUNK, N)
    if init is None:
        init = jnp.zeros((n_out * rpt, LANES), src.dtype)
    return pl.pallas_call(
        functools.partial(_scatter_rows_kernel, chunk, rpt),
        grid_spec=pltpu.PrefetchScalarGridSpec(
            num_scalar_prefetch=1, grid=(N // chunk,),
            in_specs=[pl.BlockSpec((chunk * rpt, LANES), lambda i, idx: (i, 0)), pl.BlockSpec(memory_space=pl.ANY)],
            out_specs=pl.BlockSpec(memory_space=pl.ANY),
            scratch_shapes=[pltpu.SemaphoreType.DMA(())]),
        out_shape=jax.ShapeDtypeStruct(init.shape, src.dtype),
        input_output_aliases={2: 0},
        compiler_params=_cparams(),
        name="scatter_rows",
    )(idx, src, init)


def _ffn_kernel(ea_ref, eb_ref, nact_ref, x_ref, rwt_ref, wga_ref, wua_ref, wda_ref, wgb_ref, wub_ref, wdb_ref,
                o_ref):
    j = pl.program_id(0)

    @pl.when(j < nact_ref[0])
    def _():
        x = _tt_load(x_ref, FFN_BLOCK, wga_ref.shape[0])
        sa = _sigmoid(jnp.sum(x * rwt_ref[pl.ds(ea_ref[j], 1), :], axis=-1, keepdims=True))
        sb = _sigmoid(jnp.sum(x * rwt_ref[pl.ds(eb_ref[j], 1), :], axis=-1, keepdims=True))
        inv = 1.0 / (sa + sb)
        xb = x.astype(BF16)
        aa = _silu(_dot(xb, wga_ref[...])) * _dot(xb, wua_ref[...]) * (sa * inv)
        ab = _silu(_dot(xb, wgb_ref[...])) * _dot(xb, wub_ref[...]) * (sb * inv)
        _tt_store(o_ref, _dot(aa, wda_ref[...]) + _dot(ab, wdb_ref[...]))

    @pl.when(j >= nact_ref[0])
    def _():
        o_ref[...] = jnp.zeros_like(o_ref)


def _ffn(xs, blk_ea, blk_eb, n_active, router_wt, w_gate, w_up, w_down):
    D, F = w_gate.shape[1:]
    rpt = D // LANES
    nb = xs.shape[0] // (FFN_BLOCK * rpt)

    def wa(j, ea, eb, na):
        return (ea[j], 0, 0)

    def wb(j, ea, eb, na):
        return (eb[j], 0, 0)

    row = pl.BlockSpec((FFN_BLOCK * rpt, LANES), lambda j, ea, eb, na: (j, 0))
    return pl.pallas_call(
        _ffn_kernel,
        grid_spec=pltpu.PrefetchScalarGridSpec(
            num_scalar_prefetch=3, grid=(nb,),
            in_specs=[row, pl.BlockSpec(router_wt.shape, lambda j, ea, eb, na: (0, 0)),
                      pl.BlockSpec((None, D, F), wa), pl.BlockSpec((None, D, F), wa), pl.BlockSpec((None, F, D), wa),
                      pl.BlockSpec((None, D, F), wb), pl.BlockSpec((None, D, F), wb), pl.BlockSpec((None, F, D), wb)],
            out_specs=row),
        out_shape=jax.ShapeDtypeStruct(xs.shape, F32),
        compiler_params=_cparams(),
        name="moe_ffn",
    )(blk_ea, blk_eb, n_active, xs, router_wt, w_gate, w_up, w_down, w_gate, w_up, w_down)


def _moe(layer, x1, mod, g, router_w, router_bias, w_gate, w_up, w_down, T, xs_spare):
    N, D = x1.shape
    h2, cls, rank, cnt = _route(x1, mod, g, router_w, router_bias, T)
    counts = cnt.astype(jnp.int32)
    padded = (counts + FFN_BLOCK - 1) // FFN_BLOCK * FFN_BLOCK
    ends = jnp.cumsum(padded)
    starts = ends - padded
    dest = starts[cls] + rank
    nb = N // FFN_BLOCK + N_CLASSES
    n_active = (ends[-1] // FFN_BLOCK).astype(jnp.int32)
    blk = jnp.arange(nb, dtype=jnp.int32)
    blk_start = jnp.minimum(blk, n_active - 1) * FFN_BLOCK
    blk_cls = jnp.sum((ends[None, :] <= blk_start[:, None]).astype(jnp.int32), axis=1)
    blk_cls = jnp.minimum(blk_cls, N_CLASSES - 1)
    pair_a = jnp.array([p[0] for p in PAIRS], jnp.int32)
    pair_b = jnp.array([p[1] for p in PAIRS], jnp.int32)
    grp = blk_cls // len(PAIRS)
    blk_ea = grp * EXPERTS_PER_GROUP + pair_a[blk_cls % len(PAIRS)]
    blk_eb = grp * EXPERTS_PER_GROUP + pair_b[blk_cls % len(PAIRS)]
    rpt = D // LANES
    xs = _scatter_rows(h2, dest, nb * FFN_BLOCK, rpt, xs_spare)
    ys = _ffn(xs, blk_ea, blk_eb, n_active.reshape(1), jnp.transpose(router_w),
              _cast_bf16(w_gate, layer), _cast_bf16(w_up, layer), _cast_bf16(w_down, layer))
    return ys, dest, xs


def _final_kernel(tm, dest_ref, x_ref, ys_ref, pm_ref, g_ref, o_ref, buf, sem):
    x = x_ref[...]
    y, _, _ = _gathered_rows(dest_ref, ys_ref, buf, sem, tm, x.shape[1], burst=True)
    x = x + pm_ref[5:6, :] * y
    ms = jnp.mean(x * x, axis=-1, keepdims=True)
    o_ref[...] = x * lax.rsqrt(ms + EPS) * g_ref[...]


def _final(x, prev, g, T):
    ys, dest, mod = prev
    N, D = x.shape
    tm = min(512, T)
    row = pl.BlockSpec((tm, D), lambda i, *_: (i, 0))
    modspec = pl.BlockSpec((None, 6, D), lambda i, *_: ((i * tm) // T, 0, 0))
    return pl.pallas_call(
        functools.partial(_final_kernel, tm),
        grid_spec=pltpu.PrefetchScalarGridSpec(
            num_scalar_prefetch=1, grid=(N // tm,),
            in_specs=[row, pl.BlockSpec(memory_space=pl.ANY), modspec, _full((1, D))],
            out_specs=row,
            scratch_shapes=[pltpu.VMEM((2, tm * D // LANES, LANES), F32), pltpu.SemaphoreType.DMA((2,))]),
        out_shape=jax.ShapeDtypeStruct((N, D), F32),
        compiler_params=_cparams(),
        name="final_norm",
    )(dest, x, ys, mod, g.reshape(1, D))


def kernel(x, c, ada_w, ada_b, norm_g, final_norm_g, pool_w_in, pool_w_grp, pool_scale, pool_w_out, hgrn_w_in, hgrn_lb_logits, hgrn_norm_g, hgrn_w_out, swa_w_in, swa_sinks, swa_w_out, conv_w_in, conv_w, conv_w_out, router_w, router_bias, moe_w_gate, moe_w_up, moe_w_down):
    B, T, D = x.shape
    depth = ada_w.shape[0]
    n_mixers = 4
    mod = _ada(c, ada_w, ada_b).reshape(depth, B, 6, D)
    xt = x.reshape(B * T, D)
    prev = None
    xs_spare = None
    for i in range(depth):
        m, j = i % n_mixers, i // n_mixers
        g1n = norm_g[i, 0]
        if m == 0:
            x1 = _pool_mixer(xt, prev, mod[i], g1n, pool_w_in[j], pool_w_grp[j], pool_scale[j], pool_w_out[j], T)
        elif m == 1:
            x1 = _hgrn_mixer(i, xt, prev, mod[i], g1n, hgrn_w_in[j], hgrn_lb_logits, hgrn_norm_g[j],
                             hgrn_w_out[j], T)
        elif m == 2:
            x1 = _swa_mixer(xt, prev, mod[i], g1n, swa_w_in[j], swa_sinks[j], swa_w_out[j], T)
        else:
            x1 = _conv_mixer(xt, prev, mod[i], g1n, conv_w_in[j], conv_w[j], conv_w_out[j], T)
        ys, dest, xs_spare = _moe(i, x1, mod[i], norm_g[i, 1], router_w, router_bias,
                                  moe_w_gate, moe_w_up, moe_w_down, T, xs_spare)
        xt = x1
        prev = (ys, dest, mod[i])
    out = _final(xt, prev, final_norm_g, T)
    return out.reshape(B, T, D)
```

```python
import functools

import jax
import jax.numpy as jnp
import numpy as np
from jax import lax
from jax.experimental import pallas as pl
from jax.experimental.pallas import tpu as pltpu

F32 = jnp.float32
BF16 = jnp.bfloat16
EPS = 1e-6
LOG2E = 1.4426950408889634

POOL_WINDOWS = (2, 4, 8, 16)
POOL_HALO = 16
HGRN_HEAD_DIM = 128
HGRN_CHUNK = 128
SWA_HEADS = 16
SWA_KV_HEADS = 4
SWA_WINDOW = 128
CONV_WIDTH = 3
CONV_HALO = 8
N_EXPERTS = 16
N_GROUPS = 4
EXPERTS_PER_GROUP = 4
PAIRS = ((0, 1), (0, 2), (0, 3), (1, 2), (1, 3), (2, 3))
N_CLASSES = N_GROUPS * len(PAIRS)
CLASS_ROWS = 32
LANES = 128
FFN_BLOCK = 256
PERM_CHUNK = 2048
VMEM_LIMIT = 52 * 1024 * 1024


def _cparams():
    return pltpu.CompilerParams(dimension_semantics=("arbitrary",), vmem_limit_bytes=VMEM_LIMIT)


def _full(shape):
    nd = len(shape)
    return pl.BlockSpec(shape, lambda i, *_: (0,) * nd)


def _norm_mod(x, g, sc, sh):
    ms = jnp.mean(x * x, axis=-1, keepdims=True)
    return x * lax.rsqrt(ms + EPS) * (g * (1.0 + sc)) + sh


def _sigmoid(x):
    return 1.0 / (1.0 + jnp.exp(-x))


def _silu(x):
    return x * _sigmoid(x)


def _dot(a, b):
    return jnp.dot(a.astype(BF16), b.astype(BF16), preferred_element_type=F32)


def _dot_nt(a, b):
    return lax.dot_general(a.astype(BF16), b.astype(BF16), (((1,), (1,)), ((), ())), preferred_element_type=F32)


def _dot_tn(a, b):
    return lax.dot_general(a.astype(BF16), b.astype(BF16), (((0,), (0,)), ((), ())), preferred_element_type=F32)


def _tt_load(ref, rows, d):
    n = d // LANES
    return jnp.concatenate([ref[pl.ds(j, rows, stride=n), :] for j in range(n)], axis=1)


def _tt_store(ref, val):
    rows, d = val.shape
    n = d // LANES
    for j in range(n):
        ref[pl.ds(j, rows, stride=n), :] = val[:, j * LANES:(j + 1) * LANES]


def _cast_kernel(w_ref, o_ref):
    o_ref[...] = w_ref[...].astype(BF16)


def _cast_bf16(w, layer):
    _, E, K, M = w.shape
    return pl.pallas_call(
        _cast_kernel,
        grid=(E,),
        in_specs=[pl.BlockSpec((None, None, K, M), lambda e: (layer, e, 0, 0))],
        out_specs=pl.BlockSpec((None, K, M), lambda e: (e, 0, 0)),
        out_shape=jax.ShapeDtypeStruct((E, K, M), BF16),
        compiler_params=_cparams(),
        name="cast_bf16",
    )(w)


def _ada_kernel(c_ref, w_ref, b_ref, o_ref):
    cond = _silu(c_ref[...])
    o_ref[...] = jnp.dot(cond, w_ref[...], preferred_element_type=F32, precision=lax.Precision.HIGHEST) + b_ref[...]


def _ada(c, ada_w, ada_b):
    L, D, D6 = ada_w.shape
    B = c.shape[0]
    bn = D6 // 4
    return pl.pallas_call(
        _ada_kernel,
        grid=(L, D6 // bn),
        in_specs=[pl.BlockSpec((B, D), lambda l, j: (0, 0)),
                  pl.BlockSpec((None, D, bn), lambda l, j: (l, 0, j)),
                  pl.BlockSpec((None, 1, bn), lambda l, j: (l, 0, j))],
        out_specs=pl.BlockSpec((None, B, bn), lambda l, j: (l, 0, j)),
        out_shape=jax.ShapeDtypeStruct((L, B, D6), F32),
        compiler_params=pltpu.CompilerParams(dimension_semantics=("arbitrary", "arbitrary"),
                                             vmem_limit_bytes=VMEM_LIMIT),
        name="ada_mod",
    )(c, ada_w, ada_b.reshape(L, 1, D6))


def _mixer_call(body, name, tm, T, x, prev, mod, g, weights, scratch, smem=()):
    N, D = x.shape
    rpt = D // LANES
    row = pl.BlockSpec((tm, D), lambda i, *_: (i, 0))
    modspec = pl.BlockSpec((None, 6, D), lambda i, *_: ((i * tm) // T, 0, 0))
    args, specs, prefetch = [x], [row], []
    scratch = list(scratch)
    if prev is not None:
        prefetch = [prev[1]]
        args += [prev[0], prev[2]]
        specs += [pl.BlockSpec(memory_space=pl.ANY), modspec]
        scratch += [pltpu.VMEM((2, tm * rpt, LANES), F32), pltpu.SemaphoreType.DMA((2,))]
    args += [mod, g.reshape(1, D)]
    specs += [modspec, _full((1, D))]
    for w in weights:
        args.append(w)
        specs.append(_full(w.shape))
    for s in smem:
        args.append(s)
        specs.append(pl.BlockSpec(memory_space=pltpu.SMEM))
    return pl.pallas_call(
        functools.partial(body, prev is not None, tm, T),
        grid_spec=pltpu.PrefetchScalarGridSpec(
            num_scalar_prefetch=len(prefetch), grid=(N // tm,), in_specs=specs, out_specs=row,
            scratch_shapes=scratch),
        out_shape=jax.ShapeDtypeStruct((N, D), F32),
        compiler_params=_cparams(),
        name=name,
    )(*prefetch, *args)


def _gathered_rows(dest_ref, ys_ref, buf, sem, tm, d, burst=False):
    i = pl.program_id(0)
    last = pl.num_programs(0) - 1
    rpt = d // LANES

    def issue(tile, slot, r, priority):
        src = _tok(ys_ref, dest_ref[tile * tm + r], rpt)
        pltpu.make_async_copy(src, _tok(buf.at[slot], r, rpt), sem.at[slot]).start(priority=priority)

    def issue_all(tile, slot):
        def pair(r2, carry):
            issue(tile, slot, 2 * r2, 0)
            issue(tile, slot, 2 * r2 + 1, 1)
            return carry
        lax.fori_loop(0, tm // 2, pair, 0, unroll=4)

    def wait(slot):
        pltpu.make_async_copy(ys_ref.at[pl.ds(0, tm * rpt), :], buf.at[slot], sem.at[slot]).wait()

    @pl.when(i == 0)
    def _():
        issue_all(0, 0)

    if burst:
        @pl.when(i < last)
        def _():
            issue_all(i + 1, (i + 1) % 2)

        wait(i % 2)
        return _tt_load(buf.at[i % 2], tm, d), (lambda k, n: None), (lambda: None)

    nxt = jnp.minimum(i + 1, last)

    def prefetch(k, n):
        for r in range(k * (tm // n), (k + 1) * (tm // n)):
            issue(nxt, (i + 1) % 2, r, r % 2)

    def finish():
        @pl.when(i == last)
        def _():
            wait((i + 1) % 2)

    wait(i % 2)
    return _tt_load(buf.at[i % 2], tm, d), prefetch, finish


def _mixer_input(has_prev, tm, refs, burst=False):
    if has_prev:
        dest_ref, x_ref, ys_ref, pm_ref, m_ref, g_ref = refs[:6]
        buf, sem = refs[-2:]
        x = x_ref[...]
        y, prefetch, finish = _gathered_rows(dest_ref, ys_ref, buf, sem, tm, x.shape[1], burst)
        x = x + pm_ref[5:6, :] * y
        rest = refs[6:-2]
    else:
        x_ref, m_ref, g_ref = refs[:3]
        x = x_ref[...]
        rest = refs[3:]
        prefetch = lambda k, n: None
        finish = lambda: None
    return x, m_ref, g_ref, rest, prefetch, finish


def _pool_kernel(has_prev, tm, T, *refs):
    x, m_ref, g_ref, (win_ref, wgrp_ref, scale_ref, wout_ref, o_ref, tail_ref), prefetch, finish = \
        _mixer_input(has_prev, tm, refs)
    i = pl.program_id(0)
    start = (i * tm) % T
    n_pre = 2 * len(POOL_WINDOWS)
    h = _norm_mod(x, g_ref[...], m_ref[1:2, :], m_ref[0:1, :])
    u = _dot(h, win_ref[...])

    @pl.when(start == 0)
    def _():
        tail_ref[...] = jnp.zeros_like(tail_ref)

    pos = start + lax.broadcasted_iota(jnp.int32, (tm, 1), 0)
    C = u.shape[1] // len(POOL_WINDOWS)
    ys = []
    for gi, w in enumerate(POOL_WINDOWS):
        prefetch(2 * gi, n_pre)
        ug = u[:, gi * C:(gi + 1) * C]
        s = jnp.concatenate([tail_ref[:, gi * C:(gi + 1) * C], ug], axis=0)
        k = 1
        while k < w:
            s = s + pltpu.roll(s, k, 0)
            k *= 2
        cnt = jnp.minimum(pos + 1, w).astype(F32)
        pooled = s[POOL_HALO:] / cnt - ug
        prefetch(2 * gi + 1, n_pre)
        ys.append(_dot(pooled, wgrp_ref[gi]))
    tail_ref[...] = u[tm - POOL_HALO:, :]
    y = jnp.concatenate(ys, axis=1) * scale_ref[...]
    o_ref[...] = x + m_ref[2:3, :] * _dot(y, wout_ref[...])
    finish()


def _pool_mixer(x, prev, mod, g, w_in, w_grp, scale, w_out, T):
    D = x.shape[1]
    weights = [w_in.astype(BF16), w_grp.astype(BF16), scale.reshape(1, D), w_out.astype(BF16)]
    return _mixer_call(_pool_kernel, "mixer_pool", 512, T, x, prev, mod, g, weights,
                       [pltpu.VMEM((POOL_HALO, D), F32)])


def _conv_kernel(has_prev, tm, T, *refs):
    x, m_ref, g_ref, (win_ref, cw_ref, wout_ref, o_ref, tail_ref), prefetch, finish = \
        _mixer_input(has_prev, tm, refs)
    i = pl.program_id(0)
    D = x.shape[1]
    h = _norm_mod(x, g_ref[...], m_ref[1:2, :], m_ref[0:1, :]).astype(BF16)
    prefetch(0, 4)
    gate_b = _dot(h, win_ref[:, :D])
    prefetch(1, 4)
    z = _dot(h, win_ref[:, D:2 * D])
    prefetch(2, 4)
    z = z * _dot(h, win_ref[:, 2 * D:])
    prefetch(3, 4)

    @pl.when((i * tm) % T == 0)
    def _():
        tail_ref[...] = jnp.zeros_like(tail_ref)

    ze = jnp.concatenate([tail_ref[...], z], axis=0)
    zc = cw_ref[CONV_WIDTH - 1:CONV_WIDTH, :] * ze
    for j in range(1, CONV_WIDTH):
        zc = zc + cw_ref[CONV_WIDTH - 1 - j:CONV_WIDTH - j, :] * pltpu.roll(ze, j, 0)
    tail_ref[...] = z[tm - CONV_HALO:, :]
    y = gate_b * zc[CONV_HALO:]
    o_ref[...] = x + m_ref[2:3, :] * _dot(y, wout_ref[...])
    finish()


def _conv_mixer(x, prev, mod, g, w_in, conv_w, w_out, T):
    D = x.shape[1]
    weights = [w_in.astype(BF16), conv_w, w_out.astype(BF16)]
    return _mixer_call(_conv_kernel, "mixer_conv", 512, T, x, prev, mod, g, weights,
                       [pltpu.VMEM((CONV_HALO, D), F32)])


def _swa_kernel(has_prev, tm, T, *refs):
    x, m_ref, g_ref, (win_ref, wout_ref, sink_ref, o_ref, q_scr, klo, khi, vlo, vhi, o_scr), _, _ = \
        _mixer_input(has_prev, tm, refs, burst=True)
    i = pl.program_id(0)
    D = x.shape[1]
    W = SWA_WINDOW
    hd = D // SWA_HEADS
    G = SWA_HEADS // SWA_KV_HEADS
    kvd = SWA_KV_HEADS * hd
    assert 2 * hd == LANES
    seq_start = (i * tm) % T == 0
    kv_scr = (klo, khi, vlo, vhi)

    @pl.when(seq_start)
    def _():
        for r in kv_scr:
            r[0:W, :] = jnp.zeros((W, r.shape[1]), BF16)

    @pl.when(jnp.logical_not(seq_start))
    def _():
        for r in kv_scr:
            r[0:W, :] = r[tm:tm + W, :]

    h = _norm_mod(x, g_ref[...], m_ref[1:2, :], m_ref[0:1, :])
    qkv = _dot(h, win_ref[...])
    q_scr[...] = (qkv[:, :D] * (hd ** -0.5 * LOG2E)).astype(BF16)
    low = lax.broadcasted_iota(jnp.int32, (tm, LANES), 1) < hd
    for src, lo_ref, hi_ref in ((qkv[:, D:D + kvd], klo, khi), (qkv[:, D + kvd:], vlo, vhi)):
        for a in range(kvd // LANES):
            kg = src[:, a * LANES:(a + 1) * LANES]
            sw = pltpu.roll(kg, hd, 1)
            for par, (lo_v, hi_v) in enumerate(((kg, sw), (sw, kg))):
                c = (2 * a + par) * LANES
                lo_ref[W:, c:c + LANES] = jnp.where(low, lo_v, 0.0).astype(BF16)
                hi_ref[W:, c:c + LANES] = jnp.where(low, 0.0, hi_v).astype(BF16)
    qpos = lax.broadcasted_iota(jnp.int32, (W, 2 * W), 0)
    kj = lax.broadcasted_iota(jnp.int32, (W, 2 * W), 1)
    band = (kj > qpos) & (kj <= qpos + W)
    bias = jnp.where(band, 0.0, -jnp.inf)
    bias0 = jnp.where(band & (kj >= jnp.where(seq_start, W, 0)), 0.0, -jnp.inf)
    for j in range(tm // W):
        b = bias0 if j == 0 else bias
        for grp in range(SWA_HEADS // 2):
            hk = (2 * grp) // G
            qg = q_scr[j * W:(j + 1) * W, grp * LANES:(grp + 1) * LANES]
            pair = None
            for par, (k_ref, v_ref) in enumerate(((klo, vlo), (khi, vhi))):
                kk = k_ref[j * W:(j + 2) * W, hk * LANES:(hk + 1) * LANES]
                vv = v_ref[j * W:(j + 2) * W, hk * LANES:(hk + 1) * LANES]
                sink = sink_ref[2 * grp + par] * LOG2E
                s = lax.dot_general(qg, kk, (((1,), (1,)), ((), ())), preferred_element_type=F32) + b
                mx = jnp.maximum(jnp.max(s, axis=-1, keepdims=True), sink)
                p = jnp.exp2(s - mx)
                denom = jnp.sum(p, axis=-1, keepdims=True) + jnp.exp2(sink - mx)
                o = jnp.dot(p.astype(BF16), vv, preferred_element_type=F32) * (1.0 / denom)
                pair = o if pair is None else pair + o
            o_scr[j * W:(j + 1) * W, grp * LANES:(grp + 1) * LANES] = pair.astype(BF16)
    o_ref[...] = x + m_ref[2:3, :] * jnp.dot(o_scr[...], wout_ref[...], preferred_element_type=F32)


def _swa_mixer(x, prev, mod, g, w_in, sinks, w_out, T):
    D = x.shape[1]
    tm = 256
    rows = SWA_WINDOW + tm
    kv = pltpu.VMEM((rows, SWA_KV_HEADS * LANES), BF16)
    weights = [w_in.astype(BF16), w_out.astype(BF16)]
    return _mixer_call(_swa_kernel, "mixer_swa", tm, T, x, prev, mod, g, weights,
                       [pltpu.VMEM((tm, D), BF16), kv, kv, kv, kv, pltpu.VMEM((tm, D), BF16)], smem=[sinks])


HGRN_LEVELS = tuple(2 ** e for e in range(HGRN_CHUNK.bit_length() - 1, 0, -1))


def _hgrn_tri():
    tri = np.tril(np.ones((HGRN_CHUNK, HGRN_CHUNK), np.float32))
    return np.concatenate([tri, tri, tri], axis=1)


def _split3(a):
    hi = a.astype(BF16)
    r1 = a - hi.astype(F32)
    mid = r1.astype(BF16)
    lo = (r1 - mid.astype(F32)).astype(BF16)
    return hi, mid, lo


def _hgrn_kernel(layer, has_prev, tm, T, *refs):
    x, m_ref, g_ref, (win_ref, lbl_ref, ng_ref, wout_ref, tri_ref, o_ref, st_ref), prefetch, finish = \
        _mixer_input(has_prev, tm, refs, burst=True)
    i = pl.program_id(0)
    D = x.shape[1]
    dk = HGRN_HEAD_DIM
    C = HGRN_CHUNK

    @pl.when((i * tm) % T == 0)
    def _():
        st_ref[...] = jnp.zeros_like(st_ref)

    rows = [lbl_ref[j:j + 1, :] for j in range(lbl_ref.shape[0])]
    mx = functools.reduce(jnp.maximum, rows)
    es = [jnp.exp(r - mx) for r in rows]
    tot = functools.reduce(lambda a, b: a + b, es)
    lb = jnp.zeros_like(mx)
    for j in range(1, layer + 1):
        lb = lb + es[j] / tot

    h = _norm_mod(x, g_ref[...], m_ref[1:2, :], m_ref[0:1, :]).astype(BF16)
    proj = []
    for part in range(4):
        prefetch(part, 4)
        proj.append(_dot(h, win_ref[:, part * D:(part + 1) * D]))
    pq, pf, pv, pg = proj
    row = lax.broadcasted_iota(jnp.int32, (C, C), 0)
    col = lax.broadcasted_iota(jnp.int32, (C, C), 1)
    pair_masks = [(row // B == col // B) & (row % B >= B // 2) & (col % B < B // 2) for B in HGRN_LEVELS]
    diag = row == col
    trow = lax.broadcasted_iota(jnp.int32, (tm, 1), 0)
    nc = tm // C

    def block_row(a, B, r):
        a3 = a.reshape(tm // B, B, a.shape[1])
        return jnp.broadcast_to(a3[:, r:r + 1, :], a3.shape).reshape(a.shape)

    outs = []
    for hh in range(D // dk):
        sl = slice(hh * dk, (hh + 1) * dk)
        q = _silu(pq[:, sl])
        lbh = lb[:, sl]
        f = lbh + (1.0 - lbh) * _sigmoid(pf[:, sl])
        kk = 1.0 - f
        v = pv[:, sl]
        gate = pg[:, sl]
        lf = jnp.log2(f)
        parts = _split3(lf)
        rhs = jnp.concatenate(
            [jnp.concatenate([p[c * C:(c + 1) * C] for p in parts], axis=0) for c in range(nc)], axis=1)
        bb = jnp.dot(tri_ref[...], rhs, preferred_element_type=F32)
        b = jnp.concatenate([bb[:, c * dk:(c + 1) * dk] for c in range(nc)], axis=0)
        from_start = jnp.exp2(b)
        qs = (q * from_start).astype(BF16)
        ks = (kk * jnp.exp2(block_row(b, C, C - 1) - b)).astype(BF16)
        zs = []
        for B in HGRN_LEVELS:
            pos = trow % B
            if B >= 8:
                e = (b - block_row(b, B, B // 2 - 1)) * jnp.where(pos >= B // 2, 1.0, -1.0)
            elif B == 4:
                e = jnp.where(pos == 0, pltpu.roll(lf, tm - 1, 0),
                              jnp.where(pos == 1, 0.0, jnp.where(pos == 2, lf, lf + pltpu.roll(lf, 1, 0))))
            else:
                e = jnp.where(pos == 1, lf, 0.0)
            zs.append((jnp.where(pos >= B // 2, q, kk) * jnp.exp2(e)).astype(BF16))
        qk = jnp.sum(q * kk, axis=-1, keepdims=True)
        st = st_ref[hh]
        oc = []
        for c in range(nc):
            rs = slice(c * C, (c + 1) * C)
            vc = v[rs].astype(BF16)
            sc = jnp.where(diag, qk[rs], 0.0)
            for z, mask in zip(zs, pair_masks):
                sc = sc + jnp.where(mask, _dot_nt(z[rs], z[rs]), 0.0)
            oc.append(_dot(sc, vc) + _dot_nt(qs[rs], st))
            st = st * from_start[(c + 1) * C - 1:(c + 1) * C, :] + _dot_tn(vc, ks[rs])
        st_ref[hh] = st
        o = jnp.concatenate(oc, axis=0)
        o = o * lax.rsqrt(jnp.mean(o * o, axis=-1, keepdims=True) + EPS) * ng_ref[...]
        outs.append(o * _silu(gate))
    y = jnp.concatenate(outs, axis=1)
    o_ref[...] = x + m_ref[2:3, :] * _dot(y, wout_ref[...])
    finish()


def _hgrn_mixer(layer, x, prev, mod, g, w_in, lb_logits, norm_g, w_out, T):
    D = x.shape[1]
    dk = HGRN_HEAD_DIM
    weights = [w_in.astype(BF16), lb_logits, norm_g.reshape(1, dk), w_out.astype(BF16),
               jnp.asarray(_hgrn_tri(), BF16)]
    return _mixer_call(functools.partial(_hgrn_kernel, layer), "mixer_hgrn", 256, T, x, prev, mod, g, weights,
                       [pltpu.VMEM((D // dk, dk, dk), F32)])


def _split2(a):
    hi = a.astype(BF16)
    lo = (a - hi.astype(F32)).astype(BF16)
    return hi, lo


def _route_kernel(tm, x_ref, m_ref, g_ref, rw_ref, rb_ref, tri_ref, h_ref, cls_ref, rank_ref, cnt_ref, carry_ref):
    i = pl.program_id(0)

    @pl.when(i == 0)
    def _():
        carry_ref[...] = jnp.zeros_like(carry_ref)

    h = _norm_mod(x_ref[...], g_ref[...], m_ref[4:5, :], m_ref[3:4, :])
    _tt_store(h_ref, h)
    h_hi, h_lo = _split2(h)
    w_hi, w_lo = _split2(rw_ref[...])
    logits = (jnp.dot(h_hi, w_hi, preferred_element_type=F32) + jnp.dot(h_lo, w_hi, preferred_element_type=F32)
              + jnp.dot(h_hi, w_lo, preferred_element_type=F32))
    lt = jnp.transpose(logits)[:N_EXPERTS, :]
    score = _sigmoid(lt)
    sel = score + rb_ref[...]
    gscore, gsel = [], []
    for gi in range(N_GROUPS):
        r = [sel[gi * 4 + e:gi * 4 + e + 1, :] for e in range(EXPERTS_PER_GROUP)]
        m1 = functools.reduce(jnp.maximum, r)
        m2 = None
        for a in range(EXPERTS_PER_GROUP):
            for b2 in range(a + 1, EXPERTS_PER_GROUP):
                pm = jnp.minimum(r[a], r[b2])
                m2 = pm if m2 is None else jnp.maximum(m2, pm)
        gscore.append(m1 + m2)
        gsel.append(r)
    best = jnp.zeros((1, tm), jnp.int32)
    bs = gscore[0]
    for gi in range(1, N_GROUPS):
        better = gscore[gi] > bs
        best = jnp.where(better, gi, best)
        bs = jnp.where(better, gscore[gi], bs)
    r = [functools.reduce(lambda a, b2: a + b2,
                          [jnp.where(best == gi, gsel[gi][e], 0.0) for gi in range(N_GROUPS)])
         for e in range(EXPERTS_PER_GROUP)]
    keep = []
    for e in range(EXPERTS_PER_GROUP):
        beaten = jnp.zeros((1, tm), jnp.int32)
        for o in range(EXPERTS_PER_GROUP):
            if o != e:
                wins = (r[o] > r[e]) | ((r[o] == r[e]) & (o < e))
                beaten = beaten + wins.astype(jnp.int32)
        keep.append(beaten < 2)
    pair = jnp.zeros((1, tm), jnp.int32)
    for pi, (a, b2) in enumerate(PAIRS):
        pair = jnp.where(keep[a] & keep[b2], pi, pair)
    cls = best * len(PAIRS) + pair
    cls_ref[...] = cls.reshape(1, 1, tm)
    onehot = (lax.broadcasted_iota(jnp.int32, (CLASS_ROWS, tm), 0) == cls).astype(F32)
    before = jnp.dot(onehot.astype(BF16), tri_ref[...], preferred_element_type=F32) + carry_ref[:, 0:1]
    rank = jnp.sum(onehot * before, axis=0, keepdims=True)
    rank_ref[...] = rank.astype(jnp.int32).reshape(1, 1, tm)
    carry_ref[...] = carry_ref[...] + jnp.sum(onehot, axis=1, keepdims=True)
    cnt_ref[...] = carry_ref[...]


def _route(x1, mod, g, router_w, router_bias, T):
    N, D = x1.shape
    tm = 512
    nt = N // tm
    E = router_w.shape[1]
    rw = jnp.zeros((D, LANES), F32).at[:, :E].set(router_w)
    rb = router_bias.reshape(E, 1)
    tri = (jnp.arange(tm)[:, None] < jnp.arange(tm)[None, :]).astype(BF16)
    row = pl.BlockSpec((tm, D), lambda i: (i, 0))
    modspec = pl.BlockSpec((None, 6, D), lambda i: ((i * tm) // T, 0, 0))
    tok = pl.BlockSpec((1, 1, tm), lambda i: (i, 0, 0))
    h2, cls, rank, cnt = pl.pallas_call(
        functools.partial(_route_kernel, tm),
        grid=(nt,),
        in_specs=[row, modspec, _full((1, D)), _full((D, LANES)), _full((E, 1)), _full((tm, tm))],
        out_specs=[pl.BlockSpec((tm * D // LANES, LANES), lambda i: (i, 0)), tok, tok, _full((CLASS_ROWS, LANES))],
        out_shape=[jax.ShapeDtypeStruct((N * D // LANES, LANES), F32),
                   jax.ShapeDtypeStruct((nt, 1, tm), jnp.int32),
                   jax.ShapeDtypeStruct((nt, 1, tm), jnp.int32),
                   jax.ShapeDtypeStruct((CLASS_ROWS, LANES), F32)],
        scratch_shapes=[pltpu.VMEM((CLASS_ROWS, LANES), F32)],
        compiler_params=_cparams(),
        name="router",
    )(x1, mod, g.reshape(1, D), rw, rb, tri)
    return h2, cls.reshape(N), rank.reshape(N), cnt[:N_CLASSES, 0]


def _tok(ref, t, rpt):
    return ref.at[pl.ds(pl.multiple_of(t * rpt, rpt), rpt), :]


def _scatter_rows_kernel(chunk, rpt, idx_ref, src_ref, init_ref, dst_ref, sem):
    del init_ref
    base = pl.program_id(0) * chunk

    def issue(r2, carry):
        for par in range(2):
            r = 2 * r2 + par
            dst = _tok(dst_ref, idx_ref[base + r], rpt)
            pltpu.make_async_copy(_tok(src_ref, r, rpt), dst, sem).start(priority=par)
        return carry

    lax.fori_loop(0, chunk // 2, issue, 0, unroll=4)
    pltpu.make_async_copy(src_ref, dst_ref.at[pl.ds(0, chunk * rpt), :], sem).wait()


def _scatter_rows(src, idx, n_out, rpt, init=None):
    N = idx.shape[0]
    chunk = min(PERM_CHUNK, N)
    if init is None:
        init = jnp.zeros((n_out * rpt, LANES), src.dtype)
    return pl.pallas_call(
        functools.partial(_scatter_rows_kernel, chunk, rpt),
        grid_spec=pltpu.PrefetchScalarGridSpec(
            num_scalar_prefetch=1, grid=(N // chunk,),
            in_specs=[pl.BlockSpec((chunk * rpt, LANES), lambda i, idx: (i, 0)), pl.BlockSpec(memory_space=pl.ANY)],
            out_specs=pl.BlockSpec(memory_space=pl.ANY),
            scratch_shapes=[pltpu.SemaphoreType.DMA(())]),
        out_shape=jax.ShapeDtypeStruct(init.shape, src.dtype),
        input_output_aliases={2: 0},
        compiler_params=_cparams(),
        name="scatter_rows",
    )(idx, src, init)


def _ffn_kernel(ea_ref, eb_ref, nact_ref, x_ref, rwt_ref, wga_ref, wua_ref, wda_ref, wgb_ref, wub_ref, wdb_ref,
                o_ref):
    j = pl.program_id(0)

    @pl.when(j < nact_ref[0])
    def _():
        x = _tt_load(x_ref, FFN_BLOCK, wga_ref.shape[0])
        sa = _sigmoid(jnp.sum(x * rwt_ref[pl.ds(ea_ref[j], 1), :], axis=-1, keepdims=True))
        sb = _sigmoid(jnp.sum(x * rwt_ref[pl.ds(eb_ref[j], 1), :], axis=-1, keepdims=True))
        inv = 1.0 / (sa + sb)
        xb = x.astype(BF16)
        aa = _silu(_dot(xb, wga_ref[...])) * _dot(xb, wua_ref[...]) * (sa * inv)
        ab = _silu(_dot(xb, wgb_ref[...])) * _dot(xb, wub_ref[...]) * (sb * inv)
        _tt_store(o_ref, _dot(aa, wda_ref[...]) + _dot(ab, wdb_ref[...]))

    @pl.when(j >= nact_ref[0])
    def _():
        o_ref[...] = jnp.zeros_like(o_ref)


def _ffn(xs, blk_ea, blk_eb, n_active, router_wt, w_gate, w_up, w_down):
    D, F = w_gate.shape[1:]
    rpt = D // LANES
    nb = xs.shape[0] // (FFN_BLOCK * rpt)

    def wa(j, ea, eb, na):
        return (ea[j], 0, 0)

    def wb(j, ea, eb, na):
        return (eb[j], 0, 0)

    row = pl.BlockSpec((FFN_BLOCK * rpt, LANES), lambda j, ea, eb, na: (j, 0))
    return pl.pallas_call(
        _ffn_kernel,
        grid_spec=pltpu.PrefetchScalarGridSpec(
            num_scalar_prefetch=3, grid=(nb,),
            in_specs=[row, pl.BlockSpec(router_wt.shape, lambda j, ea, eb, na: (0, 0)),
                      pl.BlockSpec((None, D, F), wa), pl.BlockSpec((None, D, F), wa), pl.BlockSpec((None, F, D), wa),
                      pl.BlockSpec((None, D, F), wb), pl.BlockSpec((None, D, F), wb), pl.BlockSpec((None, F, D), wb)],
            out_specs=row),
        out_shape=jax.ShapeDtypeStruct(xs.shape, F32),
        compiler_params=_cparams(),
        name="moe_ffn",
    )(blk_ea, blk_eb, n_active, xs, router_wt, w_gate, w_up, w_down, w_gate, w_up, w_down)


def _moe(layer, x1, mod, g, router_w, router_bias, w_gate, w_up, w_down, T, xs_spare):
    N, D = x1.shape
    h2, cls, rank, cnt = _route(x1, mod, g, router_w, router_bias, T)
    counts = cnt.astype(jnp.int32)
    padded = (counts + FFN_BLOCK - 1) // FFN_BLOCK * FFN_BLOCK
    ends = jnp.cumsum(padded)
    starts = ends - padded
    dest = starts[cls] + rank
    nb = N // FFN_BLOCK + N_CLASSES
    n_active = (ends[-1] // FFN_BLOCK).astype(jnp.int32)
    blk = jnp.arange(nb, dtype=jnp.int32)
    blk_start = jnp.minimum(blk, n_active - 1) * FFN_BLOCK
    blk_cls = jnp.sum((ends[None, :] <= blk_start[:, None]).astype(jnp.int32), axis=1)
    blk_cls = jnp.minimum(blk_cls, N_CLASSES - 1)
    pair_a = jnp.array([p[0] for p in PAIRS], jnp.int32)
    pair_b = jnp.array([p[1] for p in PAIRS], jnp.int32)
    grp = blk_cls // len(PAIRS)
    blk_ea = grp * EXPERTS_PER_GROUP + pair_a[blk_cls % len(PAIRS)]
    blk_eb = grp * EXPERTS_PER_GROUP + pair_b[blk_cls % len(PAIRS)]
    rpt = D // LANES
    xs = _scatter_rows(h2, dest, nb * FFN_BLOCK, rpt, xs_spare)
    ys = _ffn(xs, blk_ea, blk_eb, n_active.reshape(1), jnp.transpose(router_w),
              _cast_bf16(w_gate, layer), _cast_bf16(w_up, layer), _cast_bf16(w_down, layer))
    return ys, dest, xs


def _final_kernel(tm, dest_ref, x_ref, ys_ref, pm_ref, g_ref, o_ref, buf, sem):
    x = x_ref[...]
    y, _, _ = _gathered_rows(dest_ref, ys_ref, buf, sem, tm, x.shape[1], burst=True)
    x = x + pm_ref[5:6, :] * y
    ms = jnp.mean(x * x, axis=-1, keepdims=True)
    o_ref[...] = x * lax.rsqrt(ms + EPS) * g_ref[...]


def _final(x, prev, g, T):
    ys, dest, mod = prev
    N, D = x.shape
    tm = min(512, T)
    row = pl.BlockSpec((tm, D), lambda i, *_: (i, 0))
    modspec = pl.BlockSpec((None, 6, D), lambda i, *_: ((i * tm) // T, 0, 0))
    return pl.pallas_call(
        functools.partial(_final_kernel, tm),
        grid_spec=pltpu.PrefetchScalarGridSpec(
            num_scalar_prefetch=1, grid=(N // tm,),
            in_specs=[row, pl.BlockSpec(memory_space=pl.ANY), modspec, _full((1, D))],
            out_specs=row,
            scratch_shapes=[pltpu.VMEM((2, tm * D // LANES, LANES), F32), pltpu.SemaphoreType.DMA((2,))]),
        out_shape=jax.ShapeDtypeStruct((N, D), F32),
        compiler_params=_cparams(),
        name="final_norm",
    )(dest, x, ys, mod, g.reshape(1, D))


def kernel(x, c, ada_w, ada_b, norm_g, final_norm_g, pool_w_in, pool_w_grp, pool_scale, pool_w_out, hgrn_w_in, hgrn_lb_logits, hgrn_norm_g, hgrn_w_out, swa_w_in, swa_sinks, swa_w_out, conv_w_in, conv_w, conv_w_out, router_w, router_bias, moe_w_gate, moe_w_up, moe_w_down):
    B, T, D = x.shape
    depth = ada_w.shape[0]
    n_mixers = 4
    mod = _ada(c, ada_w, ada_b).reshape(depth, B, 6, D)
    xt = x.reshape(B * T, D)
    prev = None
    xs_spare = None
    for i in range(depth):
        m, j = i % n_mixers, i // n_mixers
        g1n = norm_g[i, 0]
        if m == 0:
            x1 = _pool_mixer(xt, prev, mod[i], g1n, pool_w_in[j], pool_w_grp[j], pool_scale[j], pool_w_out[j], T)
        elif m == 1:
            x1 = _hgrn_mixer(i, xt, prev, mod[i], g1n, hgrn_w_in[j], hgrn_lb_logits, hgrn_norm_g[j],
                             hgrn_w_out[j], T)
        elif m == 2:
            x1 = _swa_mixer(xt, prev, mod[i], g1n, swa_w_in[j], swa_sinks[j], swa_w_out[j], T)
        else:
            x1 = _conv_mixer(xt, prev, mod[i], g1n, conv_w_in[j], conv_w[j], conv_w_out[j], T)
        ys, dest, xs_spare = _moe(i, x1, mod[i], norm_g[i, 1], router_w, router_bias,
                                  moe_w_gate, moe_w_up, moe_w_down, T, xs_spare)
        xt = x1
        prev = (ys, dest, mod[i])
    out = _final(xt, prev, final_norm_g, T)
    return out.reshape(B, T, D)
```

```python
import functools

import jax
import jax.numpy as jnp
import numpy as np
from jax import lax
from jax.experimental import pallas as pl
from jax.experimental.pallas import tpu as pltpu

F32 = jnp.float32
BF16 = jnp.bfloat16
EPS = 1e-6
LOG2E = 1.4426950408889634

POOL_WINDOWS = (2, 4, 8, 16)
POOL_HALO = 16
HGRN_HEAD_DIM = 128
HGRN_CHUNK = 64
SWA_HEADS = 16
SWA_KV_HEADS = 4
SWA_WINDOW = 128
CONV_WIDTH = 3
CONV_HALO = 8
N_EXPERTS = 16
N_GROUPS = 4
EXPERTS_PER_GROUP = 4
PAIRS = ((0, 1), (0, 2), (0, 3), (1, 2), (1, 3), (2, 3))
N_CLASSES = N_GROUPS * len(PAIRS)
CLASS_ROWS = 32
LANES = 128
FFN_BLOCK = 256
PERM_CHUNK = 2048
VMEM_LIMIT = 52 * 1024 * 1024
TM_POOL = 512
TM_CONV = 512
TM_SWA = 256
TM_HGRN = 256
TM_ROUTER = 512
TM_FINAL = 512


def _cparams():
    return pltpu.CompilerParams(dimension_semantics=("arbitrary",), vmem_limit_bytes=VMEM_LIMIT)


def _full(shape):
    nd = len(shape)
    return pl.BlockSpec(shape, lambda i, *_: (0,) * nd)


def _norm_mod(x, g, sc, sh):
    ms = jnp.mean(x * x, axis=-1, keepdims=True)
    return x * lax.rsqrt(ms + EPS) * (g * (1.0 + sc)) + sh


def _sigmoid(x):
    return 1.0 / (1.0 + jnp.exp(-x))


def _silu(x):
    return x * _sigmoid(x)


def _dot(a, b):
    return jnp.dot(a.astype(BF16), b.astype(BF16), preferred_element_type=F32)


def _dot_nt(a, b):
    return lax.dot_general(a.astype(BF16), b.astype(BF16), (((1,), (1,)), ((), ())), preferred_element_type=F32)


def _dot_tn(a, b):
    return lax.dot_general(a.astype(BF16), b.astype(BF16), (((0,), (0,)), ((), ())), preferred_element_type=F32)


def _tt_load(ref, rows, d):
    n = d // LANES
    return jnp.concatenate([ref[pl.ds(j, rows, stride=n), :] for j in range(n)], axis=1)


def _tt_store(ref, val):
    rows, d = val.shape
    n = d // LANES
    for j in range(n):
        ref[pl.ds(j, rows, stride=n), :] = val[:, j * LANES:(j + 1) * LANES]


def _cast_kernel(w_ref, o_ref):
    o_ref[...] = w_ref[...].astype(BF16)


def _cast_bf16(w, layer):
    _, E, K, M = w.shape
    return pl.pallas_call(
        _cast_kernel,
        grid=(E,),
        in_specs=[pl.BlockSpec((None, None, K, M), lambda e: (layer, e, 0, 0))],
        out_specs=pl.BlockSpec((None, K, M), lambda e: (e, 0, 0)),
        out_shape=jax.ShapeDtypeStruct((E, K, M), BF16),
        compiler_params=_cparams(),
        name="cast_bf16",
    )(w)


def _ada_kernel(c_ref, w_ref, b_ref, o_ref):
    cond = _silu(c_ref[...])
    o_ref[...] = jnp.dot(cond, w_ref[...], preferred_element_type=F32, precision=lax.Precision.HIGHEST) + b_ref[...]


def _ada(c, ada_w, ada_b):
    L, D, D6 = ada_w.shape
    B = c.shape[0]
    bn = D6 // 4
    return pl.pallas_call(
        _ada_kernel,
        grid=(L, D6 // bn),
        in_specs=[pl.BlockSpec((B, D), lambda l, j: (0, 0)),
                  pl.BlockSpec((None, D, bn), lambda l, j: (l, 0, j)),
                  pl.BlockSpec((None, 1, bn), lambda l, j: (l, 0, j))],
        out_specs=pl.BlockSpec((None, B, bn), lambda l, j: (l, 0, j)),
        out_shape=jax.ShapeDtypeStruct((L, B, D6), F32),
        compiler_params=pltpu.CompilerParams(dimension_semantics=("arbitrary", "arbitrary"),
                                             vmem_limit_bytes=VMEM_LIMIT),
        name="ada_mod",
    )(c, ada_w, ada_b.reshape(L, 1, D6))


def _mixer_call(body, name, tm, T, x, prev, mod, g, weights, scratch, smem=()):
    N, D = x.shape
    rpt = D // LANES
    row = pl.BlockSpec((tm, D), lambda i, *_: (i, 0))
    modspec = pl.BlockSpec((None, 6, D), lambda i, *_: ((i * tm) // T, 0, 0))
    args, specs, prefetch = [x], [row], []
    scratch = list(scratch)
    if prev is not None:
        prefetch = [prev[1]]
        args += [prev[0], prev[2]]
        specs += [pl.BlockSpec(memory_space=pl.ANY), modspec]
        scratch += [pltpu.VMEM((2, tm * rpt, LANES), F32), pltpu.SemaphoreType.DMA((2,))]
    args += [mod, g.reshape(1, D)]
    specs += [modspec, _full((1, D))]
    for w in weights:
        args.append(w)
        specs.append(_full(w.shape))
    for s in smem:
        args.append(s)
        specs.append(pl.BlockSpec(memory_space=pltpu.SMEM))
    return pl.pallas_call(
        functools.partial(body, prev is not None, tm, T),
        grid_spec=pltpu.PrefetchScalarGridSpec(
            num_scalar_prefetch=len(prefetch), grid=(N // tm,), in_specs=specs, out_specs=row,
            scratch_shapes=scratch),
        out_shape=jax.ShapeDtypeStruct((N, D), F32),
        compiler_params=_cparams(),
        name=name,
    )(*prefetch, *args)


def _gathered_rows(dest_ref, ys_ref, buf, sem, tm, d, burst=False):
    i = pl.program_id(0)
    last = pl.num_programs(0) - 1
    rpt = d // LANES

    def issue(tile, slot, r, priority):
        src = _tok(ys_ref, dest_ref[tile * tm + r], rpt)
        pltpu.make_async_copy(src, _tok(buf.at[slot], r, rpt), sem.at[slot]).start(priority=priority)

    def issue_all(tile, slot):
        def pair(r2, carry):
            issue(tile, slot, 2 * r2, 0)
            issue(tile, slot, 2 * r2 + 1, 1)
            return carry
        lax.fori_loop(0, tm // 2, pair, 0, unroll=4)

    def wait(slot):
        pltpu.make_async_copy(ys_ref.at[pl.ds(0, tm * rpt), :], buf.at[slot], sem.at[slot]).wait()

    @pl.when(i == 0)
    def _():
        issue_all(0, 0)

    if burst:
        @pl.when(i < last)
        def _():
            issue_all(i + 1, (i + 1) % 2)

        wait(i % 2)
        return _tt_load(buf.at[i % 2], tm, d), (lambda k, n: None), (lambda: None)

    nxt = jnp.minimum(i + 1, last)

    def prefetch(k, n):
        for r in range(k * (tm // n), (k + 1) * (tm // n)):
            issue(nxt, (i + 1) % 2, r, r % 2)

    def finish():
        @pl.when(i == last)
        def _():
            wait((i + 1) % 2)

    wait(i % 2)
    return _tt_load(buf.at[i % 2], tm, d), prefetch, finish


def _mixer_input(has_prev, tm, refs, burst=False):
    if has_prev:
        dest_ref, x_ref, ys_ref, pm_ref, m_ref, g_ref = refs[:6]
        buf, sem = refs[-2:]
        x = x_ref[...]
        y, prefetch, finish = _gathered_rows(dest_ref, ys_ref, buf, sem, tm, x.shape[1], burst)
        x = x + pm_ref[5:6, :] * y
        rest = refs[6:-2]
    else:
        x_ref, m_ref, g_ref = refs[:3]
        x = x_ref[...]
        rest = refs[3:]
        prefetch = lambda k, n: None
        finish = lambda: None
    return x, m_ref, g_ref, rest, prefetch, finish


def _pool_kernel(has_prev, tm, T, *refs):
    x, m_ref, g_ref, (win_ref, wgrp_ref, scale_ref, wout_ref, o_ref, tail_ref), prefetch, finish = \
        _mixer_input(has_prev, tm, refs)
    i = pl.program_id(0)
    start = (i * tm) % T
    n_pre = 2 * len(POOL_WINDOWS)
    h = _norm_mod(x, g_ref[...], m_ref[1:2, :], m_ref[0:1, :])
    u = _dot(h, win_ref[...])

    @pl.when(start == 0)
    def _():
        tail_ref[...] = jnp.zeros_like(tail_ref)

    pos = start + lax.broadcasted_iota(jnp.int32, (tm, 1), 0)
    C = u.shape[1] // len(POOL_WINDOWS)
    ys = []
    for gi, w in enumerate(POOL_WINDOWS):
        prefetch(2 * gi, n_pre)
        ug = u[:, gi * C:(gi + 1) * C]
        s = jnp.concatenate([tail_ref[:, gi * C:(gi + 1) * C], ug], axis=0)
        k = 1
        while k < w:
            s = s + pltpu.roll(s, k, 0)
            k *= 2
        cnt = jnp.minimum(pos + 1, w).astype(F32)
        pooled = s[POOL_HALO:] / cnt - ug
        prefetch(2 * gi + 1, n_pre)
        ys.append(_dot(pooled, wgrp_ref[gi]))
    tail_ref[...] = u[tm - POOL_HALO:, :]
    y = jnp.concatenate(ys, axis=1) * scale_ref[...]
    o_ref[...] = x + m_ref[2:3, :] * _dot(y, wout_ref[...])
    finish()


def _pool_mixer(x, prev, mod, g, w_in, w_grp, scale, w_out, T):
    D = x.shape[1]
    weights = [w_in.astype(BF16), w_grp.astype(BF16), scale.reshape(1, D), w_out.astype(BF16)]
    return _mixer_call(_pool_kernel, "mixer_pool", TM_POOL, T, x, prev, mod, g, weights,
                       [pltpu.VMEM((POOL_HALO, D), F32)])


def _conv_kernel(has_prev, tm, T, *refs):
    x, m_ref, g_ref, (win_ref, cw_ref, wout_ref, o_ref, tail_ref), prefetch, finish = \
        _mixer_input(has_prev, tm, refs)
    i = pl.program_id(0)
    D = x.shape[1]
    h = _norm_mod(x, g_ref[...], m_ref[1:2, :], m_ref[0:1, :]).astype(BF16)
    prefetch(0, 4)
    gate_b = _dot(h, win_ref[:, :D])
    prefetch(1, 4)
    z = _dot(h, win_ref[:, D:2 * D])
    prefetch(2, 4)
    z = z * _dot(h, win_ref[:, 2 * D:])
    prefetch(3, 4)

    @pl.when((i * tm) % T == 0)
    def _():
        tail_ref[...] = jnp.zeros_like(tail_ref)

    ze = jnp.concatenate([tail_ref[...], z], axis=0)
    zc = cw_ref[CONV_WIDTH - 1:CONV_WIDTH, :] * ze
    for j in range(1, CONV_WIDTH):
        zc = zc + cw_ref[CONV_WIDTH - 1 - j:CONV_WIDTH - j, :] * pltpu.roll(ze, j, 0)
    tail_ref[...] = z[tm - CONV_HALO:, :]
    y = gate_b * zc[CONV_HALO:]
    o_ref[...] = x + m_ref[2:3, :] * _dot(y, wout_ref[...])
    finish()


def _conv_mixer(x, prev, mod, g, w_in, conv_w, w_out, T):
    D = x.shape[1]
    weights = [w_in.astype(BF16), conv_w, w_out.astype(BF16)]
    return _mixer_call(_conv_kernel, "mixer_conv", TM_CONV, T, x, prev, mod, g, weights,
                       [pltpu.VMEM((CONV_HALO, D), F32)])


def _swa_kernel(has_prev, tm, T, *refs):
    x, m_ref, g_ref, (win_ref, wout_ref, sink_ref, o_ref, q_scr, klo, khi, vlo, vhi, o_scr), _, _ = \
        _mixer_input(has_prev, tm, refs, burst=True)
    i = pl.program_id(0)
    D = x.shape[1]
    W = SWA_WINDOW
    hd = D // SWA_HEADS
    G = SWA_HEADS // SWA_KV_HEADS
    kvd = SWA_KV_HEADS * hd
    assert 2 * hd == LANES
    seq_start = (i * tm) % T == 0
    kv_scr = (klo, khi, vlo, vhi)

    @pl.when(seq_start)
    def _():
        for r in kv_scr:
            r[0:W, :] = jnp.zeros((W, r.shape[1]), BF16)

    @pl.when(jnp.logical_not(seq_start))
    def _():
        for r in kv_scr:
            r[0:W, :] = r[tm:tm + W, :]

    h = _norm_mod(x, g_ref[...], m_ref[1:2, :], m_ref[0:1, :])
    qkv = _dot(h, win_ref[...])
    q_scr[...] = (qkv[:, :D] * (hd ** -0.5 * LOG2E)).astype(BF16)
    low = lax.broadcasted_iota(jnp.int32, (tm, LANES), 1) < hd
    for src, lo_ref, hi_ref in ((qkv[:, D:D + kvd], klo, khi), (qkv[:, D + kvd:], vlo, vhi)):
        for a in range(kvd // LANES):
            kg = src[:, a * LANES:(a + 1) * LANES]
            sw = pltpu.roll(kg, hd, 1)
            for par, (lo_v, hi_v) in enumerate(((kg, sw), (sw, kg))):
                c = (2 * a + par) * LANES
                lo_ref[W:, c:c + LANES] = jnp.where(low, lo_v, 0.0).astype(BF16)
                hi_ref[W:, c:c + LANES] = jnp.where(low, 0.0, hi_v).astype(BF16)
    qpos = lax.broadcasted_iota(jnp.int32, (W, 2 * W), 0)
    kj = lax.broadcasted_iota(jnp.int32, (W, 2 * W), 1)
    band = (kj > qpos) & (kj <= qpos + W)
    bias = jnp.where(band, 0.0, -jnp.inf)
    bias0 = jnp.where(band & (kj >= jnp.where(seq_start, W, 0)), 0.0, -jnp.inf)
    for j in range(tm // W):
        b = bias0 if j == 0 else bias
        for grp in range(SWA_HEADS // 2):
            hk = (2 * grp) // G
            qg = q_scr[j * W:(j + 1) * W, grp * LANES:(grp + 1) * LANES]
            pair = None
            for par, (k_ref, v_ref) in enumerate(((klo, vlo), (khi, vhi))):
                kk = k_ref[j * W:(j + 2) * W, hk * LANES:(hk + 1) * LANES]
                vv = v_ref[j * W:(j + 2) * W, hk * LANES:(hk + 1) * LANES]
                sink = sink_ref[2 * grp + par] * LOG2E
                s = lax.dot_general(qg, kk, (((1,), (1,)), ((), ())), preferred_element_type=F32) + b
                mx = jnp.maximum(jnp.max(s, axis=-1, keepdims=True), sink)
                p = jnp.exp2(s - mx)
                denom = jnp.sum(p, axis=-1, keepdims=True) + jnp.exp2(sink - mx)
                o = jnp.dot(p.astype(BF16), vv, preferred_element_type=F32) * (1.0 / denom)
                pair = o if pair is None else pair + o
            o_scr[j * W:(j + 1) * W, grp * LANES:(grp + 1) * LANES] = pair.astype(BF16)
    o_ref[...] = x + m_ref[2:3, :] * jnp.dot(o_scr[...], wout_ref[...], preferred_element_type=F32)


def _swa_mixer(x, prev, mod, g, w_in, sinks, w_out, T):
    D = x.shape[1]
    tm = TM_SWA
    rows = SWA_WINDOW + tm
    kv = pltpu.VMEM((rows, SWA_KV_HEADS * LANES), BF16)
    weights = [w_in.astype(BF16), w_out.astype(BF16)]
    return _mixer_call(_swa_kernel, "mixer_swa", tm, T, x, prev, mod, g, weights,
                       [pltpu.VMEM((tm, D), BF16), kv, kv, kv, kv, pltpu.VMEM((tm, D), BF16)], smem=[sinks])


HGRN_LEVELS = tuple(2 ** e for e in range(HGRN_CHUNK.bit_length() - 1, 0, -1))


def _hgrn_tri():
    tri = np.tril(np.ones((HGRN_CHUNK, HGRN_CHUNK), np.float32))
    return np.concatenate([tri, tri, tri], axis=1)


def _split3(a):
    hi = a.astype(BF16)
    r1 = a - hi.astype(F32)
    mid = r1.astype(BF16)
    lo = (r1 - mid.astype(F32)).astype(BF16)
    return hi, mid, lo


def _hgrn_kernel(layer, has_prev, tm, T, *refs):
    x, m_ref, g_ref, (win_ref, lbl_ref, ng_ref, wout_ref, tri_ref, o_ref, st_ref), _, _ = \
        _mixer_input(has_prev, tm, refs, burst=True)
    i = pl.program_id(0)
    D = x.shape[1]
    dk = HGRN_HEAD_DIM
    C = HGRN_CHUNK

    @pl.when((i * tm) % T == 0)
    def _():
        st_ref[...] = jnp.zeros_like(st_ref)

    rows = [lbl_ref[j:j + 1, :] for j in range(lbl_ref.shape[0])]
    mx = functools.reduce(jnp.maximum, rows)
    es = [jnp.exp(r - mx) for r in rows]
    tot = functools.reduce(lambda a, b: a + b, es)
    lb = jnp.zeros_like(mx)
    for j in range(1, layer + 1):
        lb = lb + es[j] / tot

    h = _norm_mod(x, g_ref[...], m_ref[1:2, :], m_ref[0:1, :]).astype(BF16)
    pq, pf, pv, pg = [_dot(h, win_ref[:, part * D:(part + 1) * D]) for part in range(4)]
    row = lax.broadcasted_iota(jnp.int32, (C, C), 0)
    col = lax.broadcasted_iota(jnp.int32, (C, C), 1)
    pair_masks = [(row // B == col // B) & (row % B >= B // 2) & (col % B < B // 2) for B in HGRN_LEVELS]
    diag = row == col
    trow = lax.broadcasted_iota(jnp.int32, (tm, 1), 0)
    nc = tm // C

    def block_row(a, B, r):
        a3 = a.reshape(tm // B, B, a.shape[1])
        return jnp.broadcast_to(a3[:, r:r + 1, :], a3.shape).reshape(a.shape)

    outs = []
    for hh in range(D // dk):
        sl = slice(hh * dk, (hh + 1) * dk)
        q = _silu(pq[:, sl])
        lbh = lb[:, sl]
        f = lbh + (1.0 - lbh) * _sigmoid(pf[:, sl])
        kk = 1.0 - f
        v = pv[:, sl]
        gate = pg[:, sl]
        lf = jnp.log2(f)
        parts = _split3(lf)
        rhs = jnp.concatenate(
            [jnp.concatenate([p[c * C:(c + 1) * C] for p in parts], axis=0) for c in range(nc)], axis=1)
        bb = jnp.dot(tri_ref[...], rhs, preferred_element_type=F32)
        b = jnp.concatenate([bb[:, c * dk:(c + 1) * dk] for c in range(nc)], axis=0)
        from_start = jnp.exp2(b)
        qs = (q * from_start).astype(BF16)
        ks = (kk * jnp.exp2(block_row(b, C, C - 1) - b)).astype(BF16)
        zs = []
        for B in HGRN_LEVELS:
            pos = trow % B
            if B >= 8:
                e = (b - block_row(b, B, B // 2 - 1)) * jnp.where(pos >= B // 2, 1.0, -1.0)
            elif B == 4:
                e = jnp.where(pos == 0, pltpu.roll(lf, tm - 1, 0),
                              jnp.where(pos == 1, 0.0, jnp.where(pos == 2, lf, lf + pltpu.roll(lf, 1, 0))))
            else:
                e = jnp.where(pos == 1, lf, 0.0)
            zs.append((jnp.where(pos >= B // 2, q, kk) * jnp.exp2(e)).astype(BF16))
        qk = jnp.sum(q * kk, axis=-1, keepdims=True)
        st = st_ref[hh]
        oc = []
        for c in range(nc):
            rs = slice(c * C, (c + 1) * C)
            vc = v[rs].astype(BF16)
            sc = jnp.where(diag, qk[rs], 0.0)
            for z, mask in zip(zs, pair_masks):
                sc = sc + jnp.where(mask, _dot_nt(z[rs], z[rs]), 0.0)
            oc.append(_dot(sc, vc) + _dot_nt(qs[rs], st))
            st = st * from_start[(c + 1) * C - 1:(c + 1) * C, :] + _dot_tn(vc, ks[rs])
        st_ref[hh] = st
        o = jnp.concatenate(oc, axis=0)
        o = o * lax.rsqrt(jnp.mean(o * o, axis=-1, keepdims=True) + EPS) * ng_ref[...]
        outs.append(o * _silu(gate))
    y = jnp.concatenate(outs, axis=1)
    o_ref[...] = x + m_ref[2:3, :] * _dot(y, wout_ref[...])


def _hgrn_mixer(layer, x, prev, mod, g, w_in, lb_logits, norm_g, w_out, T):
    D = x.shape[1]
    dk = HGRN_HEAD_DIM
    weights = [w_in.astype(BF16), lb_logits, norm_g.reshape(1, dk), w_out.astype(BF16),
               jnp.asarray(_hgrn_tri(), BF16)]
    return _mixer_call(functools.partial(_hgrn_kernel, layer), "mixer_hgrn", TM_HGRN, T, x, prev, mod, g, weights,
                       [pltpu.VMEM((D // dk, dk, dk), F32)])


def _split2(a):
    hi = a.astype(BF16)
    lo = (a - hi.astype(F32)).astype(BF16)
    return hi, lo


def _route_kernel(tm, x_ref, m_ref, g_ref, rw_ref, rb_ref, tri_ref, h_ref, cls_ref, rank_ref, cnt_ref, carry_ref):
    i = pl.program_id(0)

    @pl.when(i == 0)
    def _():
        carry_ref[...] = jnp.zeros_like(carry_ref)

    h = _norm_mod(x_ref[...], g_ref[...], m_ref[4:5, :], m_ref[3:4, :])
    _tt_store(h_ref, h)
    h_hi, h_lo = _split2(h)
    w_hi, w_lo = _split2(rw_ref[...])
    logits = (jnp.dot(h_hi, w_hi, preferred_element_type=F32) + jnp.dot(h_lo, w_hi, preferred_element_type=F32)
              + jnp.dot(h_hi, w_lo, preferred_element_type=F32))
    lt = jnp.transpose(logits)[:N_EXPERTS, :]
    score = _sigmoid(lt)
    sel = score + rb_ref[...]
    gscore, gsel = [], []
    for gi in range(N_GROUPS):
        r = [sel[gi * 4 + e:gi * 4 + e + 1, :] for e in range(EXPERTS_PER_GROUP)]
        m1 = functools.reduce(jnp.maximum, r)
        m2 = None
        for a in range(EXPERTS_PER_GROUP):
            for b2 in range(a + 1, EXPERTS_PER_GROUP):
                pm = jnp.minimum(r[a], r[b2])
                m2 = pm if m2 is None else jnp.maximum(m2, pm)
        gscore.append(m1 + m2)
        gsel.append(r)
    best = jnp.zeros((1, tm), jnp.int32)
    bs = gscore[0]
    for gi in range(1, N_GROUPS):
        better = gscore[gi] > bs
        best = jnp.where(better, gi, best)
        bs = jnp.where(better, gscore[gi], bs)
    r = [functools.reduce(lambda a, b2: a + b2,
                          [jnp.where(best == gi, gsel[gi][e], 0.0) for gi in range(N_GROUPS)])
         for e in range(EXPERTS_PER_GROUP)]
    keep = []
    for e in range(EXPERTS_PER_GROUP):
        beaten = jnp.zeros((1, tm), jnp.int32)
        for o in range(EXPERTS_PER_GROUP):
            if o != e:
                wins = (r[o] > r[e]) | ((r[o] == r[e]) & (o < e))
                beaten = beaten + wins.astype(jnp.int32)
        keep.append(beaten < 2)
    pair = jnp.zeros((1, tm), jnp.int32)
    for pi, (a, b2) in enumerate(PAIRS):
        pair = jnp.where(keep[a] & keep[b2], pi, pair)
    cls = best * len(PAIRS) + pair
    cls_ref[...] = cls.reshape(1, 1, tm)
    onehot = (lax.broadcasted_iota(jnp.int32, (CLASS_ROWS, tm), 0) == cls).astype(F32)
    before = jnp.dot(onehot.astype(BF16), tri_ref[...], preferred_element_type=F32) + carry_ref[:, 0:1]
    rank = jnp.sum(onehot * before, axis=0, keepdims=True)
    rank_ref[...] = rank.astype(jnp.int32).reshape(1, 1, tm)
    carry_ref[...] = carry_ref[...] + jnp.sum(onehot, axis=1, keepdims=True)
    cnt_ref[...] = carry_ref[...]


def _route(x1, mod, g, router_w, router_bias, T):
    N, D = x1.shape
    tm = TM_ROUTER
    nt = N // tm
    E = router_w.shape[1]
    rw = jnp.zeros((D, LANES), F32).at[:, :E].set(router_w)
    rb = router_bias.reshape(E, 1)
    tri = (jnp.arange(tm)[:, None] < jnp.arange(tm)[None, :]).astype(BF16)
    row = pl.BlockSpec((tm, D), lambda i: (i, 0))
    modspec = pl.BlockSpec((None, 6, D), lambda i: ((i * tm) // T, 0, 0))
    tok = pl.BlockSpec((1, 1, tm), lambda i: (i, 0, 0))
    h2, cls, rank, cnt = pl.pallas_call(
        functools.partial(_route_kernel, tm),
        grid=(nt,),
        in_specs=[row, modspec, _full((1, D)), _full((D, LANES)), _full((E, 1)), _full((tm, tm))],
        out_specs=[pl.BlockSpec((tm * D // LANES, LANES), lambda i: (i, 0)), tok, tok, _full((CLASS_ROWS, LANES))],
        out_shape=[jax.ShapeDtypeStruct((N * D // LANES, LANES), F32),
                   jax.ShapeDtypeStruct((nt, 1, tm), jnp.int32),
                   jax.ShapeDtypeStruct((nt, 1, tm), jnp.int32),
                   jax.ShapeDtypeStruct((CLASS_ROWS, LANES), F32)],
        scratch_shapes=[pltpu.VMEM((CLASS_ROWS, LANES), F32)],
        compiler_params=_cparams(),
        name="router",
    )(x1, mod, g.reshape(1, D), rw, rb, tri)
    return h2, cls.reshape(N), rank.reshape(N), cnt[:N_CLASSES, 0]


def _tok(ref, t, rpt):
    return ref.at[pl.ds(pl.multiple_of(t * rpt, rpt), rpt), :]


def _scatter_rows_kernel(chunk, rpt, idx_ref, src_ref, init_ref, dst_ref, sem):
    del init_ref
    base = pl.program_id(0) * chunk

    def issue(r2, carry):
        for par in range(2):
            r = 2 * r2 + par
            dst = _tok(dst_ref, idx_ref[base + r], rpt)
            pltpu.make_async_copy(_tok(src_ref, r, rpt), dst, sem).start(priority=par)
        return carry

    lax.fori_loop(0, chunk // 2, issue, 0, unroll=4)
    pltpu.make_async_copy(src_ref, dst_ref.at[pl.ds(0, chunk * rpt), :], sem).wait()


def _scatter_rows(src, idx, n_out, rpt, init=None):
    N = idx.shape[0]
    chunk = min(PERM_CHUNK, N)
    if init is None:
        init = jnp.zeros((n_out * rpt, LANES), src.dtype)
    return pl.pallas_call(
        functools.partial(_scatter_rows_kernel, chunk, rpt),
        grid_spec=pltpu.PrefetchScalarGridSpec(
            num_scalar_prefetch=1, grid=(N // chunk,),
            in_specs=[pl.BlockSpec((chunk * rpt, LANES), lambda i, idx: (i, 0)), pl.BlockSpec(memory_space=pl.ANY)],
            out_specs=pl.BlockSpec(memory_space=pl.ANY),
            scratch_shapes=[pltpu.SemaphoreType.DMA(())]),
        out_shape=jax.ShapeDtypeStruct(init.shape, src.dtype),
        input_output_aliases={2: 0},
        compiler_params=_cparams(),
        name="scatter_rows",
    )(idx, src, init)


def _ffn_kernel(ea_ref, eb_ref, nact_ref, x_ref, rwt_ref, wga_ref, wua_ref, wda_ref, wgb_ref, wub_ref, wdb_ref,
                o_ref):
    j = pl.program_id(0)

    @pl.when(j < nact_ref[0])
    def _():
        x = _tt_load(x_ref, FFN_BLOCK, wga_ref.shape[0])
        sa = _sigmoid(jnp.sum(x * rwt_ref[pl.ds(ea_ref[j], 1), :], axis=-1, keepdims=True))
        sb = _sigmoid(jnp.sum(x * rwt_ref[pl.ds(eb_ref[j], 1), :], axis=-1, keepdims=True))
        inv = 1.0 / (sa + sb)
        xb = x.astype(BF16)
        aa = _silu(_dot(xb, wga_ref[...])) * _dot(xb, wua_ref[...]) * (sa * inv)
        ab = _silu(_dot(xb, wgb_ref[...])) * _dot(xb, wub_ref[...]) * (sb * inv)
        _tt_store(o_ref, _dot(aa, wda_ref[...]) + _dot(ab, wdb_ref[...]))

    @pl.when(j >= nact_ref[0])
    def _():
        o_ref[...] = jnp.zeros_like(o_ref)


def _ffn(xs, blk_ea, blk_eb, n_active, router_wt, w_gate, w_up, w_down):
    D, F = w_gate.shape[1:]
    rpt = D // LANES
    nb = xs.shape[0] // (FFN_BLOCK * rpt)

    def wa(j, ea, eb, na):
        return (ea[j], 0, 0)

    def wb(j, ea, eb, na):
        return (eb[j], 0, 0)

    row = pl.BlockSpec((FFN_BLOCK * rpt, LANES), lambda j, ea, eb, na: (j, 0))
    return pl.pallas_call(
        _ffn_kernel,
        grid_spec=pltpu.PrefetchScalarGridSpec(
            num_scalar_prefetch=3, grid=(nb,),
            in_specs=[row, pl.BlockSpec(router_wt.shape, lambda j, ea, eb, na: (0, 0)),
                      pl.BlockSpec((None, D, F), wa), pl.BlockSpec((None, D, F), wa), pl.BlockSpec((None, F, D), wa),
                      pl.BlockSpec((None, D, F), wb), pl.BlockSpec((None, D, F), wb), pl.BlockSpec((None, F, D), wb)],
            out_specs=row),
        out_shape=jax.ShapeDtypeStruct(xs.shape, F32),
        compiler_params=_cparams(),
        name="moe_ffn",
    )(blk_ea, blk_eb, n_active, xs, router_wt, w_gate, w_up, w_down, w_gate, w_up, w_down)


def _moe(layer, x1, mod, g, router_w, router_bias, w_gate, w_up, w_down, T, xs_spare):
    N, D = x1.shape
    h2, cls, rank, cnt = _route(x1, mod, g, router_w, router_bias, T)
    counts = cnt.astype(jnp.int32)
    padded = (counts + FFN_BLOCK - 1) // FFN_BLOCK * FFN_BLOCK
    ends = jnp.cumsum(padded)
    starts = ends - padded
    dest = starts[cls] + rank
    nb = N // FFN_BLOCK + N_CLASSES
    n_active = (ends[-1] // FFN_BLOCK).astype(jnp.int32)
    blk = jnp.arange(nb, dtype=jnp.int32)
    blk_start = jnp.minimum(blk, n_active - 1) * FFN_BLOCK
    blk_cls = jnp.sum((ends[None, :] <= blk_start[:, None]).astype(jnp.int32), axis=1)
    blk_cls = jnp.minimum(blk_cls, N_CLASSES - 1)
    pair_a = jnp.array([p[0] for p in PAIRS], jnp.int32)
    pair_b = jnp.array([p[1] for p in PAIRS], jnp.int32)
    grp = blk_cls // len(PAIRS)
    blk_ea = grp * EXPERTS_PER_GROUP + pair_a[blk_cls % len(PAIRS)]
    blk_eb = grp * EXPERTS_PER_GROUP + pair_b[blk_cls % len(PAIRS)]
    rpt = D // LANES
    xs = _scatter_rows(h2, dest, nb * FFN_BLOCK, rpt, xs_spare)
    ys = _ffn(xs, blk_ea, blk_eb, n_active.reshape(1), jnp.transpose(router_w),
              _cast_bf16(w_gate, layer), _cast_bf16(w_up, layer), _cast_bf16(w_down, layer))
    return ys, dest, xs


def _final_kernel(tm, dest_ref, x_ref, ys_ref, pm_ref, g_ref, o_ref, buf, sem):
    x = x_ref[...]
    y, _, _ = _gathered_rows(dest_ref, ys_ref, buf, sem, tm, x.shape[1], burst=True)
    x = x + pm_ref[5:6, :] * y
    ms = jnp.mean(x * x, axis=-1, keepdims=True)
    o_ref[...] = x * lax.rsqrt(ms + EPS) * g_ref[...]


def _final(x, prev, g, T):
    ys, dest, mod = prev
    N, D = x.shape
    tm = TM_FINAL
    row = pl.BlockSpec((tm, D), lambda i, *_: (i, 0))
    modspec = pl.BlockSpec((None, 6, D), lambda i, *_: ((i * tm) // T, 0, 0))
    return pl.pallas_call(
        functools.partial(_final_kernel, tm),
        grid_spec=pltpu.PrefetchScalarGridSpec(
            num_scalar_prefetch=1, grid=(N // tm,),
            in_specs=[row, pl.BlockSpec(memory_space=pl.ANY), modspec, _full((1, D))],
            out_specs=row,
            scratch_shapes=[pltpu.VMEM((2, tm * D // LANES, LANES), F32), pltpu.SemaphoreType.DMA((2,))]),
        out_shape=jax.ShapeDtypeStruct((N, D), F32),
        compiler_params=_cparams(),
        name="final_norm",
    )(dest, x, ys, mod, g.reshape(1, D))


def kernel(x, c, ada_w, ada_b, norm_g, final_norm_g, pool_w_in, pool_w_grp, pool_scale, pool_w_out, hgrn_w_in, hgrn_lb_logits, hgrn_norm_g, hgrn_w_out, swa_w_in, swa_sinks, swa_w_out, conv_w_in, conv_w, conv_w_out, router_w, router_bias, moe_w_gate, moe_w_up, moe_w_down):
    B, T, D = x.shape
    depth = ada_w.shape[0]
    n_mixers = 4
    assert D % LANES == 0 and all(T % tm == 0 for tm in (TM_POOL, TM_CONV, TM_SWA, TM_HGRN, TM_ROUTER, TM_FINAL))
    assert TM_SWA % SWA_WINDOW == 0 and TM_HGRN % HGRN_CHUNK == 0 and (B * T) % min(PERM_CHUNK, B * T) == 0
    mod = _ada(c, ada_w, ada_b).reshape(depth, B, 6, D)
    xt = x.reshape(B * T, D)
    prev = None
    xs_spare = None
    for i in range(depth):
        m, j = i % n_mixers, i // n_mixers
        g1n = norm_g[i, 0]
        if m == 0:
            x1 = _pool_mixer(xt, prev, mod[i], g1n, pool_w_in[j], pool_w_grp[j], pool_scale[j], pool_w_out[j], T)
        elif m == 1:
            x1 = _hgrn_mixer(i, xt, prev, mod[i], g1n, hgrn_w_in[j], hgrn_lb_logits, hgrn_norm_g[j],
                             hgrn_w_out[j], T)
        elif m == 2:
            x1 = _swa_mixer(xt, prev, mod[i], g1n, swa_w_in[j], swa_sinks[j], swa_w_out[j], T)
        else:
            x1 = _conv_mixer(xt, prev, mod[i], g1n, conv_w_in[j], conv_w[j], conv_w_out[j], T)
        ys, dest, xs_spare = _moe(i, x1, mod[i], norm_g[i, 1], router_w, router_bias,
                                  moe_w_gate, moe_w_up, moe_w_down, T, xs_spare)
        xt = x1
        prev = (ys, dest, mod[i])
    out = _final(xt, prev, final_norm_g, T)
    return out.reshape(B, T, D)
```

```python
import functools

import jax
import jax.numpy as jnp
import numpy as np
from jax import lax
from jax.experimental import pallas as pl
from jax.experimental.pallas import tpu as pltpu

F32 = jnp.float32
BF16 = jnp.bfloat16
EPS = 1e-6
LOG2E = 1.4426950408889634

POOL_WINDOWS = (2, 4, 8, 16)
POOL_HALO = 16
HGRN_HEAD_DIM = 128
HGRN_CHUNK = 64
SWA_HEADS = 16
SWA_KV_HEADS = 4
SWA_WINDOW = 128
CONV_WIDTH = 3
CONV_HALO = 8
N_EXPERTS = 16
N_GROUPS = 4
EXPERTS_PER_GROUP = 4
PAIRS = ((0, 1), (0, 2), (0, 3), (1, 2), (1, 3), (2, 3))
N_CLASSES = N_GROUPS * len(PAIRS)
CLASS_ROWS = 32
LANES = 128
FFN_BLOCK = 256
PERM_CHUNK = 2048
VMEM_LIMIT = 52 * 1024 * 1024
TM_POOL = 512
TM_CONV = 512
TM_SWA = 256
TM_HGRN = 256
TM_ROUTER = 512
TM_FINAL = 512


def _cparams():
    return pltpu.CompilerParams(dimension_semantics=("arbitrary",), vmem_limit_bytes=VMEM_LIMIT)


def _full(shape):
    nd = len(shape)
    return pl.BlockSpec(shape, lambda i, *_: (0,) * nd)


def _norm_mod(x, g, sc, sh):
    ms = jnp.mean(x * x, axis=-1, keepdims=True)
    return x * lax.rsqrt(ms + EPS) * (g * (1.0 + sc)) + sh


def _sigmoid(x):
    return 1.0 / (1.0 + jnp.exp(-x))


def _silu(x):
    return x * _sigmoid(x)


def _dot(a, b):
    return jnp.dot(a.astype(BF16), b.astype(BF16), preferred_element_type=F32)


def _dot_nt(a, b):
    return lax.dot_general(a.astype(BF16), b.astype(BF16), (((1,), (1,)), ((), ())), preferred_element_type=F32)


def _dot_tn(a, b):
    return lax.dot_general(a.astype(BF16), b.astype(BF16), (((0,), (0,)), ((), ())), preferred_element_type=F32)


def _tt_load(ref, rows, d):
    n = d // LANES
    return jnp.concatenate([ref[pl.ds(j, rows, stride=n), :] for j in range(n)], axis=1)


def _tt_store(ref, val):
    rows, d = val.shape
    n = d // LANES
    for j in range(n):
        ref[pl.ds(j, rows, stride=n), :] = val[:, j * LANES:(j + 1) * LANES]


def _cast_kernel(w_ref, o_ref):
    o_ref[...] = w_ref[...].astype(BF16)


def _cast_bf16(w, layer):
    _, E, K, M = w.shape
    return pl.pallas_call(
        _cast_kernel,
        grid=(E,),
        in_specs=[pl.BlockSpec((None, None, K, M), lambda e: (layer, e, 0, 0))],
        out_specs=pl.BlockSpec((None, K, M), lambda e: (e, 0, 0)),
        out_shape=jax.ShapeDtypeStruct((E, K, M), BF16),
        compiler_params=_cparams(),
        name="cast_bf16",
    )(w)


def _ada_kernel(c_ref, w_ref, b_ref, o_ref):
    cond = _silu(c_ref[...])
    o_ref[...] = jnp.dot(cond, w_ref[...], preferred_element_type=F32, precision=lax.Precision.HIGHEST) + b_ref[...]


def _ada(c, ada_w, ada_b):
    L, D, D6 = ada_w.shape
    B = c.shape[0]
    bn = D6 // 4
    return pl.pallas_call(
        _ada_kernel,
        grid=(L, D6 // bn),
        in_specs=[pl.BlockSpec((B, D), lambda l, j: (0, 0)),
                  pl.BlockSpec((None, D, bn), lambda l, j: (l, 0, j)),
                  pl.BlockSpec((None, 1, bn), lambda l, j: (l, 0, j))],
        out_specs=pl.BlockSpec((None, B, bn), lambda l, j: (l, 0, j)),
        out_shape=jax.ShapeDtypeStruct((L, B, D6), F32),
        compiler_params=pltpu.CompilerParams(dimension_semantics=("arbitrary", "arbitrary"),
                                             vmem_limit_bytes=VMEM_LIMIT),
        name="ada_mod",
    )(c, ada_w, ada_b.reshape(L, 1, D6))


def _mixer_call(body, name, tm, T, x, prev, mod, g, weights, scratch, smem=()):
    N, D = x.shape
    rpt = D // LANES
    row = pl.BlockSpec((tm, D), lambda i, *_: (i, 0))
    modspec = pl.BlockSpec((None, 6, D), lambda i, *_: ((i * tm) // T, 0, 0))
    args, specs, prefetch = [x], [row], []
    scratch = list(scratch)
    if prev is not None:
        prefetch = [prev[1]]
        args += [prev[0], prev[2]]
        specs += [pl.BlockSpec(memory_space=pl.ANY), modspec]
        scratch += [pltpu.VMEM((2, tm * rpt, LANES), F32), pltpu.SemaphoreType.DMA((2,))]
    args += [mod, g.reshape(1, D)]
    specs += [modspec, _full((1, D))]
    for w in weights:
        args.append(w)
        specs.append(_full(w.shape))
    for s in smem:
        args.append(s)
        specs.append(pl.BlockSpec(memory_space=pltpu.SMEM))
    return pl.pallas_call(
        functools.partial(body, prev is not None, tm, T),
        grid_spec=pltpu.PrefetchScalarGridSpec(
            num_scalar_prefetch=len(prefetch), grid=(N // tm,), in_specs=specs, out_specs=row,
            scratch_shapes=scratch),
        out_shape=jax.ShapeDtypeStruct((N, D), F32),
        compiler_params=_cparams(),
        name=name,
    )(*prefetch, *args)


def _gathered_rows(dest_ref, ys_ref, buf, sem, tm, d, burst=False):
    i = pl.program_id(0)
    last = pl.num_programs(0) - 1
    rpt = d // LANES

    def issue(tile, slot, r, priority):
        src = _tok(ys_ref, dest_ref[tile * tm + r], rpt)
        pltpu.make_async_copy(src, _tok(buf.at[slot], r, rpt), sem.at[slot]).start(priority=priority)

    def issue_all(tile, slot):
        def pair(r2, carry):
            issue(tile, slot, 2 * r2, 0)
            issue(tile, slot, 2 * r2 + 1, 1)
            return carry
        lax.fori_loop(0, tm // 2, pair, 0, unroll=4)

    def wait(slot):
        pltpu.make_async_copy(ys_ref.at[pl.ds(0, tm * rpt), :], buf.at[slot], sem.at[slot]).wait()

    @pl.when(i == 0)
    def _():
        issue_all(0, 0)

    if burst:
        @pl.when(i < last)
        def _():
            issue_all(i + 1, (i + 1) % 2)

        wait(i % 2)
        return _tt_load(buf.at[i % 2], tm, d), (lambda k, n: None), (lambda: None)

    nxt = jnp.minimum(i + 1, last)

    def prefetch(k, n):
        for r in range(k * (tm // n), (k + 1) * (tm // n)):
            issue(nxt, (i + 1) % 2, r, r % 2)

    def finish():
        @pl.when(i == last)
        def _():
            wait((i + 1) % 2)

    wait(i % 2)
    return _tt_load(buf.at[i % 2], tm, d), prefetch, finish


def _mixer_input(has_prev, tm, refs, burst=False):
    if has_prev:
        dest_ref, x_ref, ys_ref, pm_ref, m_ref, g_ref = refs[:6]
        buf, sem = refs[-2:]
        x = x_ref[...]
        y, prefetch, finish = _gathered_rows(dest_ref, ys_ref, buf, sem, tm, x.shape[1], burst)
        x = x + pm_ref[5:6, :] * y
        rest = refs[6:-2]
    else:
        x_ref, m_ref, g_ref = refs[:3]
        x = x_ref[...]
        rest = refs[3:]
        prefetch = lambda k, n: None
        finish = lambda: None
    return x, m_ref, g_ref, rest, prefetch, finish


def _pool_kernel(has_prev, tm, T, *refs):
    x, m_ref, g_ref, (win_ref, wgrp_ref, scale_ref, wout_ref, o_ref, tail_ref), prefetch, finish = \
        _mixer_input(has_prev, tm, refs)
    i = pl.program_id(0)
    start = (i * tm) % T
    n_pre = 2 * len(POOL_WINDOWS)
    h = _norm_mod(x, g_ref[...], m_ref[1:2, :], m_ref[0:1, :])
    u = _dot(h, win_ref[...])

    @pl.when(start == 0)
    def _():
        tail_ref[...] = jnp.zeros_like(tail_ref)

    pos = start + lax.broadcasted_iota(jnp.int32, (tm, 1), 0)
    C = u.shape[1] // len(POOL_WINDOWS)
    ys = []
    for gi, w in enumerate(POOL_WINDOWS):
        prefetch(2 * gi, n_pre)
        ug = u[:, gi * C:(gi + 1) * C]
        s = jnp.concatenate([tail_ref[:, gi * C:(gi + 1) * C], ug], axis=0)
        k = 1
        while k < w:
            s = s + pltpu.roll(s, k, 0)
            k *= 2
        cnt = jnp.minimum(pos + 1, w).astype(F32)
        pooled = s[POOL_HALO:] / cnt - ug
        prefetch(2 * gi + 1, n_pre)
        ys.append(_dot(pooled, wgrp_ref[gi]))
    tail_ref[...] = u[tm - POOL_HALO:, :]
    y = jnp.concatenate(ys, axis=1) * scale_ref[...]
    o_ref[...] = x + m_ref[2:3, :] * _dot(y, wout_ref[...])
    finish()


def _pool_mixer(x, prev, mod, g, w_in, w_grp, scale, w_out, T):
    D = x.shape[1]
    weights = [w_in.astype(BF16), w_grp.astype(BF16), scale.reshape(1, D), w_out.astype(BF16)]
    return _mixer_call(_pool_kernel, "mixer_pool", TM_POOL, T, x, prev, mod, g, weights,
                       [pltpu.VMEM((POOL_HALO, D), F32)])


def _conv_kernel(has_prev, tm, T, *refs):
    x, m_ref, g_ref, (win_ref, cw_ref, wout_ref, o_ref, tail_ref), prefetch, finish = \
        _mixer_input(has_prev, tm, refs)
    i = pl.program_id(0)
    D = x.shape[1]
    h = _norm_mod(x, g_ref[...], m_ref[1:2, :], m_ref[0:1, :]).astype(BF16)
    prefetch(0, 4)
    gate_b = _dot(h, win_ref[:, :D])
    prefetch(1, 4)
    z = _dot(h, win_ref[:, D:2 * D])
    prefetch(2, 4)
    z = z * _dot(h, win_ref[:, 2 * D:])
    prefetch(3, 4)

    @pl.when((i * tm) % T == 0)
    def _():
        tail_ref[...] = jnp.zeros_like(tail_ref)

    ze = jnp.concatenate([tail_ref[...], z], axis=0)
    zc = cw_ref[CONV_WIDTH - 1:CONV_WIDTH, :] * ze
    for j in range(1, CONV_WIDTH):
        zc = zc + cw_ref[CONV_WIDTH - 1 - j:CONV_WIDTH - j, :] * pltpu.roll(ze, j, 0)
    tail_ref[...] = z[tm - CONV_HALO:, :]
    y = gate_b * zc[CONV_HALO:]
    o_ref[...] = x + m_ref[2:3, :] * _dot(y, wout_ref[...])
    finish()


def _conv_mixer(x, prev, mod, g, w_in, conv_w, w_out, T):
    D = x.shape[1]
    weights = [w_in.astype(BF16), conv_w, w_out.astype(BF16)]
    return _mixer_call(_conv_kernel, "mixer_conv", TM_CONV, T, x, prev, mod, g, weights,
                       [pltpu.VMEM((CONV_HALO, D), F32)])


def _swa_kernel(has_prev, tm, T, *refs):
    x, m_ref, g_ref, (win_ref, wout_ref, sink_ref, o_ref, q_scr, klo, khi, vlo, vhi, o_scr), _, _ = \
        _mixer_input(has_prev, tm, refs, burst=True)
    i = pl.program_id(0)
    D = x.shape[1]
    W = SWA_WINDOW
    hd = D // SWA_HEADS
    G = SWA_HEADS // SWA_KV_HEADS
    kvd = SWA_KV_HEADS * hd
    assert 2 * hd == LANES
    seq_start = (i * tm) % T == 0
    kv_scr = (klo, khi, vlo, vhi)

    @pl.when(seq_start)
    def _():
        for r in kv_scr:
            r[0:W, :] = jnp.zeros((W, r.shape[1]), BF16)

    @pl.when(jnp.logical_not(seq_start))
    def _():
        for r in kv_scr:
            r[0:W, :] = r[tm:tm + W, :]

    h = _norm_mod(x, g_ref[...], m_ref[1:2, :], m_ref[0:1, :])
    qkv = _dot(h, win_ref[...])
    q_scr[...] = (qkv[:, :D] * (hd ** -0.5 * LOG2E)).astype(BF16)
    low = lax.broadcasted_iota(jnp.int32, (tm, LANES), 1) < hd
    for src, lo_ref, hi_ref in ((qkv[:, D:D + kvd], klo, khi), (qkv[:, D + kvd:], vlo, vhi)):
        for a in range(kvd // LANES):
            kg = src[:, a * LANES:(a + 1) * LANES]
            sw = pltpu.roll(kg, hd, 1)
            for par, (lo_v, hi_v) in enumerate(((kg, sw), (sw, kg))):
                c = (2 * a + par) * LANES
                lo_ref[W:, c:c + LANES] = jnp.where(low, lo_v, 0.0).astype(BF16)
                hi_ref[W:, c:c + LANES] = jnp.where(low, 0.0, hi_v).astype(BF16)
    qpos = lax.broadcasted_iota(jnp.int32, (W, 2 * W), 0)
    kj = lax.broadcasted_iota(jnp.int32, (W, 2 * W), 1)
    band = (kj > qpos) & (kj <= qpos + W)
    bias = jnp.where(band, 0.0, -jnp.inf)
    bias0 = jnp.where(band & (kj >= jnp.where(seq_start, W, 0)), 0.0, -jnp.inf)
    for j in range(tm // W):
        b = bias0 if j == 0 else bias
        for grp in range(SWA_HEADS // 2):
            hk = (2 * grp) // G
            qg = q_scr[j * W:(j + 1) * W, grp * LANES:(grp + 1) * LANES]
            pair = None
            for par, (k_ref, v_ref) in enumerate(((klo, vlo), (khi, vhi))):
                kk = k_ref[j * W:(j + 2) * W, hk * LANES:(hk + 1) * LANES]
                vv = v_ref[j * W:(j + 2) * W, hk * LANES:(hk + 1) * LANES]
                sink = sink_ref[2 * grp + par] * LOG2E
                s = lax.dot_general(qg, kk, (((1,), (1,)), ((), ())), preferred_element_type=F32) + b
                mx = jnp.maximum(jnp.max(s, axis=-1, keepdims=True), sink)
                p = jnp.exp2(s - mx)
                denom = jnp.sum(p, axis=-1, keepdims=True) + jnp.exp2(sink - mx)
                o = jnp.dot(p.astype(BF16), vv, preferred_element_type=F32) * (1.0 / denom)
                pair = o if pair is None else pair + o
            o_scr[j * W:(j + 1) * W, grp * LANES:(grp + 1) * LANES] = pair.astype(BF16)
    o_ref[...] = x + m_ref[2:3, :] * jnp.dot(o_scr[...], wout_ref[...], preferred_element_type=F32)


def _swa_mixer(x, prev, mod, g, w_in, sinks, w_out, T):
    D = x.shape[1]
    tm = TM_SWA
    rows = SWA_WINDOW + tm
    kv = pltpu.VMEM((rows, SWA_KV_HEADS * LANES), BF16)
    weights = [w_in.astype(BF16), w_out.astype(BF16)]
    return _mixer_call(_swa_kernel, "mixer_swa", tm, T, x, prev, mod, g, weights,
                       [pltpu.VMEM((tm, D), BF16), kv, kv, kv, kv, pltpu.VMEM((tm, D), BF16)], smem=[sinks])


HGRN_LEVELS = tuple(2 ** e for e in range(HGRN_CHUNK.bit_length() - 1, 0, -1))


def _hgrn_tri():
    tri = np.tril(np.ones((HGRN_CHUNK, HGRN_CHUNK), np.float32))
    return np.concatenate([tri, tri, tri], axis=1)


def _split3(a):
    hi = a.astype(BF16)
    r1 = a - hi.astype(F32)
    mid = r1.astype(BF16)
    lo = (r1 - mid.astype(F32)).astype(BF16)
    return hi, mid, lo


def _hgrn_kernel(layer, has_prev, tm, T, *refs):
    x, m_ref, g_ref, (win_ref, lbl_ref, ng_ref, wout_ref, tri_ref, o_ref, st_ref), prefetch, finish = \
        _mixer_input(has_prev, tm, refs)
    i = pl.program_id(0)
    D = x.shape[1]
    dk = HGRN_HEAD_DIM
    C = HGRN_CHUNK

    @pl.when((i * tm) % T == 0)
    def _():
        st_ref[...] = jnp.zeros_like(st_ref)

    rows = [lbl_ref[j:j + 1, :] for j in range(lbl_ref.shape[0])]
    mx = functools.reduce(jnp.maximum, rows)
    es = [jnp.exp(r - mx) for r in rows]
    tot = functools.reduce(lambda a, b: a + b, es)
    lb = jnp.zeros_like(mx)
    for j in range(1, layer + 1):
        lb = lb + es[j] / tot

    h = _norm_mod(x, g_ref[...], m_ref[1:2, :], m_ref[0:1, :]).astype(BF16)
    proj = _dot(h, win_ref[...])
    pq, pf, pv, pg = [proj[:, part * D:(part + 1) * D] for part in range(4)]
    row = lax.broadcasted_iota(jnp.int32, (C, C), 0)
    col = lax.broadcasted_iota(jnp.int32, (C, C), 1)
    pair_masks = [(row // B == col // B) & (row % B >= B // 2) & (col % B < B // 2) for B in HGRN_LEVELS]
    diag = row == col
    trow = lax.broadcasted_iota(jnp.int32, (tm, 1), 0)
    nc = tm // C

    def block_row(a, B, r):
        a3 = a.reshape(tm // B, B, a.shape[1])
        return jnp.broadcast_to(a3[:, r:r + 1, :], a3.shape).reshape(a.shape)

    outs = []
    for hh in range(D // dk):
        sl = slice(hh * dk, (hh + 1) * dk)
        q = _silu(pq[:, sl])
        lbh = lb[:, sl]
        f = lbh + (1.0 - lbh) * _sigmoid(pf[:, sl])
        kk = 1.0 - f
        v = pv[:, sl]
        gate = pg[:, sl]
        lf = jnp.log2(f)
        parts = _split3(lf)
        rhs = jnp.concatenate(
            [jnp.concatenate([p[c * C:(c + 1) * C] for p in parts], axis=0) for c in range(nc)], axis=1)
        bb = jnp.dot(tri_ref[...], rhs, preferred_element_type=F32)
        b = jnp.concatenate([bb[:, c * dk:(c + 1) * dk] for c in range(nc)], axis=0)
        from_start = jnp.exp2(b)
        qs = (q * from_start).astype(BF16)
        ks = (kk * jnp.exp2(block_row(b, C, C - 1) - b)).astype(BF16)
        zs = []
        for B in HGRN_LEVELS:
            pos = trow % B
            if B >= 8:
                e = (b - block_row(b, B, B // 2 - 1)) * jnp.where(pos >= B // 2, 1.0, -1.0)
            elif B == 4:
                e = jnp.where(pos == 0, pltpu.roll(lf, tm - 1, 0),
                              jnp.where(pos == 1, 0.0, jnp.where(pos == 2, lf, lf + pltpu.roll(lf, 1, 0))))
            else:
                e = jnp.where(pos == 1, lf, 0.0)
            zs.append((jnp.where(pos >= B // 2, q, kk) * jnp.exp2(e)).astype(BF16))
        qk = jnp.sum(q * kk, axis=-1, keepdims=True)
        st = st_ref[hh]
        oc = []
        for c in range(nc):
            prefetch(hh * nc + c, (D // dk) * nc)
            rs = slice(c * C, (c + 1) * C)
            vc = v[rs].astype(BF16)
            sc = jnp.where(diag, qk[rs], 0.0)
            for z, mask in zip(zs, pair_masks):
                sc = sc + jnp.where(mask, _dot_nt(z[rs], z[rs]), 0.0)
            oc.append(_dot(sc, vc) + _dot_nt(qs[rs], st))
            st = st * from_start[(c + 1) * C - 1:(c + 1) * C, :] + _dot_tn(vc, ks[rs])
        st_ref[hh] = st
        o = jnp.concatenate(oc, axis=0)
        o = o * lax.rsqrt(jnp.mean(o * o, axis=-1, keepdims=True) + EPS) * ng_ref[...]
        outs.append(o * _silu(gate))
    y = jnp.concatenate(outs, axis=1)
    o_ref[...] = x + m_ref[2:3, :] * _dot(y, wout_ref[...])
    finish()


def _hgrn_mixer(layer, x, prev, mod, g, w_in, lb_logits, norm_g, w_out, T):
    D = x.shape[1]
    dk = HGRN_HEAD_DIM
    weights = [w_in.astype(BF16), lb_logits, norm_g.reshape(1, dk), w_out.astype(BF16),
               jnp.asarray(_hgrn_tri(), BF16)]
    return _mixer_call(functools.partial(_hgrn_kernel, layer), "mixer_hgrn", TM_HGRN, T, x, prev, mod, g, weights,
                       [pltpu.VMEM((D // dk, dk, dk), F32)])


def _split2(a):
    hi = a.astype(BF16)
    lo = (a - hi.astype(F32)).astype(BF16)
    return hi, lo


def _route_kernel(tm, x_ref, m_ref, g_ref, rw_ref, rb_ref, tri_ref, h_ref, cls_ref, rank_ref, cnt_ref, carry_ref):
    i = pl.program_id(0)

    @pl.when(i == 0)
    def _():
        carry_ref[...] = jnp.zeros_like(carry_ref)

    h = _norm_mod(x_ref[...], g_ref[...], m_ref[4:5, :], m_ref[3:4, :])
    _tt_store(h_ref, h)
    h_hi, h_lo = _split2(h)
    w_hi, w_lo = _split2(rw_ref[...])
    logits = (jnp.dot(h_hi, w_hi, preferred_element_type=F32) + jnp.dot(h_lo, w_hi, preferred_element_type=F32)
              + jnp.dot(h_hi, w_lo, preferred_element_type=F32))
    lt = jnp.transpose(logits)[:N_EXPERTS, :]
    score = _sigmoid(lt)
    sel = score + rb_ref[...]
    gscore, gsel = [], []
    for gi in range(N_GROUPS):
        r = [sel[gi * 4 + e:gi * 4 + e + 1, :] for e in range(EXPERTS_PER_GROUP)]
        m1 = functools.reduce(jnp.maximum, r)
        m2 = None
        for a in range(EXPERTS_PER_GROUP):
            for b2 in range(a + 1, EXPERTS_PER_GROUP):
                pm = jnp.minimum(r[a], r[b2])
                m2 = pm if m2 is None else jnp.maximum(m2, pm)
        gscore.append(m1 + m2)
        gsel.append(r)
    best = jnp.zeros((1, tm), jnp.int32)
    bs = gscore[0]
    for gi in range(1, N_GROUPS):
        better = gscore[gi] > bs
        best = jnp.where(better, gi, best)
        bs = jnp.where(better, gscore[gi], bs)
    r = [functools.reduce(lambda a, b2: a + b2,
                          [jnp.where(best == gi, gsel[gi][e], 0.0) for gi in range(N_GROUPS)])
         for e in range(EXPERTS_PER_GROUP)]
    keep = []
    for e in range(EXPERTS_PER_GROUP):
        beaten = jnp.zeros((1, tm), jnp.int32)
        for o in range(EXPERTS_PER_GROUP):
            if o != e:
                wins = (r[o] > r[e]) | ((r[o] == r[e]) & (o < e))
                beaten = beaten + wins.astype(jnp.int32)
        keep.append(beaten < 2)
    pair = jnp.zeros((1, tm), jnp.int32)
    for pi, (a, b2) in enumerate(PAIRS):
        pair = jnp.where(keep[a] & keep[b2], pi, pair)
    cls = best * len(PAIRS) + pair
    cls_ref[...] = cls.reshape(1, 1, tm)
    onehot = (lax.broadcasted_iota(jnp.int32, (CLASS_ROWS, tm), 0) == cls).astype(F32)
    before = jnp.dot(onehot.astype(BF16), tri_ref[...], preferred_element_type=F32) + carry_ref[:, 0:1]
    rank = jnp.sum(onehot * before, axis=0, keepdims=True)
    rank_ref[...] = rank.astype(jnp.int32).reshape(1, 1, tm)
    carry_ref[...] = carry_ref[...] + jnp.sum(onehot, axis=1, keepdims=True)
    cnt_ref[...] = carry_ref[...]


def _route(x1, mod, g, router_w, router_bias, T):
    N, D = x1.shape
    tm = TM_ROUTER
    nt = N // tm
    E = router_w.shape[1]
    rw = jnp.zeros((D, LANES), F32).at[:, :E].set(router_w)
    rb = router_bias.reshape(E, 1)
    tri = (jnp.arange(tm)[:, None] < jnp.arange(tm)[None, :]).astype(BF16)
    row = pl.BlockSpec((tm, D), lambda i: (i, 0))
    modspec = pl.BlockSpec((None, 6, D), lambda i: ((i * tm) // T, 0, 0))
    tok = pl.BlockSpec((1, 1, tm), lambda i: (i, 0, 0))
    h2, cls, rank, cnt = pl.pallas_call(
        functools.partial(_route_kernel, tm),
        grid=(nt,),
        in_specs=[row, modspec, _full((1, D)), _full((D, LANES)), _full((E, 1)), _full((tm, tm))],
        out_specs=[pl.BlockSpec((tm * D // LANES, LANES), lambda i: (i, 0)), tok, tok, _full((CLASS_ROWS, LANES))],
        out_shape=[jax.ShapeDtypeStruct((N * D // LANES, LANES), F32),
                   jax.ShapeDtypeStruct((nt, 1, tm), jnp.int32),
                   jax.ShapeDtypeStruct((nt, 1, tm), jnp.int32),
                   jax.ShapeDtypeStruct((CLASS_ROWS, LANES), F32)],
        scratch_shapes=[pltpu.VMEM((CLASS_ROWS, LANES), F32)],
        compiler_params=_cparams(),
        name="router",
    )(x1, mod, g.reshape(1, D), rw, rb, tri)
    return h2, cls.reshape(N), rank.reshape(N), cnt[:N_CLASSES, 0]


def _tok(ref, t, rpt):
    return ref.at[pl.ds(pl.multiple_of(t * rpt, rpt), rpt), :]


def _scatter_rows_kernel(chunk, rpt, idx_ref, src_ref, init_ref, dst_ref, sem):
    del init_ref
    base = pl.program_id(0) * chunk

    def issue(r2, carry):
        for par in range(2):
            r = 2 * r2 + par
            dst = _tok(dst_ref, idx_ref[base + r], rpt)
            pltpu.make_async_copy(_tok(src_ref, r, rpt), dst, sem).start(priority=par)
        return carry

    lax.fori_loop(0, chunk // 2, issue, 0, unroll=4)
    pltpu.make_async_copy(src_ref, dst_ref.at[pl.ds(0, chunk * rpt), :], sem).wait()


def _scatter_rows(src, idx, n_out, rpt, init=None):
    N = idx.shape[0]
    chunk = min(PERM_CHUNK, N)
    if init is None:
        init = jnp.zeros((n_out * rpt, LANES), src.dtype)
    return pl.pallas_call(
        functools.partial(_scatter_rows_kernel, chunk, rpt),
        grid_spec=pltpu.PrefetchScalarGridSpec(
            num_scalar_prefetch=1, grid=(N // chunk,),
            in_specs=[pl.BlockSpec((chunk * rpt, LANES), lambda i, idx: (i, 0)), pl.BlockSpec(memory_space=pl.ANY)],
            out_specs=pl.BlockSpec(memory_space=pl.ANY),
            scratch_shapes=[pltpu.SemaphoreType.DMA(())]),
        out_shape=jax.ShapeDtypeStruct(init.shape, src.dtype),
        input_output_aliases={2: 0},
        compiler_params=_cparams(),
        name="scatter_rows",
    )(idx, src, init)


def _ffn_kernel(ea_ref, eb_ref, nact_ref, x_ref, rwt_ref, wga_ref, wua_ref, wda_ref, wgb_ref, wub_ref, wdb_ref,
                o_ref):
    j = pl.program_id(0)

    @pl.when(j < nact_ref[0])
    def _():
        x = _tt_load(x_ref, FFN_BLOCK, wga_ref.shape[0])
        sa = _sigmoid(jnp.sum(x * rwt_ref[pl.ds(ea_ref[j], 1), :], axis=-1, keepdims=True))
        sb = _sigmoid(jnp.sum(x * rwt_ref[pl.ds(eb_ref[j], 1), :], axis=-1, keepdims=True))
        inv = 1.0 / (sa + sb)
        xb = x.astype(BF16)
        aa = _silu(_dot(xb, wga_ref[...])) * _dot(xb, wua_ref[...]) * (sa * inv)
        ab = _silu(_dot(xb, wgb_ref[...])) * _dot(xb, wub_ref[...]) * (sb * inv)
        _tt_store(o_ref, _dot(aa, wda_ref[...]) + _dot(ab, wdb_ref[...]))

    @pl.when(j >= nact_ref[0])
    def _():
        o_ref[...] = jnp.zeros_like(o_ref)


def _ffn(xs, blk_ea, blk_eb, n_active, router_wt, w_gate, w_up, w_down):
    D, F = w_gate.shape[1:]
    rpt = D // LANES
    nb = xs.shape[0] // (FFN_BLOCK * rpt)

    def wa(j, ea, eb, na):
        return (ea[j], 0, 0)

    def wb(j, ea, eb, na):
        return (eb[j], 0, 0)

    row = pl.BlockSpec((FFN_BLOCK * rpt, LANES), lambda j, ea, eb, na: (j, 0))
    return pl.pallas_call(
        _ffn_kernel,
        grid_spec=pltpu.PrefetchScalarGridSpec(
            num_scalar_prefetch=3, grid=(nb,),
            in_specs=[row, pl.BlockSpec(router_wt.shape, lambda j, ea, eb, na: (0, 0)),
                      pl.BlockSpec((None, D, F), wa), pl.BlockSpec((None, D, F), wa), pl.BlockSpec((None, F, D), wa),
                      pl.BlockSpec((None, D, F), wb), pl.BlockSpec((None, D, F), wb), pl.BlockSpec((None, F, D), wb)],
            out_specs=row),
        out_shape=jax.ShapeDtypeStruct(xs.shape, F32),
        compiler_params=_cparams(),
        name="moe_ffn",
    )(blk_ea, blk_eb, n_active, xs, router_wt, w_gate, w_up, w_down, w_gate, w_up, w_down)


def _moe(layer, x1, mod, g, router_w, router_bias, w_gate, w_up, w_down, T, xs_spare):
    N, D = x1.shape
    h2, cls, rank, cnt = _route(x1, mod, g, router_w, router_bias, T)
    counts = cnt.astype(jnp.int32)
    padded = (counts + FFN_BLOCK - 1) // FFN_BLOCK * FFN_BLOCK
    ends = jnp.cumsum(padded)
    starts = ends - padded
    dest = starts[cls] + rank
    nb = N // FFN_BLOCK + N_CLASSES
    n_active = (ends[-1] // FFN_BLOCK).astype(jnp.int32)
    blk = jnp.arange(nb, dtype=jnp.int32)
    blk_start = jnp.minimum(blk, n_active - 1) * FFN_BLOCK
    blk_cls = jnp.sum((ends[None, :] <= blk_start[:, None]).astype(jnp.int32), axis=1)
    blk_cls = jnp.minimum(blk_cls, N_CLASSES - 1)
    pair_a = jnp.array([p[0] for p in PAIRS], jnp.int32)
    pair_b = jnp.array([p[1] for p in PAIRS], jnp.int32)
    grp = blk_cls // len(PAIRS)
    blk_ea = grp * EXPERTS_PER_GROUP + pair_a[blk_cls % len(PAIRS)]
    blk_eb = grp * EXPERTS_PER_GROUP + pair_b[blk_cls % len(PAIRS)]
    rpt = D // LANES
    xs = _scatter_rows(h2, dest, nb * FFN_BLOCK, rpt, xs_spare)
    ys = _ffn(xs, blk_ea, blk_eb, n_active.reshape(1), jnp.transpose(router_w),
              _cast_bf16(w_gate, layer), _cast_bf16(w_up, layer), _cast_bf16(w_down, layer))
    return ys, dest, xs


def _final_kernel(tm, dest_ref, x_ref, ys_ref, pm_ref, g_ref, o_ref, buf, sem):
    x = x_ref[...]
    y, _, _ = _gathered_rows(dest_ref, ys_ref, buf, sem, tm, x.shape[1], burst=True)
    x = x + pm_ref[5:6, :] * y
    ms = jnp.mean(x * x, axis=-1, keepdims=True)
    o_ref[...] = x * lax.rsqrt(ms + EPS) * g_ref[...]


def _final(x, prev, g, T):
    ys, dest, mod = prev
    N, D = x.shape
    tm = TM_FINAL
    row = pl.BlockSpec((tm, D), lambda i, *_: (i, 0))
    modspec = pl.BlockSpec((None, 6, D), lambda i, *_: ((i * tm) // T, 0, 0))
    return pl.pallas_call(
        functools.partial(_final_kernel, tm),
        grid_spec=pltpu.PrefetchScalarGridSpec(
            num_scalar_prefetch=1, grid=(N // tm,),
            in_specs=[row, pl.BlockSpec(memory_space=pl.ANY), modspec, _full((1, D))],
            out_specs=row,
            scratch_shapes=[pltpu.VMEM((2, tm * D // LANES, LANES), F32), pltpu.SemaphoreType.DMA((2,))]),
        out_shape=jax.ShapeDtypeStruct((N, D), F32),
        compiler_params=_cparams(),
        name="final_norm",
    )(dest, x, ys, mod, g.reshape(1, D))


def kernel(x, c, ada_w, ada_b, norm_g, final_norm_g, pool_w_in, pool_w_grp, pool_scale, pool_w_out, hgrn_w_in, hgrn_lb_logits, hgrn_norm_g, hgrn_w_out, swa_w_in, swa_sinks, swa_w_out, conv_w_in, conv_w, conv_w_out, router_w, router_bias, moe_w_gate, moe_w_up, moe_w_down):
    B, T, D = x.shape
    depth = ada_w.shape[0]
    n_mixers = 4
    assert D % LANES == 0 and all(T % tm == 0 for tm in (TM_POOL, TM_CONV, TM_SWA, TM_HGRN, TM_ROUTER, TM_FINAL))
    assert TM_SWA % SWA_WINDOW == 0 and TM_HGRN % HGRN_CHUNK == 0 and (B * T) % min(PERM_CHUNK, B * T) == 0
    mod = _ada(c, ada_w, ada_b).reshape(depth, B, 6, D)
    xt = x.reshape(B * T, D)
    prev = None
    xs_spare = None
    for i in range(depth):
        m, j = i % n_mixers, i // n_mixers
        g1n = norm_g[i, 0]
        if m == 0:
            x1 = _pool_mixer(xt, prev, mod[i], g1n, pool_w_in[j], pool_w_grp[j], pool_scale[j], pool_w_out[j], T)
        elif m == 1:
            x1 = _hgrn_mixer(i, xt, prev, mod[i], g1n, hgrn_w_in[j], hgrn_lb_logits, hgrn_norm_g[j],
                             hgrn_w_out[j], T)
        elif m == 2:
            x1 = _swa_mixer(xt, prev, mod[i], g1n, swa_w_in[j], swa_sinks[j], swa_w_out[j], T)
        else:
            x1 = _conv_mixer(xt, prev, mod[i], g1n, conv_w_in[j], conv_w[j], conv_w_out[j], T)
        ys, dest, xs_spare = _moe(i, x1, mod[i], norm_g[i, 1], router_w, router_bias,
                                  moe_w_gate, moe_w_up, moe_w_down, T, xs_spare)
        xt = x1
        prev = (ys, dest, mod[i])
    out = _final(xt, prev, final_norm_g, T)
    return out.reshape(B, T, D)
```

```python
import functools

import jax
import jax.numpy as jnp
import numpy as np
from jax import lax
from jax.experimental import pallas as pl
from jax.experimental.pallas import tpu as pltpu

F32 = jnp.float32
BF16 = jnp.bfloat16
EPS = 1e-6
LOG2E = 1.4426950408889634

POOL_WINDOWS = (2, 4, 8, 16)
POOL_HALO = 16
HGRN_HEAD_DIM = 128
HGRN_CHUNK = 64
SWA_HEADS = 16
SWA_KV_HEADS = 4
SWA_WINDOW = 128
CONV_WIDTH = 3
CONV_HALO = 8
N_EXPERTS = 16
N_GROUPS = 4
EXPERTS_PER_GROUP = 4
PAIRS = ((0, 1), (0, 2), (0, 3), (1, 2), (1, 3), (2, 3))
N_CLASSES = N_GROUPS * len(PAIRS)
CLASS_ROWS = 32
LANES = 128
FFN_BLOCK = 256
PERM_CHUNK = 2048
VMEM_LIMIT = 52 * 1024 * 1024
TM_POOL = 512
TM_CONV = 512
TM_SWA = 256
TM_HGRN = 256
TM_ROUTER = 1024
TM_FINAL = 1024


def _cparams():
    return pltpu.CompilerParams(dimension_semantics=("arbitrary",), vmem_limit_bytes=VMEM_LIMIT)


def _full(shape):
    nd = len(shape)
    return pl.BlockSpec(shape, lambda i, *_: (0,) * nd)


def _norm_mod(x, g, sc, sh):
    ms = jnp.mean(x * x, axis=-1, keepdims=True)
    return x * lax.rsqrt(ms + EPS) * (g * (1.0 + sc)) + sh


def _sigmoid(x):
    return 1.0 / (1.0 + jnp.exp(-x))


def _silu(x):
    return x * _sigmoid(x)


def _dot(a, b):
    return jnp.dot(a.astype(BF16), b.astype(BF16), preferred_element_type=F32)


def _dot_nt(a, b):
    return lax.dot_general(a.astype(BF16), b.astype(BF16), (((1,), (1,)), ((), ())), preferred_element_type=F32)


def _dot_tn(a, b):
    return lax.dot_general(a.astype(BF16), b.astype(BF16), (((0,), (0,)), ((), ())), preferred_element_type=F32)


def _tt_load(ref, rows, d):
    n = d // LANES
    return jnp.concatenate([ref[pl.ds(j, rows, stride=n), :] for j in range(n)], axis=1)


def _tt_store(ref, val):
    rows, d = val.shape
    n = d // LANES
    for j in range(n):
        ref[pl.ds(j, rows, stride=n), :] = val[:, j * LANES:(j + 1) * LANES]


def _cast_kernel(w_ref, o_ref):
    o_ref[...] = w_ref[...].astype(BF16)


def _cast_bf16(w, layer):
    _, E, K, M = w.shape
    return pl.pallas_call(
        _cast_kernel,
        grid=(E,),
        in_specs=[pl.BlockSpec((None, None, K, M), lambda e: (layer, e, 0, 0))],
        out_specs=pl.BlockSpec((None, K, M), lambda e: (e, 0, 0)),
        out_shape=jax.ShapeDtypeStruct((E, K, M), BF16),
        compiler_params=_cparams(),
        name="cast_bf16",
    )(w)


def _ada_kernel(c_ref, w_ref, b_ref, o_ref):
    cond = _silu(c_ref[...])
    o_ref[...] = jnp.dot(cond, w_ref[...], preferred_element_type=F32, precision=lax.Precision.HIGHEST) + b_ref[...]


def _ada(c, ada_w, ada_b):
    L, D, D6 = ada_w.shape
    B = c.shape[0]
    bn = D6 // 4
    return pl.pallas_call(
        _ada_kernel,
        grid=(L, D6 // bn),
        in_specs=[pl.BlockSpec((B, D), lambda l, j: (0, 0)),
                  pl.BlockSpec((None, D, bn), lambda l, j: (l, 0, j)),
                  pl.BlockSpec((None, 1, bn), lambda l, j: (l, 0, j))],
        out_specs=pl.BlockSpec((None, B, bn), lambda l, j: (l, 0, j)),
        out_shape=jax.ShapeDtypeStruct((L, B, D6), F32),
        compiler_params=pltpu.CompilerParams(dimension_semantics=("arbitrary", "arbitrary"),
                                             vmem_limit_bytes=VMEM_LIMIT),
        name="ada_mod",
    )(c, ada_w, ada_b.reshape(L, 1, D6))


def _mixer_call(body, name, tm, T, x, prev, mod, g, weights, scratch, smem=()):
    N, D = x.shape
    rpt = D // LANES
    row = pl.BlockSpec((tm, D), lambda i, *_: (i, 0))
    modspec = pl.BlockSpec((None, 6, D), lambda i, *_: ((i * tm) // T, 0, 0))
    args, specs, prefetch = [x], [row], []
    scratch = list(scratch)
    if prev is not None:
        prefetch = [prev[1]]
        args += [prev[0], prev[2]]
        specs += [pl.BlockSpec(memory_space=pl.ANY), modspec]
        scratch += [pltpu.VMEM((2, tm * rpt, LANES), F32), pltpu.SemaphoreType.DMA((2,))]
    args += [mod, g.reshape(1, D)]
    specs += [modspec, _full((1, D))]
    for w in weights:
        args.append(w)
        specs.append(_full(w.shape))
    for s in smem:
        args.append(s)
        specs.append(pl.BlockSpec(memory_space=pltpu.SMEM))
    return pl.pallas_call(
        functools.partial(body, prev is not None, tm, T),
        grid_spec=pltpu.PrefetchScalarGridSpec(
            num_scalar_prefetch=len(prefetch), grid=(N // tm,), in_specs=specs, out_specs=row,
            scratch_shapes=scratch),
        out_shape=jax.ShapeDtypeStruct((N, D), F32),
        compiler_params=_cparams(),
        name=name,
    )(*prefetch, *args)


def _gathered_rows(dest_ref, ys_ref, buf, sem, tm, d, burst=False):
    i = pl.program_id(0)
    last = pl.num_programs(0) - 1
    rpt = d // LANES

    def issue(tile, slot, r, priority):
        src = _tok(ys_ref, dest_ref[tile * tm + r], rpt)
        pltpu.make_async_copy(src, _tok(buf.at[slot], r, rpt), sem.at[slot]).start(priority=priority)

    def issue_all(tile, slot):
        def pair(r2, carry):
            issue(tile, slot, 2 * r2, 0)
            issue(tile, slot, 2 * r2 + 1, 1)
            return carry
        lax.fori_loop(0, tm // 2, pair, 0, unroll=4)

    def wait(slot):
        pltpu.make_async_copy(ys_ref.at[pl.ds(0, tm * rpt), :], buf.at[slot], sem.at[slot]).wait()

    @pl.when(i == 0)
    def _():
        issue_all(0, 0)

    if burst:
        @pl.when(i < last)
        def _():
            issue_all(i + 1, (i + 1) % 2)

        wait(i % 2)
        return _tt_load(buf.at[i % 2], tm, d), (lambda k, n: None), (lambda: None)

    nxt = jnp.minimum(i + 1, last)

    def prefetch(k, n):
        for r in range(k * (tm // n), (k + 1) * (tm // n)):
            issue(nxt, (i + 1) % 2, r, r % 2)

    def finish():
        @pl.when(i == last)
        def _():
            wait((i + 1) % 2)

    wait(i % 2)
    return _tt_load(buf.at[i % 2], tm, d), prefetch, finish


def _mixer_input(has_prev, tm, refs, burst=False):
    if has_prev:
        dest_ref, x_ref, ys_ref, pm_ref, m_ref, g_ref = refs[:6]
        buf, sem = refs[-2:]
        x = x_ref[...]
        y, prefetch, finish = _gathered_rows(dest_ref, ys_ref, buf, sem, tm, x.shape[1], burst)
        x = x + pm_ref[5:6, :] * y
        rest = refs[6:-2]
    else:
        x_ref, m_ref, g_ref = refs[:3]
        x = x_ref[...]
        rest = refs[3:]
        prefetch = lambda k, n: None
        finish = lambda: None
    return x, m_ref, g_ref, rest, prefetch, finish


def _pool_kernel(has_prev, tm, T, *refs):
    x, m_ref, g_ref, (win_ref, wgrp_ref, scale_ref, wout_ref, o_ref, tail_ref), prefetch, finish = \
        _mixer_input(has_prev, tm, refs)
    i = pl.program_id(0)
    start = (i * tm) % T
    n_pre = 2 * len(POOL_WINDOWS)
    h = _norm_mod(x, g_ref[...], m_ref[1:2, :], m_ref[0:1, :])
    u = _dot(h, win_ref[...])

    @pl.when(start == 0)
    def _():
        tail_ref[...] = jnp.zeros_like(tail_ref)

    pos = start + lax.broadcasted_iota(jnp.int32, (tm, 1), 0)
    C = u.shape[1] // len(POOL_WINDOWS)
    ys = []
    for gi, w in enumerate(POOL_WINDOWS):
        prefetch(2 * gi, n_pre)
        ug = u[:, gi * C:(gi + 1) * C]
        s = jnp.concatenate([tail_ref[:, gi * C:(gi + 1) * C], ug], axis=0)
        k = 1
        while k < w:
            s = s + pltpu.roll(s, k, 0)
            k *= 2
        cnt = jnp.minimum(pos + 1, w).astype(F32)
        pooled = s[POOL_HALO:] / cnt - ug
        prefetch(2 * gi + 1, n_pre)
        ys.append(_dot(pooled, wgrp_ref[gi]))
    tail_ref[...] = u[tm - POOL_HALO:, :]
    y = jnp.concatenate(ys, axis=1) * scale_ref[...]
    o_ref[...] = x + m_ref[2:3, :] * _dot(y, wout_ref[...])
    finish()


def _pool_mixer(x, prev, mod, g, w_in, w_grp, scale, w_out, T):
    D = x.shape[1]
    weights = [w_in.astype(BF16), w_grp.astype(BF16), scale.reshape(1, D), w_out.astype(BF16)]
    return _mixer_call(_pool_kernel, "mixer_pool", TM_POOL, T, x, prev, mod, g, weights,
                       [pltpu.VMEM((POOL_HALO, D), F32)])


def _conv_kernel(has_prev, tm, T, *refs):
    x, m_ref, g_ref, (win_ref, cw_ref, wout_ref, o_ref, tail_ref), prefetch, finish = \
        _mixer_input(has_prev, tm, refs)
    i = pl.program_id(0)
    D = x.shape[1]
    h = _norm_mod(x, g_ref[...], m_ref[1:2, :], m_ref[0:1, :]).astype(BF16)
    prefetch(0, 4)
    gate_b = _dot(h, win_ref[:, :D])
    prefetch(1, 4)
    z = _dot(h, win_ref[:, D:2 * D])
    prefetch(2, 4)
    z = z * _dot(h, win_ref[:, 2 * D:])
    prefetch(3, 4)

    @pl.when((i * tm) % T == 0)
    def _():
        tail_ref[...] = jnp.zeros_like(tail_ref)

    ze = jnp.concatenate([tail_ref[...], z], axis=0)
    zc = cw_ref[CONV_WIDTH - 1:CONV_WIDTH, :] * ze
    for j in range(1, CONV_WIDTH):
        zc = zc + cw_ref[CONV_WIDTH - 1 - j:CONV_WIDTH - j, :] * pltpu.roll(ze, j, 0)
    tail_ref[...] = z[tm - CONV_HALO:, :]
    y = gate_b * zc[CONV_HALO:]
    o_ref[...] = x + m_ref[2:3, :] * _dot(y, wout_ref[...])
    finish()


def _conv_mixer(x, prev, mod, g, w_in, conv_w, w_out, T):
    D = x.shape[1]
    weights = [w_in.astype(BF16), conv_w, w_out.astype(BF16)]
    return _mixer_call(_conv_kernel, "mixer_conv", TM_CONV, T, x, prev, mod, g, weights,
                       [pltpu.VMEM((CONV_HALO, D), F32)])


def _swa_kernel(has_prev, tm, T, *refs):
    x, m_ref, g_ref, (win_ref, wout_ref, sink_ref, o_ref, q_scr, klo, khi, vlo, vhi, o_scr), _, _ = \
        _mixer_input(has_prev, tm, refs, burst=True)
    i = pl.program_id(0)
    D = x.shape[1]
    W = SWA_WINDOW
    hd = D // SWA_HEADS
    G = SWA_HEADS // SWA_KV_HEADS
    kvd = SWA_KV_HEADS * hd
    assert 2 * hd == LANES
    seq_start = (i * tm) % T == 0
    kv_scr = (klo, khi, vlo, vhi)

    @pl.when(seq_start)
    def _():
        for r in kv_scr:
            r[0:W, :] = jnp.zeros((W, r.shape[1]), BF16)

    @pl.when(jnp.logical_not(seq_start))
    def _():
        for r in kv_scr:
            r[0:W, :] = r[tm:tm + W, :]

    h = _norm_mod(x, g_ref[...], m_ref[1:2, :], m_ref[0:1, :])
    qkv = _dot(h, win_ref[...])
    q_scr[...] = (qkv[:, :D] * (hd ** -0.5 * LOG2E)).astype(BF16)
    low = lax.broadcasted_iota(jnp.int32, (tm, LANES), 1) < hd
    for src, lo_ref, hi_ref in ((qkv[:, D:D + kvd], klo, khi), (qkv[:, D + kvd:], vlo, vhi)):
        for a in range(kvd // LANES):
            kg = src[:, a * LANES:(a + 1) * LANES]
            sw = pltpu.roll(kg, hd, 1)
            for par, (lo_v, hi_v) in enumerate(((kg, sw), (sw, kg))):
                c = (2 * a + par) * LANES
                lo_ref[W:, c:c + LANES] = jnp.where(low, lo_v, 0.0).astype(BF16)
                hi_ref[W:, c:c + LANES] = jnp.where(low, 0.0, hi_v).astype(BF16)
    qpos = lax.broadcasted_iota(jnp.int32, (W, 2 * W), 0)
    kj = lax.broadcasted_iota(jnp.int32, (W, 2 * W), 1)
    band = (kj > qpos) & (kj <= qpos + W)
    bias = jnp.where(band, 0.0, -jnp.inf)
    bias0 = jnp.where(band & (kj >= jnp.where(seq_start, W, 0)), 0.0, -jnp.inf)
    for j in range(tm // W):
        b = bias0 if j == 0 else bias
        for grp in range(SWA_HEADS // 2):
            hk = (2 * grp) // G
            qg = q_scr[j * W:(j + 1) * W, grp * LANES:(grp + 1) * LANES]
            pair = None
            for par, (k_ref, v_ref) in enumerate(((klo, vlo), (khi, vhi))):
                kk = k_ref[j * W:(j + 2) * W, hk * LANES:(hk + 1) * LANES]
                vv = v_ref[j * W:(j + 2) * W, hk * LANES:(hk + 1) * LANES]
                sink = sink_ref[2 * grp + par] * LOG2E
                s = lax.dot_general(qg, kk, (((1,), (1,)), ((), ())), preferred_element_type=F32) + b
                mx = jnp.maximum(jnp.max(s, axis=-1, keepdims=True), sink)
                p = jnp.exp2(s - mx)
                denom = jnp.sum(p, axis=-1, keepdims=True) + jnp.exp2(sink - mx)
                o = jnp.dot(p.astype(BF16), vv, preferred_element_type=F32) * (1.0 / denom)
                pair = o if pair is None else pair + o
            o_scr[j * W:(j + 1) * W, grp * LANES:(grp + 1) * LANES] = pair.astype(BF16)
    o_ref[...] = x + m_ref[2:3, :] * jnp.dot(o_scr[...], wout_ref[...], preferred_element_type=F32)


def _swa_mixer(x, prev, mod, g, w_in, sinks, w_out, T):
    D = x.shape[1]
    tm = TM_SWA
    rows = SWA_WINDOW + tm
    kv = pltpu.VMEM((rows, SWA_KV_HEADS * LANES), BF16)
    weights = [w_in.astype(BF16), w_out.astype(BF16)]
    return _mixer_call(_swa_kernel, "mixer_swa", tm, T, x, prev, mod, g, weights,
                       [pltpu.VMEM((tm, D), BF16), kv, kv, kv, kv, pltpu.VMEM((tm, D), BF16)], smem=[sinks])


HGRN_LEVELS = tuple(2 ** e for e in range(HGRN_CHUNK.bit_length() - 1, 0, -1))


def _hgrn_tri():
    tri = np.tril(np.ones((HGRN_CHUNK, HGRN_CHUNK), np.float32))
    return np.concatenate([tri, tri, tri], axis=1)


def _split3(a):
    hi = a.astype(BF16)
    r1 = a - hi.astype(F32)
    mid = r1.astype(BF16)
    lo = (r1 - mid.astype(F32)).astype(BF16)
    return hi, mid, lo


def _hgrn_kernel(layer, has_prev, tm, T, *refs):
    x, m_ref, g_ref, (win_ref, lbl_ref, ng_ref, wout_ref, tri_ref, o_ref, st_ref), prefetch, finish = \
        _mixer_input(has_prev, tm, refs)
    i = pl.program_id(0)
    D = x.shape[1]
    dk = HGRN_HEAD_DIM
    C = HGRN_CHUNK

    @pl.when((i * tm) % T == 0)
    def _():
        st_ref[...] = jnp.zeros_like(st_ref)

    rows = [lbl_ref[j:j + 1, :] for j in range(lbl_ref.shape[0])]
    mx = functools.reduce(jnp.maximum, rows)
    es = [jnp.exp(r - mx) for r in rows]
    tot = functools.reduce(lambda a, b: a + b, es)
    lb = jnp.zeros_like(mx)
    for j in range(1, layer + 1):
        lb = lb + es[j] / tot

    h = _norm_mod(x, g_ref[...], m_ref[1:2, :], m_ref[0:1, :]).astype(BF16)
    proj = _dot(h, win_ref[...])
    pq, pf, pv, pg = [proj[:, part * D:(part + 1) * D] for part in range(4)]
    row = lax.broadcasted_iota(jnp.int32, (C, C), 0)
    col = lax.broadcasted_iota(jnp.int32, (C, C), 1)
    pair_masks = [(row // B == col // B) & (row % B >= B // 2) & (col % B < B // 2) for B in HGRN_LEVELS]
    diag = row == col
    trow = lax.broadcasted_iota(jnp.int32, (tm, 1), 0)
    nc = tm // C

    def block_row(a, B, r):
        a3 = a.reshape(tm // B, B, a.shape[1])
        return jnp.broadcast_to(a3[:, r:r + 1, :], a3.shape).reshape(a.shape)

    outs = []
    for hh in range(D // dk):
        sl = slice(hh * dk, (hh + 1) * dk)
        q = _silu(pq[:, sl])
        lbh = lb[:, sl]
        f = lbh + (1.0 - lbh) * _sigmoid(pf[:, sl])
        kk = 1.0 - f
        v = pv[:, sl]
        gate = pg[:, sl]
        lf = jnp.log2(f)
        parts = _split3(lf)
        rhs = jnp.concatenate(
            [jnp.concatenate([p[c * C:(c + 1) * C] for p in parts], axis=0) for c in range(nc)], axis=1)
        bb = jnp.dot(tri_ref[...], rhs, preferred_element_type=F32)
        b = jnp.concatenate([bb[:, c * dk:(c + 1) * dk] for c in range(nc)], axis=0)
        from_start = jnp.exp2(b)
        qs = (q * from_start).astype(BF16)
        ks = (kk * jnp.exp2(block_row(b, C, C - 1) - b)).astype(BF16)
        zs = []
        for B in HGRN_LEVELS:
            pos = trow % B
            if B >= 8:
                e = (b - block_row(b, B, B // 2 - 1)) * jnp.where(pos >= B // 2, 1.0, -1.0)
            elif B == 4:
                e = jnp.where(pos == 0, pltpu.roll(lf, tm - 1, 0),
                              jnp.where(pos == 1, 0.0, jnp.where(pos == 2, lf, lf + pltpu.roll(lf, 1, 0))))
            else:
                e = jnp.where(pos == 1, lf, 0.0)
            zs.append((jnp.where(pos >= B // 2, q, kk) * jnp.exp2(e)).astype(BF16))
        qk = jnp.sum(q * kk, axis=-1, keepdims=True)
        st = st_ref[hh]
        oc = []
        for c in range(nc):
            prefetch(hh * nc + c, (D // dk) * nc)
            rs = slice(c * C, (c + 1) * C)
            vc = v[rs].astype(BF16)
            sc = jnp.where(diag, qk[rs], 0.0)
            for z, mask in zip(zs, pair_masks):
                sc = sc + jnp.where(mask, _dot_nt(z[rs], z[rs]), 0.0)
            oc.append(_dot(sc, vc) + _dot_nt(qs[rs], st))
            st = st * from_start[(c + 1) * C - 1:(c + 1) * C, :] + _dot_tn(vc, ks[rs])
        st_ref[hh] = st
        o = jnp.concatenate(oc, axis=0)
        o = o * lax.rsqrt(jnp.mean(o * o, axis=-1, keepdims=True) + EPS) * ng_ref[...]
        outs.append(o * _silu(gate))
    y = jnp.concatenate(outs, axis=1)
    o_ref[...] = x + m_ref[2:3, :] * _dot(y, wout_ref[...])
    finish()


def _hgrn_mixer(layer, x, prev, mod, g, w_in, lb_logits, norm_g, w_out, T):
    D = x.shape[1]
    dk = HGRN_HEAD_DIM
    weights = [w_in.astype(BF16), lb_logits, norm_g.reshape(1, dk), w_out.astype(BF16),
               jnp.asarray(_hgrn_tri(), BF16)]
    return _mixer_call(functools.partial(_hgrn_kernel, layer), "mixer_hgrn", TM_HGRN, T, x, prev, mod, g, weights,
                       [pltpu.VMEM((D // dk, dk, dk), F32)])


def _split2(a):
    hi = a.astype(BF16)
    lo = (a - hi.astype(F32)).astype(BF16)
    return hi, lo


def _route_kernel(tm, x_ref, m_ref, g_ref, rw_ref, rb_ref, tri_ref, h_ref, cls_ref, rank_ref, cnt_ref, carry_ref):
    i = pl.program_id(0)

    @pl.when(i == 0)
    def _():
        carry_ref[...] = jnp.zeros_like(carry_ref)

    h = _norm_mod(x_ref[...], g_ref[...], m_ref[4:5, :], m_ref[3:4, :])
    _tt_store(h_ref, h)
    h_hi, h_lo = _split2(h)
    w_hi, w_lo = _split2(rw_ref[...])
    logits = (jnp.dot(h_hi, w_hi, preferred_element_type=F32) + jnp.dot(h_lo, w_hi, preferred_element_type=F32)
              + jnp.dot(h_hi, w_lo, preferred_element_type=F32))
    lt = jnp.transpose(logits)[:N_EXPERTS, :]
    score = _sigmoid(lt)
    sel = score + rb_ref[...]
    gscore, gsel = [], []
    for gi in range(N_GROUPS):
        r = [sel[gi * 4 + e:gi * 4 + e + 1, :] for e in range(EXPERTS_PER_GROUP)]
        m1 = functools.reduce(jnp.maximum, r)
        m2 = None
        for a in range(EXPERTS_PER_GROUP):
            for b2 in range(a + 1, EXPERTS_PER_GROUP):
                pm = jnp.minimum(r[a], r[b2])
                m2 = pm if m2 is None else jnp.maximum(m2, pm)
        gscore.append(m1 + m2)
        gsel.append(r)
    best = jnp.zeros((1, tm), jnp.int32)
    bs = gscore[0]
    for gi in range(1, N_GROUPS):
        better = gscore[gi] > bs
        best = jnp.where(better, gi, best)
        bs = jnp.where(better, gscore[gi], bs)
    r = [functools.reduce(lambda a, b2: a + b2,
                          [jnp.where(best == gi, gsel[gi][e], 0.0) for gi in range(N_GROUPS)])
         for e in range(EXPERTS_PER_GROUP)]
    keep = []
    for e in range(EXPERTS_PER_GROUP):
        beaten = jnp.zeros((1, tm), jnp.int32)
        for o in range(EXPERTS_PER_GROUP):
            if o != e:
                wins = (r[o] > r[e]) | ((r[o] == r[e]) & (o < e))
                beaten = beaten + wins.astype(jnp.int32)
        keep.append(beaten < 2)
    pair = jnp.zeros((1, tm), jnp.int32)
    for pi, (a, b2) in enumerate(PAIRS):
        pair = jnp.where(keep[a] & keep[b2], pi, pair)
    cls = best * len(PAIRS) + pair
    cls_ref[...] = cls.reshape(1, 1, tm)
    onehot = (lax.broadcasted_iota(jnp.int32, (CLASS_ROWS, tm), 0) == cls).astype(F32)
    before = jnp.dot(onehot.astype(BF16), tri_ref[...], preferred_element_type=F32) + carry_ref[:, 0:1]
    rank = jnp.sum(onehot * before, axis=0, keepdims=True)
    rank_ref[...] = rank.astype(jnp.int32).reshape(1, 1, tm)
    carry_ref[...] = carry_ref[...] + jnp.sum(onehot, axis=1, keepdims=True)
    cnt_ref[...] = carry_ref[...]


def _route(x1, mod, g, router_w, router_bias, T):
    N, D = x1.shape
    tm = TM_ROUTER
    nt = N // tm
    E = router_w.shape[1]
    rw = jnp.zeros((D, LANES), F32).at[:, :E].set(router_w)
    rb = router_bias.reshape(E, 1)
    tri = (jnp.arange(tm)[:, None] < jnp.arange(tm)[None, :]).astype(BF16)
    row = pl.BlockSpec((tm, D), lambda i: (i, 0))
    modspec = pl.BlockSpec((None, 6, D), lambda i: ((i * tm) // T, 0, 0))
    tok = pl.BlockSpec((1, 1, tm), lambda i: (i, 0, 0))
    h2, cls, rank, cnt = pl.pallas_call(
        functools.partial(_route_kernel, tm),
        grid=(nt,),
        in_specs=[row, modspec, _full((1, D)), _full((D, LANES)), _full((E, 1)), _full((tm, tm))],
        out_specs=[pl.BlockSpec((tm * D // LANES, LANES), lambda i: (i, 0)), tok, tok, _full((CLASS_ROWS, LANES))],
        out_shape=[jax.ShapeDtypeStruct((N * D // LANES, LANES), F32),
                   jax.ShapeDtypeStruct((nt, 1, tm), jnp.int32),
                   jax.ShapeDtypeStruct((nt, 1, tm), jnp.int32),
                   jax.ShapeDtypeStruct((CLASS_ROWS, LANES), F32)],
        scratch_shapes=[pltpu.VMEM((CLASS_ROWS, LANES), F32)],
        compiler_params=_cparams(),
        name="router",
    )(x1, mod, g.reshape(1, D), rw, rb, tri)
    return h2, cls.reshape(N), rank.reshape(N), cnt[:N_CLASSES, 0]


def _tok(ref, t, rpt):
    return ref.at[pl.ds(pl.multiple_of(t * rpt, rpt), rpt), :]


def _scatter_rows_kernel(chunk, rpt, idx_ref, src_ref, init_ref, dst_ref, sem):
    del init_ref
    base = pl.program_id(0) * chunk

    def issue(r2, carry):
        for par in range(2):
            r = 2 * r2 + par
            dst = _tok(dst_ref, idx_ref[base + r], rpt)
            pltpu.make_async_copy(_tok(src_ref, r, rpt), dst, sem).start(priority=par)
        return carry

    lax.fori_loop(0, chunk // 2, issue, 0, unroll=4)
    pltpu.make_async_copy(src_ref, dst_ref.at[pl.ds(0, chunk * rpt), :], sem).wait()


def _scatter_rows(src, idx, n_out, rpt, init=None):
    N = idx.shape[0]
    chunk = min(PERM_CHUNK, N)
    if init is None:
        init = jnp.zeros((n_out * rpt, LANES), src.dtype)
    return pl.pallas_call(
        functools.partial(_scatter_rows_kernel, chunk, rpt),
        grid_spec=pltpu.PrefetchScalarGridSpec(
            num_scalar_prefetch=1, grid=(N // chunk,),
            in_specs=[pl.BlockSpec((chunk * rpt, LANES), lambda i, idx: (i, 0)), pl.BlockSpec(memory_space=pl.ANY)],
            out_specs=pl.BlockSpec(memory_space=pl.ANY),
            scratch_shapes=[pltpu.SemaphoreType.DMA(())]),
        out_shape=jax.ShapeDtypeStruct(init.shape, src.dtype),
        input_output_aliases={2: 0},
        compiler_params=_cparams(),
        name="scatter_rows",
    )(idx, src, init)


def _ffn_kernel(ea_ref, eb_ref, nact_ref, x_ref, rwt_ref, wga_ref, wua_ref, wda_ref, wgb_ref, wub_ref, wdb_ref,
                o_ref):
    j = pl.program_id(0)

    @pl.when(j < nact_ref[0])
    def _():
        x = _tt_load(x_ref, FFN_BLOCK, wga_ref.shape[0])
        sa = _sigmoid(jnp.sum(x * rwt_ref[pl.ds(ea_ref[j], 1), :], axis=-1, keepdims=True))
        sb = _sigmoid(jnp.sum(x * rwt_ref[pl.ds(eb_ref[j], 1), :], axis=-1, keepdims=True))
        inv = 1.0 / (sa + sb)
        xb = x.astype(BF16)
        aa = _silu(_dot(xb, wga_ref[...])) * _dot(xb, wua_ref[...]) * (sa * inv)
        ab = _silu(_dot(xb, wgb_ref[...])) * _dot(xb, wub_ref[...]) * (sb * inv)
        _tt_store(o_ref, _dot(aa, wda_ref[...]) + _dot(ab, wdb_ref[...]))

    @pl.when(j >= nact_ref[0])
    def _():
        o_ref[...] = jnp.zeros_like(o_ref)


def _ffn(xs, blk_ea, blk_eb, n_active, router_wt, w_gate, w_up, w_down):
    D, F = w_gate.shape[1:]
    rpt = D // LANES
    nb = xs.shape[0] // (FFN_BLOCK * rpt)

    def wa(j, ea, eb, na):
        return (ea[j], 0, 0)

    def wb(j, ea, eb, na):
        return (eb[j], 0, 0)

    row = pl.BlockSpec((FFN_BLOCK * rpt, LANES), lambda j, ea, eb, na: (j, 0))
    return pl.pallas_call(
        _ffn_kernel,
        grid_spec=pltpu.PrefetchScalarGridSpec(
            num_scalar_prefetch=3, grid=(nb,),
            in_specs=[row, pl.BlockSpec(router_wt.shape, lambda j, ea, eb, na: (0, 0)),
                      pl.BlockSpec((None, D, F), wa), pl.BlockSpec((None, D, F), wa), pl.BlockSpec((None, F, D), wa),
                      pl.BlockSpec((None, D, F), wb), pl.BlockSpec((None, D, F), wb), pl.BlockSpec((None, F, D), wb)],
            out_specs=row),
        out_shape=jax.ShapeDtypeStruct(xs.shape, F32),
        compiler_params=_cparams(),
        name="moe_ffn",
    )(blk_ea, blk_eb, n_active, xs, router_wt, w_gate, w_up, w_down, w_gate, w_up, w_down)


def _moe(layer, x1, mod, g, router_w, router_bias, w_gate, w_up, w_down, T, xs_spare):
    N, D = x1.shape
    h2, cls, rank, cnt = _route(x1, mod, g, router_w, router_bias, T)
    counts = cnt.astype(jnp.int32)
    padded = (counts + FFN_BLOCK - 1) // FFN_BLOCK * FFN_BLOCK
    ends = jnp.cumsum(padded)
    starts = ends - padded
    dest = starts[cls] + rank
    nb = N // FFN_BLOCK + N_CLASSES
    n_active = (ends[-1] // FFN_BLOCK).astype(jnp.int32)
    blk = jnp.arange(nb, dtype=jnp.int32)
    blk_start = jnp.minimum(blk, n_active - 1) * FFN_BLOCK
    blk_cls = jnp.sum((ends[None, :] <= blk_start[:, None]).astype(jnp.int32), axis=1)
    blk_cls = jnp.minimum(blk_cls, N_CLASSES - 1)
    pair_a = jnp.array([p[0] for p in PAIRS], jnp.int32)
    pair_b = jnp.array([p[1] for p in PAIRS], jnp.int32)
    grp = blk_cls // len(PAIRS)
    blk_ea = grp * EXPERTS_PER_GROUP + pair_a[blk_cls % len(PAIRS)]
    blk_eb = grp * EXPERTS_PER_GROUP + pair_b[blk_cls % len(PAIRS)]
    rpt = D // LANES
    xs = _scatter_rows(h2, dest, nb * FFN_BLOCK, rpt, xs_spare)
    ys = _ffn(xs, blk_ea, blk_eb, n_active.reshape(1), jnp.transpose(router_w),
              _cast_bf16(w_gate, layer), _cast_bf16(w_up, layer), _cast_bf16(w_down, layer))
    return ys, dest, xs


def _final_kernel(tm, dest_ref, x_ref, ys_ref, pm_ref, g_ref, o_ref, buf, sem):
    x = x_ref[...]
    y, _, _ = _gathered_rows(dest_ref, ys_ref, buf, sem, tm, x.shape[1], burst=True)
    x = x + pm_ref[5:6, :] * y
    ms = jnp.mean(x * x, axis=-1, keepdims=True)
    o_ref[...] = x * lax.rsqrt(ms + EPS) * g_ref[...]


def _final(x, prev, g, T):
    ys, dest, mod = prev
    N, D = x.shape
    tm = TM_FINAL
    row = pl.BlockSpec((tm, D), lambda i, *_: (i, 0))
    modspec = pl.BlockSpec((None, 6, D), lambda i, *_: ((i * tm) // T, 0, 0))
    return pl.pallas_call(
        functools.partial(_final_kernel, tm),
        grid_spec=pltpu.PrefetchScalarGridSpec(
            num_scalar_prefetch=1, grid=(N // tm,),
            in_specs=[row, pl.BlockSpec(memory_space=pl.ANY), modspec, _full((1, D))],
            out_specs=row,
            scratch_shapes=[pltpu.VMEM((2, tm * D // LANES, LANES), F32), pltpu.SemaphoreType.DMA((2,))]),
        out_shape=jax.ShapeDtypeStruct((N, D), F32),
        compiler_params=_cparams(),
        name="final_norm",
    )(dest, x, ys, mod, g.reshape(1, D))


def kernel(x, c, ada_w, ada_b, norm_g, final_norm_g, pool_w_in, pool_w_grp, pool_scale, pool_w_out, hgrn_w_in, hgrn_lb_logits, hgrn_norm_g, hgrn_w_out, swa_w_in, swa_sinks, swa_w_out, conv_w_in, conv_w, conv_w_out, router_w, router_bias, moe_w_gate, moe_w_up, moe_w_down):
    B, T, D = x.shape
    depth = ada_w.shape[0]
    n_mixers = 4
    assert D % LANES == 0 and all(T % tm == 0 for tm in (TM_POOL, TM_CONV, TM_SWA, TM_HGRN, TM_ROUTER, TM_FINAL))
    assert TM_SWA % SWA_WINDOW == 0 and TM_HGRN % HGRN_CHUNK == 0 and (B * T) % min(PERM_CHUNK, B * T) == 0
    mod = _ada(c, ada_w, ada_b).reshape(depth, B, 6, D)
    xt = x.reshape(B * T, D)
    prev = None
    xs_spare = None
    for i in range(depth):
        m, j = i % n_mixers, i // n_mixers
        g1n = norm_g[i, 0]
        if m == 0:
            x1 = _pool_mixer(xt, prev, mod[i], g1n, pool_w_in[j], pool_w_grp[j], pool_scale[j], pool_w_out[j], T)
        elif m == 1:
            x1 = _hgrn_mixer(i, xt, prev, mod[i], g1n, hgrn_w_in[j], hgrn_lb_logits, hgrn_norm_g[j],
                             hgrn_w_out[j], T)
        elif m == 2:
            x1 = _swa_mixer(xt, prev, mod[i], g1n, swa_w_in[j], swa_sinks[j], swa_w_out[j], T)
        else:
            x1 = _conv_mixer(xt, prev, mod[i], g1n, conv_w_in[j], conv_w[j], conv_w_out[j], T)
        ys, dest, xs_spare = _moe(i, x1, mod[i], norm_g[i, 1], router_w, router_bias,
                                  moe_w_gate, moe_w_up, moe_w_down, T, xs_spare)
        xt = x1
        prev = (ys, dest, mod[i])
    out = _final(xt, prev, final_norm_g, T)
    return out.reshape(B, T, D)
```
